```python
import math
import jax, jax.numpy as jnp
from jax import lax
import numpy as np

D_MODEL = 1024
BATCH = 4
SEQ = 8192
DEPTH = 1
DEC_BATCH = 16
DEC_SEQ = 4096
PAST_LEN = 128

M_HEADS = 4
M_HEAD_DIM = 256
M_WIDTH = M_HEADS * M_HEAD_DIM
M_CHUNK = 128
M_CONV = 3
A_PATTERNS = ((128, 1), (512, 4), (2048, 16))
N_GROUPS = 3
A_HEADS = 8
A_HEAD_DIM = 64
A_WIDTH = A_HEADS * A_HEAD_DIM
A_QBLOCK = 64
N_BUCKETS = 32
MAX_DISTANCE = 1024
EPS = 1e-6
N_BRANCHES = 2
IN_SPLITS = (2 * M_WIDTH, M_WIDTH, M_WIDTH, M_WIDTH, 2 * M_HEADS, 2 * M_HEADS,
             3 * N_GROUPS * A_WIDTH, A_WIDTH, N_BRANCHES * D_MODEL)
IN_COLS = sum(IN_SPLITS)

kernel_name = "hybrid_mlstm_dilated_attn_encoder"


def rms_norm(x, w):
    xf = x.astype(jnp.float32)
    y = xf * lax.rsqrt(jnp.mean(xf * xf, axis=-1, keepdims=True) + EPS)
    return (y * w.astype(jnp.float32)).astype(x.dtype)


def centred_dwconv(x, w, b):
    k = w.shape[0]
    s = x.shape[1]
    pad = k // 2
    xp = jnp.pad(x, ((0, 0), (pad, pad), (0, 0)))
    y = b
    for j in range(k):
        y = y + xp[:, j:j + s] * w[j]
    return y


def mlstm_scan(q, k, v, li, lf):
    n_, h_, s_, e_ = q.shape
    lc = M_CHUNK
    nc = s_ // lc

    def chunks(t):
        return jnp.moveaxis(t.reshape(n_, h_, nc, lc, *t.shape[3:]), 2, 0)

    tril = jnp.tril(jnp.ones((lc, lc), dtype=bool))

    def step(carry, xs):
        c_mat, n_vec, m = carry
        qc, kc, vc, ic, fc = xs
        bc = jnp.cumsum(fc, axis=-1)
        d_log = jnp.where(tril, bc[..., :, None] - bc[..., None, :] + ic[..., None, :], -jnp.inf)
        a = bc + m[..., None]
        mt = jnp.maximum(a, jnp.max(d_log, axis=-1))
        sc = jnp.einsum('nhtk,nhsk->nhts', qc, kc) * jnp.exp(d_log - mt[..., None])
        wa = jnp.exp(a - mt)
        num = jnp.einsum('nhts,nhsv->nhtv', sc, vc) + wa[..., None] * jnp.einsum('nhtk,nhkv->nhtv', qc, c_mat)
        den = jnp.sum(sc, axis=-1) + wa * jnp.einsum('nhtk,nhk->nht', qc, n_vec)
        h = num / jnp.maximum(jnp.abs(den), jnp.exp(-mt))[..., None]
        bl = bc[..., -1]
        g = bl[..., None] - bc + ic
        m_new = jnp.maximum(bl + m, jnp.max(g, axis=-1))
        ws = jnp.exp(g - m_new[..., None])
        decay = jnp.exp(bl + m - m_new)
        kw = kc * ws[..., None]
        c_new = decay[..., None, None] * c_mat + jnp.einsum('nhsk,nhsv->nhkv', kw, vc)
        n_new = decay[..., None] * n_vec + jnp.sum(kw, axis=2)
        return (c_new, n_new, m_new), h

    f32 = jnp.float32
    carry0 = (jnp.zeros((n_, h_, e_, e_), f32), jnp.zeros((n_, h_, e_), f32), jnp.zeros((n_, h_), f32))
    _, hs = lax.scan(step, carry0, (chunks(q), chunks(k), chunks(v), chunks(li), chunks(lf)))
    return jnp.moveaxis(hs, 0, 2).reshape(n_, h_, s_, e_)


def mlstm_mixer(qk_pre, v_pre, o_pre, i_pre, f_pre, conv_w, conv_b, i_bias, f_bias, head_norm_w):
    f32 = jnp.float32
    b_, s_, _ = v_pre.shape
    qk = jax.nn.silu(centred_dwconv(qk_pre.astype(f32), conv_w.astype(f32), conv_b.astype(f32)))

    def heads(t):
        return t.reshape(b_, s_, M_HEADS, M_HEAD_DIM).transpose(0, 2, 1, 3)

    q = heads(qk[..., :M_WIDTH])
    k = heads(qk[..., M_WIDTH:]) * (M_HEAD_DIM ** -0.5)
    v = heads(v_pre.astype(f32))
    li = (i_pre.astype(f32).reshape(b_, s_, 2, M_HEADS) + i_bias.astype(f32)).transpose(2, 0, 3, 1)
    lf = jax.nn.log_sigmoid(f_pre.astype(f32).reshape(b_, s_, 2, M_HEADS) + f_bias.astype(f32)).transpose(2, 0, 3, 1)

    def both(t_fwd, t_bwd):
        return jnp.concatenate([t_fwd, jnp.flip(t_bwd, axis=2)], axis=0)

    h = mlstm_scan(both(q, q), both(k, k), both(v, v), both(li[0], li[1]), both(lf[0], lf[1]))
    h = h[:b_] + jnp.flip(h[b_:], axis=2)
    h = h * lax.rsqrt(jnp.mean(h * h, axis=-1, keepdims=True) + EPS) * head_norm_w.astype(f32)[:, None, :]
    h = h.transpose(0, 2, 1, 3).reshape(b_, s_, M_WIDTH)
    return h * jax.nn.sigmoid(o_pre.astype(f32))


def t5_bucket(rel):
    nb = N_BUCKETS // 2
    exact = nb // 2
    n = np.abs(rel)
    large = exact + (np.log(np.maximum(n, 1) / exact) / math.log(MAX_DISTANCE / exact) * (nb - exact)).astype(np.int32)
    large = np.minimum(large, nb - 1)
    return (rel > 0).astype(np.int32) * nb + np.where(n < exact, n, large)


def dilated_group(q, k, v, dil, half, bias_table):
    b_, s_, h_, e_ = q.shape
    l_ = s_ // dil
    nblk = -(-l_ // A_QBLOCK)
    lp = nblk * A_QBLOCK
    kb_len = A_QBLOCK + 2 * half

    def residue(t):
        return t.reshape(b_, l_, dil, h_, e_).transpose(0, 2, 3, 1, 4)

    qr = jnp.pad(residue(q), ((0, 0),) * 3 + ((0, lp - l_), (0, 0))).reshape(b_, dil, h_, nblk, A_QBLOCK, e_)
    pad_kv = ((0, 0),) * 3 + ((half, lp - l_ + half), (0, 0))
    idx = np.arange(nblk)[:, None] * A_QBLOCK + np.arange(kb_len)[None, :]
    kb = jnp.pad(residue(k), pad_kv)[:, :, :, idx]
    vb = jnp.pad(residue(v), pad_kv)[:, :, :, idx]
    off = np.arange(kb_len)[None, :] - half - np.arange(A_QBLOCK)[:, None]
    key_pos = idx - half
    mask = (np.abs(off) <= half)[None] & ((key_pos >= 0) & (key_pos < l_))[:, None, :]
    bias = bias_table.astype(jnp.float32)[t5_bucket(off * dil)].transpose(2, 0, 1)
    logits = jnp.einsum('bghnqe,bghnke->bghnqk', qr, kb) + bias[:, None]
    logits = jnp.where(mask, logits, -jnp.inf)
    mx = jnp.max(logits, axis=-1)
    p = jnp.exp(logits - mx[..., None])
    den = jnp.sum(p, axis=-1)
    o = jnp.einsum('bghnqk,bghnke->bghnqe', p, vb)

    def back(t):
        tail = t.shape[5:]
        t = t.reshape(b_, dil, h_, lp, *tail)[:, :, :, :l_]
        return jnp.moveaxis(t, 3, 1).reshape(b_, s_, h_, *tail)

    return back(o), back(mx), back(den)


def dilated_attention(qkv, rel_table):
    b_, s_, _ = qkv.shape
    qkv = qkv.astype(jnp.float32).reshape(b_, s_, 3, N_GROUPS, A_HEADS, A_HEAD_DIM)
    outs, maxes, dens = [], [], []
    for g, (win, dil) in enumerate(A_PATTERNS):
        o, mx, den = dilated_group(qkv[:, :, 0, g] * (A_HEAD_DIM ** -0.5), qkv[:, :, 1, g], qkv[:, :, 2, g],
                                   dil, win // (2 * dil), rel_table[:, g])
        outs.append(o)
        maxes.append(mx)
        dens.append(den)
    mx = jnp.stack(maxes)
    w = jnp.exp(mx - jnp.max(mx, axis=0))
    num = jnp.einsum('gbsh,gbshe->bshe', w, jnp.stack(outs))
    den = jnp.sum(w * jnp.stack(dens), axis=0)
    return (num / den[..., None]).reshape(b_, s_, A_WIDTH)


def encoder_layer(x, pre_w, w_in, conv_w, conv_b, i_bias, f_bias, head_norm_w, w_pm, w_pa, w_out, post_w, rel_table):
    b_, s_, _ = x.shape
    h = rms_norm(x, pre_w)
    cols = jnp.einsum('bsd,dc->bsc', h, w_in)
    m_qk, m_v, m_o, m_z, m_i, m_f, a_qkv, a_z, gate = jnp.split(cols, np.cumsum(IN_SPLITS)[:-1], axis=-1)
    y_m = mlstm_mixer(m_qk, m_v, m_o, m_i, m_f, conv_w, conv_b, i_bias, f_bias, head_norm_w) * jax.nn.silu(m_z.astype(jnp.float32))
    y_a = dilated_attention(a_qkv, rel_table) * jax.nn.silu(a_z.astype(jnp.float32))
    p_m = jnp.einsum('bsc,cd->bsd', y_m.astype(x.dtype), w_pm)
    p_a = jnp.einsum('bsc,cd->bsd', y_a.astype(x.dtype), w_pa)
    g = jax.nn.sigmoid(gate.astype(jnp.float32)).reshape(b_, s_, N_BRANCHES, D_MODEL)
    merged = g[:, :, 0] * p_m + g[:, :, 1] * p_a
    out = jnp.einsum('bsd,de->bse', merged.astype(x.dtype), w_out)
    return x + rms_norm(out, post_w)


def setup_inputs(seed: int = 0) -> dict:
    key = jax.random.key(seed)
    ks = jax.random.split(key, 14)
    f32 = jnp.float32
    nrm = lambda k_, shape: jax.random.normal(k_, shape, f32)
    return {
        "x_prompt": nrm(ks[0], (BATCH, SEQ, D_MODEL)),
        "x_sample": nrm(ks[1], (DEC_BATCH, DEC_SEQ, D_MODEL)),
        "pre_norm_w": 1.0 + 0.1 * nrm(ks[2], (DEPTH, D_MODEL)),
        "w_in": nrm(ks[3], (DEPTH, D_MODEL, IN_COLS)) * D_MODEL ** -0.5,
        "m_conv_w": nrm(ks[4], (DEPTH, M_CONV, 2 * M_WIDTH)) * M_CONV ** -0.5,
        "m_conv_b": 0.01 * nrm(ks[5], (DEPTH, 2 * M_WIDTH)),
        "m_igate_b": 0.1 * nrm(ks[6], (DEPTH, 2, M_HEADS)),
        "m_fgate_b": jnp.linspace(3.0, 6.0, M_HEADS, dtype=f32)[None, None, :] + 0.1 * nrm(ks[7], (DEPTH, 2, M_HEADS)),
        "m_head_norm_w": 1.0 + 0.1 * nrm(ks[8], (DEPTH, M_HEADS, M_HEAD_DIM)),
        "w_proj_m": nrm(ks[9], (DEPTH, M_WIDTH, D_MODEL)) * M_WIDTH ** -0.5,
        "w_proj_a": nrm(ks[10], (DEPTH, A_WIDTH, D_MODEL)) * A_WIDTH ** -0.5,
        "w_out": nrm(ks[11], (DEPTH, D_MODEL, D_MODEL)) * D_MODEL ** -0.5,
        "post_norm_w": 1.0 + 0.1 * nrm(ks[12], (DEPTH, D_MODEL)),
        "rel_bias_table": 0.5 * nrm(ks[13], (N_BUCKETS, N_GROUPS, A_HEADS)),
    }


def reference(x_prompt, x_sample, pre_norm_w, w_in, m_conv_w, m_conv_b, m_igate_b, m_fgate_b, m_head_norm_w,
              w_proj_m, w_proj_a, w_out, post_norm_w, rel_bias_table):
    def trunk(x):
        for l in range(DEPTH):
            x = encoder_layer(x, pre_norm_w[l], w_in[l], m_conv_w[l], m_conv_b[l], m_igate_b[l], m_fgate_b[l],
                              m_head_norm_w[l], w_proj_m[l], w_proj_a[l], w_out[l], post_norm_w[l], rel_bias_table)
        return x

    y_prompt = trunk(x_prompt)
    y_sample = trunk(x_sample)
    return (y_prompt, y_sample)
```

```python
import functools
import math

import numpy as np
import jax
import jax.numpy as jnp
from jax import lax
from jax.experimental import pallas as pl
from jax.experimental.pallas import tpu as pltpu

D_MODEL = 1024
M_HEADS = 4
M_HEAD_DIM = 256
M_WIDTH = M_HEADS * M_HEAD_DIM
M_CHUNK = 128
A_PATTERNS = ((128, 1), (512, 4), (2048, 16))
N_GROUPS = 3
A_HEADS = 8
A_HEAD_DIM = 64
A_WIDTH = A_HEADS * A_HEAD_DIM
A_HALF = 64
N_BUCKETS = 32
MAX_DISTANCE = 1024
EPS = 1e-6
NEG = -1e30

OFF_QK = 0
OFF_V = 2 * M_WIDTH
OFF_O = OFF_V + M_WIDTH
OFF_Z = OFF_O + M_WIDTH
OFF_I = OFF_Z + M_WIDTH
OFF_F = OFF_I + 2 * M_HEADS
OFF_AQKV = OFF_F + 2 * M_HEADS
OFF_AZ = OFF_AQKV + 3 * N_GROUPS * A_WIDTH
OFF_GATE = OFF_AZ + A_WIDTH

LANES = 128
F32_SUBLANES = 8
VMEM_LIMIT_BYTES = 56 * 1024 * 1024

PROJ_ROWS = 512
PROJ_HALO = 8
PROJ_COLS = 512
FINAL_ROWS = 512
ATT_QBLOCK = 128
ATT_STEP_ROWS = 512

f32 = jnp.float32
bf16 = jnp.bfloat16


def _const_spec(shape):
    nd = len(shape)
    return pl.BlockSpec(shape, lambda *_: (0,) * nd, pipeline_mode=pl.Buffered(1))


def _dot(a, b):
    return jnp.dot(a, b, preferred_element_type=f32)


def _dot_nt(a, b):
    return lax.dot_general(a, b, (((1,), (1,)), ((), ())), preferred_element_type=f32)


def _dot_tn(a, b):
    return lax.dot_general(a, b, (((0,), (0,)), ((), ())), preferred_element_type=f32)


def _dot_exact(a, b):
    return jnp.dot(a, b, preferred_element_type=f32, precision=lax.Precision.HIGHEST)


def _rms(x, w):
    return x * lax.rsqrt(jnp.mean(x * x, axis=-1, keepdims=True) + EPS) * w


def _proj_body(x_ref, xp_ref, xn_ref, prew_ref, cw_ref, cb_ref, wqk_ref, wv_ref, wif_ref, wift_ref, wa_ref,
               q_ref, k_ref, v_ref, gc_ref, gr_ref, *a_refs):
    i = pl.program_id(1)
    ni = pl.num_programs(1)
    tm = x_ref.shape[1]
    keep_prev = (i > 0).astype(f32)
    keep_next = (i < ni - 1).astype(f32)
    xa = jnp.concatenate([xp_ref[0] * keep_prev, x_ref[0], xn_ref[0] * keep_next], axis=0)
    hf = _rms(xa, prew_ref[...])
    h_ext = hf.astype(bf16)
    hm = hf[PROJ_HALO:PROJ_HALO + tm].astype(bf16)
    rows = tm + 2 * PROJ_HALO

    for c in range(2 * M_WIDTH // PROJ_COLS):
        cs = slice(c * PROJ_COLS, (c + 1) * PROJ_COLS)
        r = _dot(h_ext, wqk_ref[:, cs])
        r_prev = pltpu.roll(r, 1, axis=0)
        r_next = pltpu.roll(r, rows - 1, axis=0)
        w = cw_ref[:, cs]
        y = cb_ref[:, cs] + r_prev * w[0:1] + r * w[1:2] + r_next * w[2:3]
        y = y[PROJ_HALO:PROJ_HALO + tm]
        y = y * jax.nn.sigmoid(y)
        if c * PROJ_COLS < M_WIDTH:
            q_ref[0, :, cs] = y.astype(bf16)
        else:
            ks = slice(c * PROJ_COLS - M_WIDTH, (c + 1) * PROJ_COLS - M_WIDTH)
            k_ref[0, :, ks] = (y * (M_HEAD_DIM ** -0.5)).astype(bf16)

    v_ref[0] = _dot(hm, wv_ref[...]).astype(bf16)
    gc_ref[0] = _dot(hm, wif_ref[...])
    gr_ref[0] = _dot_nt(wift_ref[...], hm)
    for n, a_ref in enumerate(a_refs):
        a_ref[0] = _dot(hm, wa_ref[:, n * A_WIDTH:(n + 1) * A_WIDTH]).astype(bf16)


def _proj(x, pre_w, conv_w, conv_b, wqk, wv, wif, wift, wa):
    b_, s_, _ = x.shape
    tm = PROJ_ROWS
    hb = tm // PROJ_HALO
    n_halo_blocks = s_ // PROJ_HALO
    row_spec = lambda width: pl.BlockSpec((1, tm, width), lambda b, i: (b, i, 0))
    in_specs = [
        row_spec(D_MODEL),
        pl.BlockSpec((1, PROJ_HALO, D_MODEL), lambda b, i: (b, jnp.maximum(i * hb - 1, 0), 0)),
        pl.BlockSpec((1, PROJ_HALO, D_MODEL), lambda b, i: (b, jnp.minimum((i + 1) * hb, n_halo_blocks - 1), 0)),
        _const_spec(pre_w.shape), _const_spec(conv_w.shape), _const_spec(conv_b.shape),
        _const_spec(wqk.shape), _const_spec(wv.shape), _const_spec(wif.shape), _const_spec(wift.shape),
        _const_spec(wa.shape),
    ]
    act = lambda width: jax.ShapeDtypeStruct((b_, s_, width), bf16)
    out_shape = [act(M_WIDTH), act(M_WIDTH), act(M_WIDTH),
                 jax.ShapeDtypeStruct((b_, s_, LANES), f32),
                 jax.ShapeDtypeStruct((b_, 4 * M_HEADS, s_), f32)] + [act(A_WIDTH)] * (3 * N_GROUPS)
    out_specs = [row_spec(M_WIDTH), row_spec(M_WIDTH), row_spec(M_WIDTH), row_spec(LANES),
                 pl.BlockSpec((1, 4 * M_HEADS, tm), lambda b, i: (b, 0, i))] + [row_spec(A_WIDTH)] * (3 * N_GROUPS)
    return pl.pallas_call(
        _proj_body,
        grid=(b_, s_ // tm),
        in_specs=in_specs,
        out_specs=out_specs,
        out_shape=out_shape,
        compiler_params=pltpu.CompilerParams(
            dimension_semantics=("parallel", "parallel"), vmem_limit_bytes=VMEM_LIMIT_BYTES),
        name="proj",
    )(x, x, x, pre_w, conv_w, conv_b, wqk, wv, wif, wift, wa)


def _mlstm_body(qf_ref, kf_ref, vf_ref, gcf_ref, grf_ref, qb_ref, kb_ref, vb_ref, gcb_ref, grb_ref,
                bias_c_ref, bias_r_ref, hf_ref, hb_ref, c_ref, n_ref, m_ref):
    j = pl.program_id(1)
    lc = M_CHUNK
    nh = M_HEADS

    @pl.when(j == 0)
    def _():
        c_ref[...] = jnp.zeros_like(c_ref)
        n_ref[...] = jnp.zeros_like(n_ref)
        m_ref[...] = jnp.zeros_like(m_ref)

    row = lax.broadcasted_iota(jnp.int32, (lc, lc), 0)
    col = lax.broadcasted_iota(jnp.int32, (lc, lc), 1)
    tril = col <= row
    triu = col >= row
    tril_f = tril.astype(f32)
    triu_f = triu.astype(f32)
    lane = lax.broadcasted_iota(jnp.int32, (lc, LANES), 1)
    sub = lax.broadcasted_iota(jnp.int32, (4 * nh, lc), 0)

    for d in range(2):
        q_ref, k_ref, v_ref, gc_ref, gr_ref, out_ref = (
            (qf_ref, kf_ref, vf_ref, gcf_ref, grf_ref, hf_ref) if d == 0
            else (qb_ref, kb_ref, vb_ref, gcb_ref, grb_ref, hb_ref))
        gcol = gc_ref[0] + bias_c_ref[...]
        grow = gr_ref[0] + bias_r_ref[...]
        lcol = jnp.where(lane >= 2 * nh, jax.nn.log_sigmoid(gcol), gcol)
        lrow = jnp.where(sub >= 2 * nh, jax.nn.log_sigmoid(grow), grow)
        if d == 0:
            cum_col = _dot_exact(tril_f, lcol)
            cum_row = _dot_exact(lrow, triu_f)
            mask = tril
        else:
            cum_col = _dot_exact(triu_f, lcol)
            cum_row = _dot_exact(lrow, tril_f)
            mask = triu
        for hh in range(nh):
            ci = d * nh + hh
            cf = 2 * nh + ci
            hs = slice(hh * M_HEAD_DIM, (hh + 1) * M_HEAD_DIM)
            q = q_ref[0, :, hs]
            k = k_ref[0, :, hs]
            v = v_ref[0, :, hs]
            bc_col = cum_col[:, cf:cf + 1]
            i_col = lcol[:, ci:ci + 1]
            bc_row = cum_row[cf:cf + 1, :]
            i_row = lrow[ci:ci + 1, :]
            m_old = m_ref[ci][:, 0:1]

            s = _dot_nt(q, k)
            d_log = jnp.where(mask, bc_col - bc_row + i_row, NEG)
            a = bc_col + m_old
            mt = jnp.maximum(a, jnp.max(d_log, axis=-1, keepdims=True))
            sc = s * jnp.exp(d_log - mt)
            wa = jnp.exp(a - mt)
            c_old = c_ref[ci]
            n_old = n_ref[ci]
            num = _dot(sc.astype(bf16), v) + wa * _dot(q, c_old.astype(bf16))
            den = (jnp.sum(sc, axis=-1, keepdims=True)
                   + wa * jnp.sum(q.astype(f32) * n_old, axis=-1, keepdims=True))
            h = num / jnp.maximum(jnp.abs(den), jnp.exp(-mt))
            out_ref[0, :, hs] = h.astype(bf16)

            bl = bc_col[lc - 1:lc, :] if d == 0 else bc_col[0:1, :]
            g = bl - bc_col + i_col
            m_new = jnp.maximum(bl + m_old, jnp.max(g, axis=0, keepdims=True))
            ws = jnp.exp(g - m_new)
            decay = jnp.exp(bl + m_old - m_new)
            kw = k.astype(f32) * ws
            c_ref[ci] = decay * c_old + _dot_tn(kw.astype(bf16), v)
            n_ref[ci] = decay * n_old + jnp.sum(kw, axis=0, keepdims=True)
            m_ref[ci] = jnp.broadcast_to(m_new, (1, LANES))


def _mlstm(q, k, v, gcol, grow, bias_c, bias_r):
    b_, s_, _ = q.shape
    lc = M_CHUNK
    nc = s_ // lc
    fwd = lambda width: pl.BlockSpec((1, lc, width), lambda b, j: (b, j, 0))
    bwd = lambda width: pl.BlockSpec((1, lc, width), lambda b, j: (b, nc - 1 - j, 0))
    gr_f = pl.BlockSpec((1, 4 * M_HEADS, lc), lambda b, j: (b, 0, j))
    gr_b = pl.BlockSpec((1, 4 * M_HEADS, lc), lambda b, j: (b, 0, nc - 1 - j))
    out = jax.ShapeDtypeStruct((b_, s_, M_WIDTH), bf16)
    return pl.pallas_call(
        _mlstm_body,
        grid=(b_, nc),
        in_specs=[fwd(M_WIDTH), fwd(M_WIDTH), fwd(M_WIDTH), fwd(LANES), gr_f,
                  bwd(M_WIDTH), bwd(M_WIDTH), bwd(M_WIDTH), bwd(LANES), gr_b,
                  _const_spec(bias_c.shape), _const_spec(bias_r.shape)],
        out_specs=[fwd(M_WIDTH), bwd(M_WIDTH)],
        out_shape=[out, out],
        scratch_shapes=[pltpu.VMEM((2 * M_HEADS, M_HEAD_DIM, M_HEAD_DIM), f32),
                        pltpu.VMEM((2 * M_HEADS, 1, M_HEAD_DIM), f32),
                        pltpu.VMEM((2 * M_HEADS, 1, LANES), f32)],
        compiler_params=pltpu.CompilerParams(
            dimension_semantics=("parallel", "arbitrary"), vmem_limit_bytes=VMEM_LIMIT_BYTES),
        name="mlstm",
    )(q, k, v, gcol, grow, q, k, v, gcol, grow, bias_c, bias_r)


def _attn_body(*refs, sub_len, step_rows, final):
    q_ref, k_ref, kp_ref, kn_ref, v_ref, vp_ref, vn_ref, bias_ref = refs[:8]
    if final:
        prev_refs = refs[8:12]
        (y_ref,) = refs[12:]
    else:
        o_ref, st_ref = refs[8:]
    i = pl.program_id(2)
    qb = ATT_QBLOCK
    win = qb + 2 * A_HALF
    npair = A_HEADS // 2

    kwin = jnp.concatenate([kp_ref[0], k_ref[0], kn_ref[0]], axis=0)
    vwin = jnp.concatenate([vp_ref[0], v_ref[0], vn_ref[0]], axis=0)
    lo_q = lax.broadcasted_iota(jnp.int32, (qb, LANES), 1) < A_HEAD_DIM
    lo_w = lax.broadcasted_iota(jnp.int32, (win, LANES), 1) < A_HEAD_DIM
    lane_q = lax.broadcasted_iota(jnp.int32, (qb, LANES), 1)
    zq = jnp.zeros((qb, LANES), bf16)
    zw = jnp.zeros((win, LANES), bf16)

    for blk in range(step_rows // qb):
        r0 = blk * qb
        kpos = i * step_rows + r0 - A_HALF + lax.broadcasted_iota(jnp.int32, (1, win), 1)
        edge = jnp.where((kpos >= 0) & (kpos < sub_len), 0.0, NEG).astype(f32)
        stats = jnp.zeros((qb, LANES), f32)
        for p in range(npair):
            ps = slice(p * LANES, (p + 1) * LANES)
            qp = q_ref[0, r0:r0 + qb, ps]
            kp = kwin[r0:r0 + win, ps]
            vp = vwin[r0:r0 + win, ps]
            q2 = jnp.concatenate([jnp.where(lo_q, qp, zq), jnp.where(lo_q, zq, qp)], axis=0)
            s = _dot_nt(q2, kp) + bias_ref[p] + edge
            mx = jnp.max(s, axis=-1, keepdims=True)
            pe = jnp.exp(s - mx)
            den = jnp.sum(pe, axis=-1, keepdims=True)
            pb = pe.astype(bf16)
            o = (_dot(pb[:qb], jnp.where(lo_w, vp, zw)) + _dot(pb[qb:], jnp.where(lo_w, zw, vp)))
            mx_a, mx_b = mx[:qb], mx[qb:]
            den_a, den_b = den[:qb], den[qb:]
            if final:
                o0_ref, st0_ref, o1_ref, st1_ref = prev_refs
                st0 = st0_ref[0, r0:r0 + qb, :]
                st1 = st1_ref[0, r0:r0 + qb, :]
                ha, hb = 2 * p, 2 * p + 1

                def mix(h_idx, mx2, den2):
                    mx0 = st0[:, h_idx:h_idx + 1]
                    mx1 = st1[:, h_idx:h_idx + 1]
                    dn0 = st0[:, A_HEADS + h_idx:A_HEADS + h_idx + 1]
                    dn1 = st1[:, A_HEADS + h_idx:A_HEADS + h_idx + 1]
                    top = jnp.maximum(jnp.maximum(mx0, mx1), mx2)
                    w0 = jnp.exp(mx0 - top)
                    w1 = jnp.exp(mx1 - top)
                    w2 = jnp.exp(mx2 - top)
                    return w0, w1, w2, w0 * dn0 + w1 * dn1 + w2 * den2

                wa0, wa1, wa2, da = mix(ha, mx_a, den_a)
                wb0, wb1, wb2, db = mix(hb, mx_b, den_b)
                o0 = o0_ref[0, r0:r0 + qb, ps].astype(f32)
                o1 = o1_ref[0, r0:r0 + qb, ps].astype(f32)
                num = (jnp.where(lo_q, wa0, wb0) * o0 + jnp.where(lo_q, wa1, wb1) * o1
                       + jnp.where(lo_q, wa2, wb2) * o)
                y_ref[0, r0:r0 + qb, ps] = (num / jnp.where(lo_q, da, db)).astype(bf16)
            else:
                o_ref[0, r0:r0 + qb, ps] = o.astype(bf16)
                stats = jnp.where(lane_q == 2 * p, mx_a, stats)
                stats = jnp.where(lane_q == 2 * p + 1, mx_b, stats)
                stats = jnp.where(lane_q == A_HEADS + 2 * p, den_a, stats)
                stats = jnp.where(lane_q == A_HEADS + 2 * p + 1, den_b, stats)
        if not final:
            st_ref[0, r0:r0 + qb, :] = stats


def _attn(q, k, v, bias, dil, prev=None):
    b_, s_, _ = q.shape
    sub_len = s_ // dil
    step_rows = min(ATT_STEP_ROWS, sub_len)
    nblk = sub_len // step_rows
    hpb = step_rows // A_HALF
    n_halo_blocks = sub_len // A_HALF
    view = lambda t: t.reshape(b_, sub_len, dil * t.shape[-1])
    main = lambda width: pl.BlockSpec((1, step_rows, width), lambda b, r, i: (b, i, r))
    before = pl.BlockSpec((1, A_HALF, A_WIDTH), lambda b, r, i: (b, jnp.maximum(i * hpb - 1, 0), r))
    after = pl.BlockSpec((1, A_HALF, A_WIDTH),
                         lambda b, r, i: (b, jnp.minimum((i + 1) * hpb, n_halo_blocks - 1), r))
    qv, kv, vv = view(q), view(k), view(v)
    in_specs = [main(A_WIDTH), main(A_WIDTH), before, after, main(A_WIDTH), before, after, _const_spec(bias.shape)]
    args = [qv, kv, kv, kv, vv, vv, vv, bias]
    final = prev is not None
    if final:
        for o_prev, st_prev in prev:
            in_specs += [main(A_WIDTH), main(LANES)]
            args += [view(o_prev), view(st_prev)]
        out_shape = [jax.ShapeDtypeStruct(qv.shape, bf16)]
        out_specs = [main(A_WIDTH)]
    else:
        out_shape = [jax.ShapeDtypeStruct(qv.shape, bf16),
                     jax.ShapeDtypeStruct((b_, sub_len, dil * LANES), f32)]
        out_specs = [main(A_WIDTH), main(LANES)]
    outs = pl.pallas_call(
        functools.partial(_attn_body, sub_len=sub_len, step_rows=step_rows, final=final),
        grid=(b_, dil, nblk),
        in_specs=in_specs,
        out_specs=out_specs,
        out_shape=out_shape,
        compiler_params=pltpu.CompilerParams(
            dimension_semantics=("parallel", "parallel", "parallel"), vmem_limit_bytes=VMEM_LIMIT_BYTES),
        name=f"attn_d{dil}",
    )(*args)
    return [t.reshape(b_, s_, t.shape[-1] // dil) for t in outs]


def _final_body(x_ref, hf_ref, hb_ref, ya_ref, prew_ref, w4_ref, hnw_ref, wpm_ref, wpa_ref, wout_ref, postw_ref,
                y_ref):
    x = x_ref[0]
    h = _rms(x, prew_ref[...]).astype(bf16)
    hsum = hf_ref[0].astype(f32) + hb_ref[0].astype(f32)
    off_z, off_az, off_ga = M_WIDTH, 2 * M_WIDTH, 2 * M_WIDTH + A_WIDTH
    off_gb = off_ga + D_MODEL
    parts = []
    for hh in range(M_HEADS):
        hs = slice(hh * M_HEAD_DIM, (hh + 1) * M_HEAD_DIM)
        o = _dot(h, w4_ref[:, hs])
        z = _dot(h, w4_ref[:, off_z + hh * M_HEAD_DIM:off_z + (hh + 1) * M_HEAD_DIM])
        hx = hsum[:, hs]
        hn = hx * lax.rsqrt(jnp.mean(hx * hx, axis=-1, keepdims=True) + EPS) * hnw_ref[:, hs]
        parts.append((hn * jax.nn.sigmoid(o) * (z * jax.nn.sigmoid(z))).astype(bf16))
    ym = jnp.concatenate(parts, axis=1)
    az = _dot(h, w4_ref[:, off_az:off_az + A_WIDTH])
    ya = (ya_ref[0].astype(f32) * (az * jax.nn.sigmoid(az))).astype(bf16)
    pm = _dot(ym, wpm_ref[...])
    pa = _dot(ya, wpa_ref[...])
    ga = jax.nn.sigmoid(_dot(h, w4_ref[:, off_ga:off_ga + D_MODEL]))
    gb = jax.nn.sigmoid(_dot(h, w4_ref[:, off_gb:off_gb + D_MODEL]))
    merged = (ga * pm + gb * pa).astype(bf16)
    out = _dot(merged, wout_ref[...])
    y_ref[0] = x + _rms(out, postw_ref[...])


def _final(x, hf, hb, ya, pre_w, w4, hnw, wpm, wpa, wout, post_w):
    b_, s_, _ = x.shape
    tm = FINAL_ROWS
    row_spec = lambda width: pl.BlockSpec((1, tm, width), lambda b, i: (b, i, 0))
    consts = [pre_w, w4, hnw, wpm, wpa, wout, post_w]
    return pl.pallas_call(
        _final_body,
        grid=(b_, s_ // tm),
        in_specs=[row_spec(D_MODEL), row_spec(M_WIDTH), row_spec(M_WIDTH), row_spec(A_WIDTH)]
                 + [_const_spec(c.shape) for c in consts],
        out_specs=row_spec(D_MODEL),
        out_shape=jax.ShapeDtypeStruct(x.shape, x.dtype),
        compiler_params=pltpu.CompilerParams(
            dimension_semantics=("parallel", "parallel"), vmem_limit_bytes=VMEM_LIMIT_BYTES),
        name="final",
    )(x, hf, hb, ya, *consts)


def _t5_bucket(rel):
    nb = N_BUCKETS // 2
    exact = nb // 2
    n = np.abs(rel)
    large = exact + (np.log(np.maximum(n, 1) / exact) / math.log(MAX_DISTANCE / exact) * (nb - exact)).astype(np.int32)
    large = np.minimum(large, nb - 1)
    return (rel > 0).astype(np.int32) * nb + np.where(n < exact, n, large)


def _attn_bias(rel_table, g, dil):
    win = ATT_QBLOCK + 2 * A_HALF
    off = np.arange(win)[None, :] - A_HALF - np.arange(ATT_QBLOCK)[:, None]
    band = np.abs(off) <= A_HALF
    bucket = _t5_bucket(off * dil)
    bias = rel_table.astype(f32)[:, g, :][bucket]
    bias = jnp.where(band[:, :, None], bias, NEG).transpose(2, 0, 1)
    return bias.reshape(A_HEADS // 2, 2 * ATT_QBLOCK, win)


def _layer(x, p):
    q, k, v, gcol, grow, *a = _proj(x, p["pre_w"], p["conv_w"], p["conv_b"], p["wqk"], p["wv"], p["wif"],
                                    p["wift"], p["wa"])
    hf, hb = _mlstm(q, k, v, gcol, grow, p["bias_c"], p["bias_r"])
    prev = []
    ya = None
    for g, (_, dil) in enumerate(A_PATTERNS):
        aq, ak, av = a[3 * g:3 * g + 3]
        if g < N_GROUPS - 1:
            prev.append(_attn(aq, ak, av, p["attn_bias"][g], dil))
        else:
            (ya,) = _attn(aq, ak, av, p["attn_bias"][g], dil, prev=prev)
    return _final(x, hf, hb, ya, p["pre_w"], p["w4"], p["hnw"], p["wpm"], p["wpa"], p["wout"], p["post_w"])


def kernel(x_prompt, x_sample, pre_norm_w, w_in, m_conv_w, m_conv_b, m_igate_b, m_fgate_b, m_head_norm_w,
           w_proj_m, w_proj_a, w_out, post_norm_w, rel_bias_table):
    depth = pre_norm_w.shape[0]
    params = []
    for l in range(depth):
        w = w_in[l]
        wif = jnp.concatenate([w[:, OFF_I:OFF_AQKV], jnp.zeros((D_MODEL, LANES - 4 * M_HEADS), w.dtype)], axis=1)
        wa = w[:, OFF_AQKV:OFF_AZ].reshape(D_MODEL, 3, N_GROUPS, A_WIDTH)
        wa = wa * jnp.asarray([A_HEAD_DIM ** -0.5, 1.0, 1.0], w.dtype)[None, :, None, None]
        wa = wa.transpose(0, 2, 1, 3).reshape(D_MODEL, 3 * N_GROUPS * A_WIDTH)
        gate_b = jnp.concatenate([m_igate_b[l].reshape(-1), m_fgate_b[l].reshape(-1)]).astype(f32)
        params.append(dict(
            pre_w=pre_norm_w[l].reshape(1, D_MODEL), post_w=post_norm_w[l].reshape(1, D_MODEL),
            conv_w=m_conv_w[l], conv_b=m_conv_b[l].reshape(1, 2 * M_WIDTH),
            wqk=w[:, OFF_QK:OFF_V].astype(bf16), wv=w[:, OFF_V:OFF_O].astype(bf16),
            wif=wif.astype(bf16), wift=w[:, OFF_I:OFF_AQKV].T.astype(bf16), wa=wa.astype(bf16),
            bias_c=jnp.concatenate([gate_b, jnp.zeros((LANES - 4 * M_HEADS,), f32)]).reshape(1, LANES),
            bias_r=gate_b.reshape(4 * M_HEADS, 1),
            w4=jnp.concatenate([w[:, OFF_O:OFF_I], w[:, OFF_AZ:]], axis=1).astype(bf16),
            hnw=m_head_norm_w[l].reshape(1, M_WIDTH),
            wpm=w_proj_m[l].astype(bf16), wpa=w_proj_a[l].astype(bf16), wout=w_out[l].astype(bf16),
            attn_bias=[_attn_bias(rel_bias_table, g, dil) for g, (_, dil) in enumerate(A_PATTERNS)],
        ))

    def trunk(x):
        for p in params:
            x = _layer(x, p)
        return x

    return (trunk(x_prompt), trunk(x_sample))
```

```python
import functools
import math

import numpy as np
import jax
import jax.numpy as jnp
from jax import lax
from jax.experimental import pallas as pl
from jax.experimental.pallas import tpu as pltpu

D_MODEL = 1024
M_HEADS = 4
M_HEAD_DIM = 256
M_WIDTH = M_HEADS * M_HEAD_DIM
M_CHUNK = 128
A_PATTERNS = ((128, 1), (512, 4), (2048, 16))
N_GROUPS = 3
A_HEADS = 8
A_HEAD_DIM = 64
A_WIDTH = A_HEADS * A_HEAD_DIM
A_HALF = 64
N_BUCKETS = 32
MAX_DISTANCE = 1024
EPS = 1e-6
NEG = -1e30

OFF_QK = 0
OFF_V = 2 * M_WIDTH
OFF_O = OFF_V + M_WIDTH
OFF_Z = OFF_O + M_WIDTH
OFF_I = OFF_Z + M_WIDTH
OFF_F = OFF_I + 2 * M_HEADS
OFF_AQKV = OFF_F + 2 * M_HEADS
OFF_AZ = OFF_AQKV + 3 * N_GROUPS * A_WIDTH
OFF_GATE = OFF_AZ + A_WIDTH

LANES = 128
F32_SUBLANES = 8
VMEM_LIMIT_BYTES = 56 * 1024 * 1024

PROJ_ROWS = 512
PROJ_HALO = 8
PROJ_COLS = 512
FINAL_ROWS = 512
ATT_QBLOCK = 128
ATT_STEP_ROWS = {1: 512, 4: 512, 16: 256}

f32 = jnp.float32
bf16 = jnp.bfloat16


def _const_spec(shape):
    nd = len(shape)
    return pl.BlockSpec(shape, lambda *_: (0,) * nd, pipeline_mode=pl.Buffered(1))


def _dot(a, b):
    return jnp.dot(a, b, preferred_element_type=f32)


def _dot_nt(a, b):
    return lax.dot_general(a, b, (((1,), (1,)), ((), ())), preferred_element_type=f32)


def _dot_tn(a, b):
    return lax.dot_general(a, b, (((0,), (0,)), ((), ())), preferred_element_type=f32)


def _dot_exact(a, b):
    return jnp.dot(a, b, preferred_element_type=f32, precision=lax.Precision.HIGHEST)


def _rms(x, w):
    return x * lax.rsqrt(jnp.mean(x * x, axis=-1, keepdims=True) + EPS) * w


def _proj_body(x_ref, xp_ref, xn_ref, prew_ref, cw_ref, cb_ref, wqk_ref, wv_ref, wif_ref, wift_ref, wa_ref,
               q_ref, k_ref, v_ref, gc_ref, gr_ref, *a_refs):
    i = pl.program_id(1)
    ni = pl.num_programs(1)
    tm = x_ref.shape[1]
    keep_prev = (i > 0).astype(f32)
    keep_next = (i < ni - 1).astype(f32)
    xa = jnp.concatenate([xp_ref[0] * keep_prev, x_ref[0], xn_ref[0] * keep_next], axis=0)
    hf = _rms(xa, prew_ref[...])
    h_ext = hf.astype(bf16)
    hm = hf[PROJ_HALO:PROJ_HALO + tm].astype(bf16)
    rows = tm + 2 * PROJ_HALO

    for c in range(2 * M_WIDTH // PROJ_COLS):
        cs = slice(c * PROJ_COLS, (c + 1) * PROJ_COLS)
        r = _dot(h_ext, wqk_ref[:, cs])
        r_prev = pltpu.roll(r, 1, axis=0)
        r_next = pltpu.roll(r, rows - 1, axis=0)
        w = cw_ref[:, cs]
        y = cb_ref[:, cs] + r_prev * w[0:1] + r * w[1:2] + r_next * w[2:3]
        y = y[PROJ_HALO:PROJ_HALO + tm]
        y = y * jax.nn.sigmoid(y)
        if c * PROJ_COLS < M_WIDTH:
            q_ref[0, :, cs] = y.astype(bf16)
        else:
            ks = slice(c * PROJ_COLS - M_WIDTH, (c + 1) * PROJ_COLS - M_WIDTH)
            k_ref[0, :, ks] = (y * (M_HEAD_DIM ** -0.5)).astype(bf16)

    v_ref[0] = _dot(hm, wv_ref[...]).astype(bf16)
    gc_ref[0] = _dot(hm, wif_ref[...])
    gr_ref[0] = _dot_nt(wift_ref[...], hm)
    a_refs, slab_ref = a_refs[:-1], a_refs[-1]
    nslab = A_WIDTH // LANES
    for n, a_ref in enumerate(a_refs):
        res = _dot(hm, wa_ref[:, n * A_WIDTH:(n + 1) * A_WIDTH])
        dil = A_PATTERNS[n // 3][1]
        if dil == 1:
            a_ref[0] = res.astype(bf16)
            continue
        for sl in range(nslab):
            slab_ref[sl] = res[:, sl * LANES:(sl + 1) * LANES]
        for r in range(dil):
            for sl in range(nslab):
                c0 = r * A_WIDTH + sl * LANES
                a_ref[0, :, c0:c0 + LANES] = slab_ref[sl, pl.ds(r, tm // dil, stride=dil), :].astype(bf16)


def _proj(x, pre_w, conv_w, conv_b, wqk, wv, wif, wift, wa):
    b_, s_, _ = x.shape
    tm = PROJ_ROWS
    hb = tm // PROJ_HALO
    n_halo_blocks = s_ // PROJ_HALO
    row_spec = lambda width: pl.BlockSpec((1, tm, width), lambda b, i: (b, i, 0))
    in_specs = [
        row_spec(D_MODEL),
        pl.BlockSpec((1, PROJ_HALO, D_MODEL), lambda b, i: (b, jnp.maximum(i * hb - 1, 0), 0)),
        pl.BlockSpec((1, PROJ_HALO, D_MODEL), lambda b, i: (b, jnp.minimum((i + 1) * hb, n_halo_blocks - 1), 0)),
        _const_spec(pre_w.shape), _const_spec(conv_w.shape), _const_spec(conv_b.shape),
        _const_spec(wqk.shape), _const_spec(wv.shape), _const_spec(wif.shape), _const_spec(wift.shape),
        _const_spec(wa.shape),
    ]
    act = lambda width: jax.ShapeDtypeStruct((b_, s_, width), bf16)
    out_shape = [act(M_WIDTH), act(M_WIDTH), act(M_WIDTH),
                 jax.ShapeDtypeStruct((b_, s_, LANES), f32),
                 jax.ShapeDtypeStruct((b_, 4 * M_HEADS, s_), f32)]
    out_specs = [row_spec(M_WIDTH), row_spec(M_WIDTH), row_spec(M_WIDTH), row_spec(LANES),
                 pl.BlockSpec((1, 4 * M_HEADS, tm), lambda b, i: (b, 0, i))]
    for _, dil in A_PATTERNS:
        out_shape += [jax.ShapeDtypeStruct((b_, s_ // dil, dil * A_WIDTH), bf16)] * 3
        out_specs += [pl.BlockSpec((1, tm // dil, dil * A_WIDTH), lambda b, i: (b, i, 0))] * 3
    return pl.pallas_call(
        _proj_body,
        grid=(b_, s_ // tm),
        in_specs=in_specs,
        out_specs=out_specs,
        out_shape=out_shape,
        scratch_shapes=[pltpu.VMEM((A_WIDTH // LANES, tm, LANES), f32)],
        compiler_params=pltpu.CompilerParams(
            dimension_semantics=("parallel", "parallel"), vmem_limit_bytes=VMEM_LIMIT_BYTES),
        name="proj",
    )(x, x, x, pre_w, conv_w, conv_b, wqk, wv, wif, wift, wa)


def _mlstm_body(qf_ref, kf_ref, vf_ref, gcf_ref, grf_ref, qb_ref, kb_ref, vb_ref, gcb_ref, grb_ref,
                bias_c_ref, bias_r_ref, hf_ref, hb_ref, c_ref, n_ref, m_ref):
    j = pl.program_id(1)
    lc = M_CHUNK
    nh = M_HEADS

    @pl.when(j == 0)
    def _():
        c_ref[...] = jnp.zeros_like(c_ref)
        n_ref[...] = jnp.zeros_like(n_ref)
        m_ref[...] = jnp.zeros_like(m_ref)

    row = lax.broadcasted_iota(jnp.int32, (lc, lc), 0)
    col = lax.broadcasted_iota(jnp.int32, (lc, lc), 1)
    tril = col <= row
    triu = col >= row
    tril_f = tril.astype(f32)
    triu_f = triu.astype(f32)
    lane = lax.broadcasted_iota(jnp.int32, (lc, LANES), 1)
    sub = lax.broadcasted_iota(jnp.int32, (4 * nh, lc), 0)

    for d in range(2):
        q_ref, k_ref, v_ref, gc_ref, gr_ref, out_ref = (
            (qf_ref, kf_ref, vf_ref, gcf_ref, grf_ref, hf_ref) if d == 0
            else (qb_ref, kb_ref, vb_ref, gcb_ref, grb_ref, hb_ref))
        gcol = gc_ref[0] + bias_c_ref[...]
        grow = gr_ref[0] + bias_r_ref[...]
        lcol = jnp.where(lane >= 2 * nh, jax.nn.log_sigmoid(gcol), gcol)
        lrow = jnp.where(sub >= 2 * nh, jax.nn.log_sigmoid(grow), grow)
        if d == 0:
            cum_col = _dot_exact(tril_f, lcol)
            cum_row = _dot_exact(lrow, triu_f)
            mask = tril
        else:
            cum_col = _dot_exact(triu_f, lcol)
            cum_row = _dot_exact(lrow, tril_f)
            mask = triu
        for hh in range(nh):
            ci = d * nh + hh
            cf = 2 * nh + ci
            hs = slice(hh * M_HEAD_DIM, (hh + 1) * M_HEAD_DIM)
            q = q_ref[0, :, hs]
            k = k_ref[0, :, hs]
            v = v_ref[0, :, hs]
            bc_col = cum_col[:, cf:cf + 1]
            i_col = lcol[:, ci:ci + 1]
            bc_row = cum_row[cf:cf + 1, :]
            i_row = lrow[ci:ci + 1, :]
            m_old = m_ref[ci][:, 0:1]

            s = _dot_nt(q, k)
            d_log = jnp.where(mask, bc_col - bc_row + i_row, NEG)
            a = bc_col + m_old
            mt = jnp.maximum(a, jnp.max(d_log, axis=-1, keepdims=True))
            sc = s * jnp.exp(d_log - mt)
            wa = jnp.exp(a - mt)
            c_old = c_ref[ci]
            n_old = n_ref[ci]
            num = _dot(sc.astype(bf16), v) + wa * _dot(q, c_old.astype(bf16))
            den = (jnp.sum(sc, axis=-1, keepdims=True)
                   + wa * jnp.sum(q.astype(f32) * n_old, axis=-1, keepdims=True))
            h = num / jnp.maximum(jnp.abs(den), jnp.exp(-mt))
            out_ref[0, :, hs] = h.astype(bf16)

            bl = bc_col[lc - 1:lc, :] if d == 0 else bc_col[0:1, :]
            g = bl - bc_col + i_col
            m_new = jnp.maximum(bl + m_old, jnp.max(g, axis=0, keepdims=True))
            ws = jnp.exp(g - m_new)
            decay = jnp.exp(bl + m_old - m_new)
            kw = k.astype(f32) * ws
            c_ref[ci] = decay * c_old + _dot_tn(kw.astype(bf16), v)
            n_ref[ci] = decay * n_old + jnp.sum(kw, axis=0, keepdims=True)
            m_ref[ci] = jnp.broadcast_to(m_new, (1, LANES))


def _mlstm(q, k, v, gcol, grow, bias_c, bias_r):
    b_, s_, _ = q.shape
    lc = M_CHUNK
    nc = s_ // lc
    fwd = lambda width: pl.BlockSpec((1, lc, width), lambda b, j: (b, j, 0))
    bwd = lambda width: pl.BlockSpec((1, lc, width), lambda b, j: (b, nc - 1 - j, 0))
    gr_f = pl.BlockSpec((1, 4 * M_HEADS, lc), lambda b, j: (b, 0, j))
    gr_b = pl.BlockSpec((1, 4 * M_HEADS, lc), lambda b, j: (b, 0, nc - 1 - j))
    out = jax.ShapeDtypeStruct((b_, s_, M_WIDTH), bf16)
    return pl.pallas_call(
        _mlstm_body,
        grid=(b_, nc),
        in_specs=[fwd(M_WIDTH), fwd(M_WIDTH), fwd(M_WIDTH), fwd(LANES), gr_f,
                  bwd(M_WIDTH), bwd(M_WIDTH), bwd(M_WIDTH), bwd(LANES), gr_b,
                  _const_spec(bias_c.shape), _const_spec(bias_r.shape)],
        out_specs=[fwd(M_WIDTH), bwd(M_WIDTH)],
        out_shape=[out, out],
        scratch_shapes=[pltpu.VMEM((2 * M_HEADS, M_HEAD_DIM, M_HEAD_DIM), f32),
                        pltpu.VMEM((2 * M_HEADS, 1, M_HEAD_DIM), f32),
                        pltpu.VMEM((2 * M_HEADS, 1, LANES), f32)],
        compiler_params=pltpu.CompilerParams(
            dimension_semantics=("parallel", "arbitrary"), vmem_limit_bytes=VMEM_LIMIT_BYTES),
        name="mlstm",
    )(q, k, v, gcol, grow, q, k, v, gcol, grow, bias_c, bias_r)


def _attn_block(q_ref, kwin, vwin, bias_ref, r0, first_key, sub_len):
    qb = ATT_QBLOCK
    win = qb + 2 * A_HALF
    lane_q = lax.broadcasted_iota(jnp.int32, (qb, LANES), 1)
    lo_q = lane_q < A_HEAD_DIM
    lo_w = lax.broadcasted_iota(jnp.int32, (win, LANES), 1) < A_HEAD_DIM
    zq = jnp.zeros((qb, LANES), bf16)
    zw = jnp.zeros((win, LANES), bf16)
    kpos = first_key + lax.broadcasted_iota(jnp.int32, (1, win), 1)
    edge = jnp.where((kpos >= 0) & (kpos < sub_len), 0.0, NEG).astype(f32)
    stats = jnp.zeros((qb, LANES), f32)
    outs = []
    for p in range(A_HEADS // 2):
        ps = slice(p * LANES, (p + 1) * LANES)
        qp = q_ref[0, r0:r0 + qb, ps]
        kp = kwin[r0:r0 + win, ps]
        vp = vwin[r0:r0 + win, ps]
        q2 = jnp.concatenate([jnp.where(lo_q, qp, zq), jnp.where(lo_q, zq, qp)], axis=0)
        s = _dot_nt(q2, kp) + bias_ref[p] + edge
        mx = jnp.max(s, axis=-1, keepdims=True)
        pe = jnp.exp(s - mx)
        den = jnp.sum(pe, axis=-1, keepdims=True)
        pb = pe.astype(bf16)
        outs.append(_dot(pb[:qb], jnp.where(lo_w, vp, zw)) + _dot(pb[qb:], jnp.where(lo_w, zw, vp)))
        stats = jnp.where(lane_q == 2 * p, mx[:qb], stats)
        stats = jnp.where(lane_q == 2 * p + 1, mx[qb:], stats)
        stats = jnp.where(lane_q == A_HEADS + 2 * p, den[:qb], stats)
        stats = jnp.where(lane_q == A_HEADS + 2 * p + 1, den[qb:], stats)
    return outs, stats


def _attn_dilated_body(q_ref, k_ref, kp_ref, kn_ref, v_ref, vp_ref, vn_ref, bias_ref, o_ref, st_ref,
                       o_scr, st_scr, *, sub_len, step_rows, dil):
    i = pl.program_id(1)
    r = pl.program_id(2)
    qb = ATT_QBLOCK
    kwin = jnp.concatenate([kp_ref[0], k_ref[0], kn_ref[0]], axis=0)
    vwin = jnp.concatenate([vp_ref[0], v_ref[0], vn_ref[0]], axis=0)
    for blk in range(step_rows // qb):
        r0 = blk * qb
        outs, stats = _attn_block(q_ref, kwin, vwin, bias_ref, r0, i * step_rows + r0 - A_HALF, sub_len)
        rows = pl.ds(r0 * dil + r, qb, stride=dil)
        for p, o in enumerate(outs):
            o_scr[p, rows, :] = o
        st_scr[rows, :] = stats

    @pl.when(r == dil - 1)
    def _():
        for p in range(A_HEADS // 2):
            o_ref[0, :, p * LANES:(p + 1) * LANES] = o_scr[p].astype(bf16)
        st_ref[0] = st_scr[...]


def _attn_mix_body(q_ref, k_ref, kp_ref, kn_ref, v_ref, vp_ref, vn_ref, bias_ref,
                   o1_ref, st1_ref, o2_ref, st2_ref, expand_ref, y_ref, *, sub_len, step_rows):
    i = pl.program_id(1)
    qb = ATT_QBLOCK
    kwin = jnp.concatenate([kp_ref[0], k_ref[0], kn_ref[0]], axis=0)
    vwin = jnp.concatenate([vp_ref[0], v_ref[0], vn_ref[0]], axis=0)
    head_lane = lax.broadcasted_iota(jnp.int32, (qb, LANES), 1) < A_HEADS
    for blk in range(step_rows // qb):
        r0 = blk * qb
        outs, st0 = _attn_block(q_ref, kwin, vwin, bias_ref, r0, i * step_rows + r0 - A_HALF, sub_len)
        st1 = st1_ref[0, r0:r0 + qb, :]
        st2 = st2_ref[0, r0:r0 + qb, :]
        dn0, dn1, dn2 = [pltpu.roll(st, LANES - A_HEADS, axis=1) for st in (st0, st1, st2)]
        top = jnp.maximum(jnp.maximum(st0, st1), st2)
        w0, w1, w2 = jnp.exp(st0 - top), jnp.exp(st1 - top), jnp.exp(st2 - top)
        dsum = w0 * dn0 + w1 * dn1 + w2 * dn2
        y = None
        o_groups = (jnp.concatenate(outs, axis=1),
                    o1_ref[0, r0:r0 + qb, :].astype(f32), o2_ref[0, r0:r0 + qb, :].astype(f32))
        for w, og in zip((w0, w1, w2), o_groups):
            c = jnp.where(head_lane, w / dsum, 0.0)
            c_hi = c.astype(bf16)
            c_lo = (c - c_hi.astype(f32)).astype(bf16)
            wide = _dot(c_hi, expand_ref[...]) + _dot(c_lo, expand_ref[...])
            y = wide * og if y is None else y + wide * og
        y_ref[0, r0:r0 + qb, :] = y.astype(bf16)


def _attn_halo_specs(step_rows, sub_len, width, index):
    hpb = step_rows // A_HALF
    last = sub_len // A_HALF - 1
    before = pl.BlockSpec((1, A_HALF, width), lambda *g: index(g, jnp.maximum(g[1] * hpb - 1, 0)))
    after = pl.BlockSpec((1, A_HALF, width), lambda *g: index(g, jnp.minimum((g[1] + 1) * hpb, last)))
    return before, after


def _attn_dilated(q, k, v, bias, dil):
    b_, sub_len, _ = q.shape
    s_ = sub_len * dil
    step_rows = min(ATT_STEP_ROWS[dil], sub_len)
    tokens = step_rows * dil
    main = pl.BlockSpec((1, step_rows, A_WIDTH), lambda b, i, r: (b, i, r))
    before, after = _attn_halo_specs(step_rows, sub_len, A_WIDTH, lambda g, row: (g[0], row, g[2]))
    return pl.pallas_call(
        functools.partial(_attn_dilated_body, sub_len=sub_len, step_rows=step_rows, dil=dil),
        grid=(b_, sub_len // step_rows, dil),
        in_specs=[main, main, before, after, main, before, after, _const_spec(bias.shape)],
        out_specs=[pl.BlockSpec((1, tokens, A_WIDTH), lambda b, i, r: (b, i, 0)),
                   pl.BlockSpec((1, tokens, LANES), lambda b, i, r: (b, i, 0))],
        out_shape=[jax.ShapeDtypeStruct((b_, s_, A_WIDTH), bf16), jax.ShapeDtypeStruct((b_, s_, LANES), f32)],
        scratch_shapes=[pltpu.VMEM((A_WIDTH // LANES, tokens, LANES), f32), pltpu.VMEM((tokens, LANES), f32)],
        compiler_params=pltpu.CompilerParams(
            dimension_semantics=("parallel", "parallel", "arbitrary"), vmem_limit_bytes=VMEM_LIMIT_BYTES),
        name=f"attn_d{dil}",
    )(q, k, k, k, v, v, v, bias)


def _attn_mix(q, k, v, bias, o1, st1, o2, st2, expand):
    b_, s_, _ = q.shape
    step_rows = min(ATT_STEP_ROWS[1], s_)
    main = lambda width: pl.BlockSpec((1, step_rows, width), lambda b, i: (b, i, 0))
    before, after = _attn_halo_specs(step_rows, s_, A_WIDTH, lambda g, row: (g[0], row, 0))
    return pl.pallas_call(
        functools.partial(_attn_mix_body, sub_len=s_, step_rows=step_rows),
        grid=(b_, s_ // step_rows),
        in_specs=[main(A_WIDTH), main(A_WIDTH), before, after, main(A_WIDTH), before, after, _const_spec(bias.shape),
                  main(A_WIDTH), main(LANES), main(A_WIDTH), main(LANES), _const_spec(expand.shape)],
        out_specs=main(A_WIDTH),
        out_shape=jax.ShapeDtypeStruct((b_, s_, A_WIDTH), bf16),
        compiler_params=pltpu.CompilerParams(
            dimension_semantics=("parallel", "parallel"), vmem_limit_bytes=VMEM_LIMIT_BYTES),
        name="attn_mix",
    )(q, k, k, k, v, v, v, bias, o1, st1, o2, st2, expand)


def _final_body(x_ref, hf_ref, hb_ref, ya_ref, prew_ref, w4_ref, hnw_ref, wpm_ref, wpa_ref, wout_ref, postw_ref,
                y_ref):
    x = x_ref[0]
    h = _rms(x, prew_ref[...]).astype(bf16)
    hsum = hf_ref[0].astype(f32) + hb_ref[0].astype(f32)
    off_z, off_az, off_ga = M_WIDTH, 2 * M_WIDTH, 2 * M_WIDTH + A_WIDTH
    off_gb = off_ga + D_MODEL
    parts = []
    for hh in range(M_HEADS):
        hs = slice(hh * M_HEAD_DIM, (hh + 1) * M_HEAD_DIM)
        o = _dot(h, w4_ref[:, hs])
        z = _dot(h, w4_ref[:, off_z + hh * M_HEAD_DIM:off_z + (hh + 1) * M_HEAD_DIM])
        hx = hsum[:, hs]
        hn = hx * lax.rsqrt(jnp.mean(hx * hx, axis=-1, keepdims=True) + EPS) * hnw_ref[:, hs]
        parts.append((hn * jax.nn.sigmoid(o) * (z * jax.nn.sigmoid(z))).astype(bf16))
    ym = jnp.concatenate(parts, axis=1)
    az = _dot(h, w4_ref[:, off_az:off_az + A_WIDTH])
    ya = (ya_ref[0].astype(f32) * (az * jax.nn.sigmoid(az))).astype(bf16)
    pm = _dot(ym, wpm_ref[...])
    pa = _dot(ya, wpa_ref[...])
    ga = jax.nn.sigmoid(_dot(h, w4_ref[:, off_ga:off_ga + D_MODEL]))
    gb = jax.nn.sigmoid(_dot(h, w4_ref[:, off_gb:off_gb + D_MODEL]))
    merged = (ga * pm + gb * pa).astype(bf16)
    out = _dot(merged, wout_ref[...])
    y_ref[0] = x + _rms(out, postw_ref[...])


def _final(x, hf, hb, ya, pre_w, w4, hnw, wpm, wpa, wout, post_w):
    b_, s_, _ = x.shape
    tm = FINAL_ROWS
    row_spec = lambda width: pl.BlockSpec((1, tm, width), lambda b, i: (b, i, 0))
    consts = [pre_w, w4, hnw, wpm, wpa, wout, post_w]
    return pl.pallas_call(
        _final_body,
        grid=(b_, s_ // tm),
        in_specs=[row_spec(D_MODEL), row_spec(M_WIDTH), row_spec(M_WIDTH), row_spec(A_WIDTH)]
                 + [_const_spec(c.shape) for c in consts],
        out_specs=row_spec(D_MODEL),
        out_shape=jax.ShapeDtypeStruct(x.shape, x.dtype),
        compiler_params=pltpu.CompilerParams(
            dimension_semantics=("parallel", "parallel"), vmem_limit_bytes=VMEM_LIMIT_BYTES),
        name="final",
    )(x, hf, hb, ya, *consts)


def _t5_bucket(rel):
    nb = N_BUCKETS // 2
    exact = nb // 2
    n = np.abs(rel)
    large = exact + (np.log(np.maximum(n, 1) / exact) / math.log(MAX_DISTANCE / exact) * (nb - exact)).astype(np.int32)
    large = np.minimum(large, nb - 1)
    return (rel > 0).astype(np.int32) * nb + np.where(n < exact, n, large)


def _attn_bias(rel_table, g, dil):
    win = ATT_QBLOCK + 2 * A_HALF
    off = np.arange(win)[None, :] - A_HALF - np.arange(ATT_QBLOCK)[:, None]
    band = np.abs(off) <= A_HALF
    bucket = np.where(band, _t5_bucket(off * dil), -1)
    onehot = jnp.asarray(bucket[None] == np.arange(N_BUCKETS)[:, None, None])
    table = rel_table.astype(f32)[:, g, :]
    bias = jnp.sum(jnp.where(onehot[:, None], table[:, :, None, None], 0.0), axis=0)
    bias = jnp.where(band[None], bias, NEG)
    return bias.reshape(A_HEADS // 2, 2 * ATT_QBLOCK, win)


def _head_expand_matrix():
    e = np.zeros((LANES, A_WIDTH), np.float32)
    for h in range(A_HEADS):
        e[h, h * A_HEAD_DIM:(h + 1) * A_HEAD_DIM] = 1.0
    return jnp.asarray(e, bf16)


def _layer(x, p):
    q, k, v, gcol, grow, *a = _proj(x, p["pre_w"], p["conv_w"], p["conv_b"], p["wqk"], p["wv"], p["wif"],
                                    p["wift"], p["wa"])
    hf, hb = _mlstm(q, k, v, gcol, grow, p["bias_c"], p["bias_r"])
    dilated = []
    for g, (_, dil) in enumerate(A_PATTERNS):
        if dil > 1:
            dilated += _attn_dilated(*a[3 * g:3 * g + 3], p["attn_bias"][g], dil)
    ya = _attn_mix(*a[0:3], p["attn_bias"][0], *dilated, _head_expand_matrix())
    return _final(x, hf, hb, ya, p["pre_w"], p["w4"], p["hnw"], p["wpm"], p["wpa"], p["wout"], p["post_w"])


def kernel(x_prompt, x_sample, pre_norm_w, w_in, m_conv_w, m_conv_b, m_igate_b, m_fgate_b, m_head_norm_w,
           w_proj_m, w_proj_a, w_out, post_norm_w, rel_bias_table):
    depth = pre_norm_w.shape[0]
    params = []
    for l in range(depth):
        w = w_in[l]
        wif = jnp.concatenate([w[:, OFF_I:OFF_AQKV], jnp.zeros((D_MODEL, LANES - 4 * M_HEADS), w.dtype)], axis=1)
        wa = w[:, OFF_AQKV:OFF_AZ].reshape(D_MODEL, 3, N_GROUPS, A_WIDTH)
        wa = wa * jnp.asarray([A_HEAD_DIM ** -0.5, 1.0, 1.0], w.dtype)[None, :, None, None]
        wa = wa.transpose(0, 2, 1, 3).reshape(D_MODEL, 3 * N_GROUPS * A_WIDTH)
        gate_b = jnp.concatenate([m_igate_b[l].reshape(-1), m_fgate_b[l].reshape(-1)]).astype(f32)
        params.append(dict(
            pre_w=pre_norm_w[l].reshape(1, D_MODEL), post_w=post_norm_w[l].reshape(1, D_MODEL),
            conv_w=m_conv_w[l], conv_b=m_conv_b[l].reshape(1, 2 * M_WIDTH),
            wqk=w[:, OFF_QK:OFF_V].astype(bf16), wv=w[:, OFF_V:OFF_O].astype(bf16),
            wif=wif.astype(bf16), wift=w[:, OFF_I:OFF_AQKV].T.astype(bf16), wa=wa.astype(bf16),
            bias_c=jnp.concatenate([gate_b, jnp.zeros((LANES - 4 * M_HEADS,), f32)]).reshape(1, LANES),
            bias_r=gate_b.reshape(4 * M_HEADS, 1),
            w4=jnp.concatenate([w[:, OFF_O:OFF_I], w[:, OFF_AZ:]], axis=1).astype(bf16),
            hnw=m_head_norm_w[l].reshape(1, M_WIDTH),
            wpm=w_proj_m[l].astype(bf16), wpa=w_proj_a[l].astype(bf16), wout=w_out[l].astype(bf16),
            attn_bias=[_attn_bias(rel_bias_table, g, dil) for g, (_, dil) in enumerate(A_PATTERNS)],
        ))

    def trunk(x):
        for p in params:
            x = _layer(x, p)
        return x

    return (trunk(x_prompt), trunk(x_sample))
```

```python
import functools
import math

import numpy as np
import jax
import jax.numpy as jnp
from jax import lax
from jax.experimental import pallas as pl
from jax.experimental.pallas import tpu as pltpu

D_MODEL = 1024
M_HEADS = 4
M_HEAD_DIM = 256
M_WIDTH = M_HEADS * M_HEAD_DIM
M_CHUNK = 128
A_PATTERNS = ((128, 1), (512, 4), (2048, 16))
N_GROUPS = 3
A_HEADS = 8
A_HEAD_DIM = 64
A_WIDTH = A_HEADS * A_HEAD_DIM
A_HALF = 64
N_BUCKETS = 32
MAX_DISTANCE = 1024
EPS = 1e-6
NEG = -1e30

OFF_QK = 0
OFF_V = 2 * M_WIDTH
OFF_O = OFF_V + M_WIDTH
OFF_Z = OFF_O + M_WIDTH
OFF_I = OFF_Z + M_WIDTH
OFF_F = OFF_I + 2 * M_HEADS
OFF_AQKV = OFF_F + 2 * M_HEADS
OFF_AZ = OFF_AQKV + 3 * N_GROUPS * A_WIDTH
OFF_GATE = OFF_AZ + A_WIDTH

LANES = 128
F32_SUBLANES = 8
VMEM_LIMIT_BYTES = 56 * 1024 * 1024

GATE_ROWS = 32
PROJ_ROWS = 512
PROJ_HALO = 8
PROJ_COLS = 512
FINAL_ROWS = 512
ATT_QBLOCK = 128
ATT_STEP_ROWS = {1: 512, 4: 512, 16: 256}

f32 = jnp.float32
bf16 = jnp.bfloat16


def _const_spec(shape):
    nd = len(shape)
    return pl.BlockSpec(shape, lambda *_: (0,) * nd, pipeline_mode=pl.Buffered(1))


def _dot(a, b):
    return jnp.dot(a, b, preferred_element_type=f32)


def _dot_nt(a, b):
    return lax.dot_general(a, b, (((1,), (1,)), ((), ())), preferred_element_type=f32)


def _dot_tn(a, b):
    return lax.dot_general(a, b, (((0,), (0,)), ((), ())), preferred_element_type=f32)


def _dot_exact(a, b):
    return jnp.dot(a, b, preferred_element_type=f32, precision=lax.Precision.HIGHEST)


def _rms(x, w):
    return x * lax.rsqrt(jnp.mean(x * x, axis=-1, keepdims=True) + EPS) * w


def _proj_body(x_ref, xp_ref, xn_ref, prew_ref, cw_ref, cb_ref, wqk_ref, wv_ref, wift_ref, gb_ref, wa_ref,
               q_ref, kt_ref, v_ref, gr_ref, *a_refs):
    i = pl.program_id(1)
    ni = pl.num_programs(1)
    tm = x_ref.shape[1]
    keep_prev = (i > 0).astype(f32)
    keep_next = (i < ni - 1).astype(f32)
    xa = jnp.concatenate([xp_ref[0] * keep_prev, x_ref[0], xn_ref[0] * keep_next], axis=0)
    hf = _rms(xa, prew_ref[...])
    h_ext = hf.astype(bf16)
    hm = hf[PROJ_HALO:PROJ_HALO + tm].astype(bf16)
    rows = tm + 2 * PROJ_HALO

    for c in range(2 * M_WIDTH // PROJ_COLS):
        cs = slice(c * PROJ_COLS, (c + 1) * PROJ_COLS)
        r = _dot(h_ext, wqk_ref[:, cs])
        r_prev = pltpu.roll(r, 1, axis=0)
        r_next = pltpu.roll(r, rows - 1, axis=0)
        w = cw_ref[:, cs]
        y = cb_ref[:, cs] + r_prev * w[0:1] + r * w[1:2] + r_next * w[2:3]
        y = y[PROJ_HALO:PROJ_HALO + tm]
        y = y * jax.nn.sigmoid(y)
        if c * PROJ_COLS < M_WIDTH:
            q_ref[0, :, cs] = y.astype(bf16)
        else:
            ks = slice(c * PROJ_COLS - M_WIDTH, (c + 1) * PROJ_COLS - M_WIDTH)
            kt_ref[0, ks, :] = (y * (M_HEAD_DIM ** -0.5)).T.astype(bf16)

    v_ref[0] = _dot(hm, wv_ref[...]).astype(bf16)

    ns = 2 * M_HEADS
    lc = M_CHUNK
    gpre = _dot_nt(wift_ref[...], hm) + gb_ref[...]
    sub = lax.broadcasted_iota(jnp.int32, (2 * ns, tm), 0)
    lrow = jnp.where(sub >= ns, jax.nn.log_sigmoid(gpre), gpre)
    tri_r = lax.broadcasted_iota(jnp.int32, (lc, lc), 0)
    tri_c = lax.broadcasted_iota(jnp.int32, (lc, lc), 1)
    upper = (tri_c >= tri_r).astype(f32)
    lower = (tri_c <= tri_r).astype(f32)
    fwd_rows = lax.broadcasted_iota(jnp.int32, (ns, lc), 0) < M_HEADS
    for c in range(tm // lc):
        ls = slice(c * lc, (c + 1) * lc)
        blk = lrow[:, ls]
        lf = blk[ns:]
        f_cum = jnp.where(fwd_rows, _dot_exact(lf, upper), _dot_exact(lf, lower))
        f_all = jnp.sum(lf, axis=-1, keepdims=True)
        gr_ref[0, 0:ns, ls] = blk[:ns] - f_cum
        gr_ref[0, ns:2 * ns, ls] = lf
        gr_ref[0, 2 * ns:3 * ns, ls] = jnp.broadcast_to(f_all, (ns, lc))
        gr_ref[0, 3 * ns:, ls] = jnp.zeros((GATE_ROWS - 3 * ns, lc), f32)

    a_refs, slab_ref = a_refs[:-1], a_refs[-1]
    nslab = A_WIDTH // LANES
    for n, a_ref in enumerate(a_refs):
        res = _dot(hm, wa_ref[:, n * A_WIDTH:(n + 1) * A_WIDTH])
        dil = A_PATTERNS[n // 3][1]
        if dil == 1:
            a_ref[0] = res.astype(bf16)
            continue
        for sl in range(nslab):
            slab_ref[sl] = res[:, sl * LANES:(sl + 1) * LANES]
        for r in range(dil):
            for sl in range(nslab):
                c0 = r * A_WIDTH + sl * LANES
                a_ref[0, :, c0:c0 + LANES] = slab_ref[sl, pl.ds(r, tm // dil, stride=dil), :].astype(bf16)


def _proj(x, pre_w, conv_w, conv_b, wqk, wv, wift, gate_b, wa):
    b_, s_, _ = x.shape
    tm = PROJ_ROWS
    hb = tm // PROJ_HALO
    n_halo_blocks = s_ // PROJ_HALO
    row_spec = lambda width: pl.BlockSpec((1, tm, width), lambda b, i: (b, i, 0))
    in_specs = [
        row_spec(D_MODEL),
        pl.BlockSpec((1, PROJ_HALO, D_MODEL), lambda b, i: (b, jnp.maximum(i * hb - 1, 0), 0)),
        pl.BlockSpec((1, PROJ_HALO, D_MODEL), lambda b, i: (b, jnp.minimum((i + 1) * hb, n_halo_blocks - 1), 0)),
        _const_spec(pre_w.shape), _const_spec(conv_w.shape), _const_spec(conv_b.shape),
        _const_spec(wqk.shape), _const_spec(wv.shape), _const_spec(wift.shape), _const_spec(gate_b.shape),
        _const_spec(wa.shape),
    ]
    act = lambda width: jax.ShapeDtypeStruct((b_, s_, width), bf16)
    col_spec = lambda height: pl.BlockSpec((1, height, tm), lambda b, i: (b, 0, i))
    out_shape = [act(M_WIDTH), jax.ShapeDtypeStruct((b_, M_WIDTH, s_), bf16), act(M_WIDTH),
                 jax.ShapeDtypeStruct((b_, GATE_ROWS, s_), f32)]
    out_specs = [row_spec(M_WIDTH), col_spec(M_WIDTH), row_spec(M_WIDTH), col_spec(GATE_ROWS)]
    for _, dil in A_PATTERNS:
        out_shape += [jax.ShapeDtypeStruct((b_, s_ // dil, dil * A_WIDTH), bf16)] * 3
        out_specs += [pl.BlockSpec((1, tm // dil, dil * A_WIDTH), lambda b, i: (b, i, 0))] * 3
    return pl.pallas_call(
        _proj_body,
        grid=(b_, s_ // tm),
        in_specs=in_specs,
        out_specs=out_specs,
        out_shape=out_shape,
        scratch_shapes=[pltpu.VMEM((A_WIDTH // LANES, tm, LANES), f32)],
        compiler_params=pltpu.CompilerParams(
            dimension_semantics=("parallel", "parallel"), vmem_limit_bytes=VMEM_LIMIT_BYTES),
        name="proj",
    )(x, x, x, pre_w, conv_w, conv_b, wqk, wv, wift, gate_b, wa)


def _mlstm_body(qf_ref, ktf_ref, vf_ref, grf_ref, qb_ref, ktb_ref, vb_ref, grb_ref,
                hf_ref, hb_ref, c_ref, n_ref, m_ref):
    j = pl.program_id(1)
    lc = M_CHUNK
    nh = M_HEADS
    ns = 2 * M_HEADS
    e = M_HEAD_DIM

    @pl.when(j == 0)
    def _():
        c_ref[...] = jnp.zeros_like(c_ref)
        n_ref[...] = jnp.zeros_like(n_ref)
        m_ref[...] = jnp.zeros_like(m_ref)

    row = lax.broadcasted_iota(jnp.int32, (lc, lc), 0)
    col = lax.broadcasted_iota(jnp.int32, (lc, lc), 1)

    streams = []
    for d in range(2):
        q_ref, kt_ref, v_ref, gr_ref, out_ref = (
            (qf_ref, ktf_ref, vf_ref, grf_ref, hf_ref) if d == 0 else (qb_ref, ktb_ref, vb_ref, grb_ref, hb_ref))
        mask = (col <= row) if d == 0 else (col >= row)
        for hh in range(nh):
            streams.append((d * nh + hh, hh, q_ref, kt_ref, v_ref, gr_ref, out_ref, mask))

    def head_cols(hh):
        return slice(hh * e, (hh + 1) * e)

    gated = []
    for ci, hh, q_ref, kt_ref, v_ref, gr_ref, out_ref, mask in streams:
        r_row = gr_ref[0, ci:ci + 1, :]
        lf_row = gr_ref[0, ns + ci:ns + ci + 1, :]
        m_sc = m_ref[ci][:, 0:1]
        s_aug = _dot(q_ref[0, :, head_cols(hh)],
                     jnp.concatenate([kt_ref[0, head_cols(hh), :], n_ref[ci].astype(bf16)], axis=1))
        r_mat = jnp.where(mask, r_row, NEG)
        u = jnp.maximum(m_sc, jnp.max(r_mat, axis=-1, keepdims=True))
        f_col = jnp.sum(jnp.where(mask, lf_row, 0.0), axis=-1, keepdims=True)
        p = s_aug[:, :lc] * jnp.exp(r_mat - u)
        wa = jnp.exp(m_sc - u)
        den = jnp.sum(p, axis=-1, keepdims=True) + wa * s_aug[:, lc:]
        inv = 1.0 / jnp.maximum(jnp.abs(den), jnp.exp(-(f_col + u)))
        gated.append((p.astype(bf16), wa, inv))

    for (ci, hh, q_ref, kt_ref, v_ref, gr_ref, out_ref, mask), (p, wa, inv) in zip(streams, gated):
        acc = (_dot(p, v_ref[0, :, head_cols(hh)])
               + wa * _dot(q_ref[0, :, head_cols(hh)], c_ref[ci].astype(bf16)))
        for part in range(e // LANES):
            ps = slice(part * LANES, (part + 1) * LANES)
            out_ref[0, :, hh * e + part * LANES:hh * e + (part + 1) * LANES] = (acc[:, ps] * inv).astype(bf16)

    for ci, hh, q_ref, kt_ref, v_ref, gr_ref, out_ref, mask in streams:
        r_row = gr_ref[0, ci:ci + 1, :]
        f_all = gr_ref[0, 2 * ns + ci:2 * ns + ci + 1, :]
        m_old = m_ref[ci]
        g = f_all + r_row
        m_new = jnp.maximum(f_all + m_old, jnp.max(g, axis=-1, keepdims=True))
        kw = kt_ref[0, head_cols(hh), :].astype(f32) * jnp.exp(g - m_new)
        decay = jnp.exp(f_all + m_old - m_new)[:, 0:1]
        c_ref[ci] = decay * c_ref[ci] + _dot(kw.astype(bf16), v_ref[0, :, head_cols(hh)])
        n_ref[ci] = decay * n_ref[ci] + jnp.sum(kw, axis=-1, keepdims=True)
        m_ref[ci] = m_new


def _mlstm(q, kt, v, grow):
    b_, s_, _ = q.shape
    lc = M_CHUNK
    nc = s_ // lc
    fwd = pl.BlockSpec((1, lc, M_WIDTH), lambda b, j: (b, j, 0))
    bwd = pl.BlockSpec((1, lc, M_WIDTH), lambda b, j: (b, nc - 1 - j, 0))
    fwd_t = lambda height: pl.BlockSpec((1, height, lc), lambda b, j: (b, 0, j))
    bwd_t = lambda height: pl.BlockSpec((1, height, lc), lambda b, j: (b, 0, nc - 1 - j))
    out = jax.ShapeDtypeStruct((b_, s_, M_WIDTH), bf16)
    return pl.pallas_call(
        _mlstm_body,
        grid=(b_, nc),
        in_specs=[fwd, fwd_t(M_WIDTH), fwd, fwd_t(GATE_ROWS), bwd, bwd_t(M_WIDTH), bwd, bwd_t(GATE_ROWS)],
        out_specs=[fwd, bwd],
        out_shape=[out, out],
        scratch_shapes=[pltpu.VMEM((2 * M_HEADS, M_HEAD_DIM, M_HEAD_DIM), f32),
                        pltpu.VMEM((2 * M_HEADS, M_HEAD_DIM, LANES), f32),
                        pltpu.VMEM((2 * M_HEADS, 1, LANES), f32)],
        compiler_params=pltpu.CompilerParams(
            dimension_semantics=("parallel", "arbitrary"), vmem_limit_bytes=VMEM_LIMIT_BYTES),
        name="mlstm",
    )(q, kt, v, grow, q, kt, v, grow)


def _attn_block(q_ref, kwin, vwin, bias_ref, r0, first_key, sub_len):
    qb = ATT_QBLOCK
    win = qb + 2 * A_HALF
    lane_q = lax.broadcasted_iota(jnp.int32, (qb, LANES), 1)
    lo_q = lane_q < A_HEAD_DIM
    lo_w = lax.broadcasted_iota(jnp.int32, (win, LANES), 1) < A_HEAD_DIM
    zq = jnp.zeros((qb, LANES), bf16)
    zw = jnp.zeros((win, LANES), bf16)
    kpos = first_key + lax.broadcasted_iota(jnp.int32, (1, win), 1)
    edge = jnp.where((kpos >= 0) & (kpos < sub_len), 0.0, NEG).astype(f32)
    stats = jnp.zeros((qb, LANES), f32)
    outs = []
    for p in range(A_HEADS // 2):
        ps = slice(p * LANES, (p + 1) * LANES)
        qp = q_ref[0, r0:r0 + qb, ps]
        kp = kwin[r0:r0 + win, ps]
        vp = vwin[r0:r0 + win, ps]
        q2 = jnp.concatenate([jnp.where(lo_q, qp, zq), jnp.where(lo_q, zq, qp)], axis=0)
        s = _dot_nt(q2, kp) + bias_ref[p] + edge
        mx = jnp.max(s, axis=-1, keepdims=True)
        pe = jnp.exp(s - mx)
        den = jnp.sum(pe, axis=-1, keepdims=True)
        pb = pe.astype(bf16)
        outs.append(_dot(pb[:qb], jnp.where(lo_w, vp, zw)) + _dot(pb[qb:], jnp.where(lo_w, zw, vp)))
        stats = jnp.where(lane_q == 2 * p, mx[:qb], stats)
        stats = jnp.where(lane_q == 2 * p + 1, mx[qb:], stats)
        stats = jnp.where(lane_q == A_HEADS + 2 * p, den[:qb], stats)
        stats = jnp.where(lane_q == A_HEADS + 2 * p + 1, den[qb:], stats)
    return outs, stats


def _attn_dilated_body(q_ref, k_ref, kp_ref, kn_ref, v_ref, vp_ref, vn_ref, bias_ref, o_ref, st_ref,
                       o_scr, st_scr, *, sub_len, step_rows, dil):
    i = pl.program_id(1)
    r = pl.program_id(2)
    qb = ATT_QBLOCK
    kwin = jnp.concatenate([kp_ref[0], k_ref[0], kn_ref[0]], axis=0)
    vwin = jnp.concatenate([vp_ref[0], v_ref[0], vn_ref[0]], axis=0)
    for blk in range(step_rows // qb):
        r0 = blk * qb
        outs, stats = _attn_block(q_ref, kwin, vwin, bias_ref, r0, i * step_rows + r0 - A_HALF, sub_len)
        rows = pl.ds(r0 * dil + r, qb, stride=dil)
        for p, o in enumerate(outs):
            o_scr[p, rows, :] = o
        st_scr[rows, :] = stats

    @pl.when(r == dil - 1)
    def _():
        for p in range(A_HEADS // 2):
            o_ref[0, :, p * LANES:(p + 1) * LANES] = o_scr[p].astype(bf16)
        st_ref[0] = st_scr[...]


def _attn_mix_body(q_ref, k_ref, kp_ref, kn_ref, v_ref, vp_ref, vn_ref, bias_ref,
                   o1_ref, st1_ref, o2_ref, st2_ref, expand_ref, y_ref, *, sub_len, step_rows):
    i = pl.program_id(1)
    qb = ATT_QBLOCK
    kwin = jnp.concatenate([kp_ref[0], k_ref[0], kn_ref[0]], axis=0)
    vwin = jnp.concatenate([vp_ref[0], v_ref[0], vn_ref[0]], axis=0)
    head_lane = lax.broadcasted_iota(jnp.int32, (qb, LANES), 1) < A_HEADS
    for blk in range(step_rows // qb):
        r0 = blk * qb
        outs, st0 = _attn_block(q_ref, kwin, vwin, bias_ref, r0, i * step_rows + r0 - A_HALF, sub_len)
        st1 = st1_ref[0, r0:r0 + qb, :]
        st2 = st2_ref[0, r0:r0 + qb, :]
        dn0, dn1, dn2 = [pltpu.roll(st, LANES - A_HEADS, axis=1) for st in (st0, st1, st2)]
        top = jnp.maximum(jnp.maximum(st0, st1), st2)
        w0, w1, w2 = jnp.exp(st0 - top), jnp.exp(st1 - top), jnp.exp(st2 - top)
        dsum = w0 * dn0 + w1 * dn1 + w2 * dn2
        y = None
        o_groups = (jnp.concatenate(outs, axis=1),
                    o1_ref[0, r0:r0 + qb, :].astype(f32), o2_ref[0, r0:r0 + qb, :].astype(f32))
        for w, og in zip((w0, w1, w2), o_groups):
            c = jnp.where(head_lane, w / dsum, 0.0)
            c_hi = c.astype(bf16)
            c_lo = (c - c_hi.astype(f32)).astype(bf16)
            wide = _dot(c_hi, expand_ref[...]) + _dot(c_lo, expand_ref[...])
            y = wide * og if y is None else y + wide * og
        y_ref[0, r0:r0 + qb, :] = y.astype(bf16)


def _attn_halo_specs(step_rows, sub_len, width, index):
    hpb = step_rows // A_HALF
    last = sub_len // A_HALF - 1
    before = pl.BlockSpec((1, A_HALF, width), lambda *g: index(g, jnp.maximum(g[1] * hpb - 1, 0)))
    after = pl.BlockSpec((1, A_HALF, width), lambda *g: index(g, jnp.minimum((g[1] + 1) * hpb, last)))
    return before, after


def _attn_dilated(q, k, v, bias, dil):
    b_, sub_len, _ = q.shape
    s_ = sub_len * dil
    step_rows = min(ATT_STEP_ROWS[dil], sub_len)
    tokens = step_rows * dil
    main = pl.BlockSpec((1, step_rows, A_WIDTH), lambda b, i, r: (b, i, r))
    before, after = _attn_halo_specs(step_rows, sub_len, A_WIDTH, lambda g, row: (g[0], row, g[2]))
    return pl.pallas_call(
        functools.partial(_attn_dilated_body, sub_len=sub_len, step_rows=step_rows, dil=dil),
        grid=(b_, sub_len // step_rows, dil),
        in_specs=[main, main, before, after, main, before, after, _const_spec(bias.shape)],
        out_specs=[pl.BlockSpec((1, tokens, A_WIDTH), lambda b, i, r: (b, i, 0)),
                   pl.BlockSpec((1, tokens, LANES), lambda b, i, r: (b, i, 0))],
        out_shape=[jax.ShapeDtypeStruct((b_, s_, A_WIDTH), bf16), jax.ShapeDtypeStruct((b_, s_, LANES), f32)],
        scratch_shapes=[pltpu.VMEM((A_WIDTH // LANES, tokens, LANES), f32), pltpu.VMEM((tokens, LANES), f32)],
        compiler_params=pltpu.CompilerParams(
            dimension_semantics=("parallel", "parallel", "arbitrary"), vmem_limit_bytes=VMEM_LIMIT_BYTES),
        name=f"attn_d{dil}",
    )(q, k, k, k, v, v, v, bias)


def _attn_mix(q, k, v, bias, o1, st1, o2, st2, expand):
    b_, s_, _ = q.shape
    step_rows = min(ATT_STEP_ROWS[1], s_)
    main = lambda width: pl.BlockSpec((1, step_rows, width), lambda b, i: (b, i, 0))
    before, after = _attn_halo_specs(step_rows, s_, A_WIDTH, lambda g, row: (g[0], row, 0))
    return pl.pallas_call(
        functools.partial(_attn_mix_body, sub_len=s_, step_rows=step_rows),
        grid=(b_, s_ // step_rows),
        in_specs=[main(A_WIDTH), main(A_WIDTH), before, after, main(A_WIDTH), before, after, _const_spec(bias.shape),
                  main(A_WIDTH), main(LANES), main(A_WIDTH), main(LANES), _const_spec(expand.shape)],
        out_specs=main(A_WIDTH),
        out_shape=jax.ShapeDtypeStruct((b_, s_, A_WIDTH), bf16),
        compiler_params=pltpu.CompilerParams(
            dimension_semantics=("parallel", "parallel"), vmem_limit_bytes=VMEM_LIMIT_BYTES),
        name="attn_mix",
    )(q, k, k, k, v, v, v, bias, o1, st1, o2, st2, expand)


def _final_body(x_ref, hf_ref, hb_ref, ya_ref, prew_ref, w4_ref, hnw_ref, wpm_ref, wpa_ref, wout_ref, postw_ref,
                y_ref):
    x = x_ref[0]
    h = _rms(x, prew_ref[...]).astype(bf16)
    hsum = hf_ref[0].astype(f32) + hb_ref[0].astype(f32)
    off_z, off_az, off_ga = M_WIDTH, 2 * M_WIDTH, 2 * M_WIDTH + A_WIDTH
    off_gb = off_ga + D_MODEL
    parts = []
    for hh in range(M_HEADS):
        hs = slice(hh * M_HEAD_DIM, (hh + 1) * M_HEAD_DIM)
        o = _dot(h, w4_ref[:, hs])
        z = _dot(h, w4_ref[:, off_z + hh * M_HEAD_DIM:off_z + (hh + 1) * M_HEAD_DIM])
        hx = hsum[:, hs]
        hn = hx * lax.rsqrt(jnp.mean(hx * hx, axis=-1, keepdims=True) + EPS) * hnw_ref[:, hs]
        parts.append((hn * jax.nn.sigmoid(o) * (z * jax.nn.sigmoid(z))).astype(bf16))
    ym = jnp.concatenate(parts, axis=1)
    az = _dot(h, w4_ref[:, off_az:off_az + A_WIDTH])
    ya = (ya_ref[0].astype(f32) * (az * jax.nn.sigmoid(az))).astype(bf16)
    pm = _dot(ym, wpm_ref[...])
    pa = _dot(ya, wpa_ref[...])
    ga = jax.nn.sigmoid(_dot(h, w4_ref[:, off_ga:off_ga + D_MODEL]))
    gb = jax.nn.sigmoid(_dot(h, w4_ref[:, off_gb:off_gb + D_MODEL]))
    merged = (ga * pm + gb * pa).astype(bf16)
    out = _dot(merged, wout_ref[...])
    y_ref[0] = x + _rms(out, postw_ref[...])


def _final(x, hf, hb, ya, pre_w, w4, hnw, wpm, wpa, wout, post_w):
    b_, s_, _ = x.shape
    tm = FINAL_ROWS
    row_spec = lambda width: pl.BlockSpec((1, tm, width), lambda b, i: (b, i, 0))
    consts = [pre_w, w4, hnw, wpm, wpa, wout, post_w]
    return pl.pallas_call(
        _final_body,
        grid=(b_, s_ // tm),
        in_specs=[row_spec(D_MODEL), row_spec(M_WIDTH), row_spec(M_WIDTH), row_spec(A_WIDTH)]
                 + [_const_spec(c.shape) for c in consts],
        out_specs=row_spec(D_MODEL),
        out_shape=jax.ShapeDtypeStruct(x.shape, x.dtype),
        compiler_params=pltpu.CompilerParams(
            dimension_semantics=("parallel", "parallel"), vmem_limit_bytes=VMEM_LIMIT_BYTES),
        name="final",
    )(x, hf, hb, ya, *consts)


def _t5_bucket(rel):
    nb = N_BUCKETS // 2
    exact = nb // 2
    n = np.abs(rel)
    large = exact + (np.log(np.maximum(n, 1) / exact) / math.log(MAX_DISTANCE / exact) * (nb - exact)).astype(np.int32)
    large = np.minimum(large, nb - 1)
    return (rel > 0).astype(np.int32) * nb + np.where(n < exact, n, large)


def _attn_bias(rel_table, g, dil):
    win = ATT_QBLOCK + 2 * A_HALF
    off = np.arange(win)[None, :] - A_HALF - np.arange(ATT_QBLOCK)[:, None]
    band = np.abs(off) <= A_HALF
    bucket = np.where(band, _t5_bucket(off * dil), -1)
    onehot = jnp.asarray(bucket[None] == np.arange(N_BUCKETS)[:, None, None])
    table = rel_table.astype(f32)[:, g, :]
    bias = jnp.sum(jnp.where(onehot[:, None], table[:, :, None, None], 0.0), axis=0)
    bias = jnp.where(band[None], bias, NEG)
    return bias.reshape(A_HEADS // 2, 2 * ATT_QBLOCK, win)


def _head_expand_matrix():
    e = np.zeros((LANES, A_WIDTH), np.float32)
    for h in range(A_HEADS):
        e[h, h * A_HEAD_DIM:(h + 1) * A_HEAD_DIM] = 1.0
    return jnp.asarray(e, bf16)


def _layer(x, p):
    q, kt, v, grow, *a = _proj(x, p["pre_w"], p["conv_w"], p["conv_b"], p["wqk"], p["wv"], p["wift"], p["bias_r"],
                               p["wa"])
    hf, hb = _mlstm(q, kt, v, grow)
    dilated = []
    for g, (_, dil) in enumerate(A_PATTERNS):
        if dil > 1:
            dilated += _attn_dilated(*a[3 * g:3 * g + 3], p["attn_bias"][g], dil)
    ya = _attn_mix(*a[0:3], p["attn_bias"][0], *dilated, _head_expand_matrix())
    return _final(x, hf, hb, ya, p["pre_w"], p["w4"], p["hnw"], p["wpm"], p["wpa"], p["wout"], p["post_w"])


def kernel(x_prompt, x_sample, pre_norm_w, w_in, m_conv_w, m_conv_b, m_igate_b, m_fgate_b, m_head_norm_w,
           w_proj_m, w_proj_a, w_out, post_norm_w, rel_bias_table):
    depth = pre_norm_w.shape[0]
    params = []
    for l in range(depth):
        w = w_in[l]
        wa = w[:, OFF_AQKV:OFF_AZ].reshape(D_MODEL, 3, N_GROUPS, A_WIDTH)
        wa = wa * jnp.asarray([A_HEAD_DIM ** -0.5, 1.0, 1.0], w.dtype)[None, :, None, None]
        wa = wa.transpose(0, 2, 1, 3).reshape(D_MODEL, 3 * N_GROUPS * A_WIDTH)
        gate_b = jnp.concatenate([m_igate_b[l].reshape(-1), m_fgate_b[l].reshape(-1)]).astype(f32)
        params.append(dict(
            pre_w=pre_norm_w[l].reshape(1, D_MODEL), post_w=post_norm_w[l].reshape(1, D_MODEL),
            conv_w=m_conv_w[l], conv_b=m_conv_b[l].reshape(1, 2 * M_WIDTH),
            wqk=w[:, OFF_QK:OFF_V].astype(bf16), wv=w[:, OFF_V:OFF_O].astype(bf16),
            wift=w[:, OFF_I:OFF_AQKV].T.astype(bf16), wa=wa.astype(bf16),
            bias_r=gate_b.reshape(4 * M_HEADS, 1),
            w4=jnp.concatenate([w[:, OFF_O:OFF_I], w[:, OFF_AZ:]], axis=1).astype(bf16),
            hnw=m_head_norm_w[l].reshape(1, M_WIDTH),
            wpm=w_proj_m[l].astype(bf16), wpa=w_proj_a[l].astype(bf16), wout=w_out[l].astype(bf16),
            attn_bias=[_attn_bias(rel_bias_table, g, dil) for g, (_, dil) in enumerate(A_PATTERNS)],
        ))

    def trunk(x):
        for p in params:
            x = _layer(x, p)
        return x

    return (trunk(x_prompt), trunk(x_sample))
```

```python
import functools
import math

import numpy as np
import jax
import jax.numpy as jnp
from jax import lax
from jax.experimental import pallas as pl
from jax.experimental.pallas import tpu as pltpu

D_MODEL = 1024
M_HEADS = 4
M_HEAD_DIM = 256
M_WIDTH = M_HEADS * M_HEAD_DIM
M_CHUNK = 128
A_PATTERNS = ((128, 1), (512, 4), (2048, 16))
N_GROUPS = 3
A_HEADS = 8
A_HEAD_DIM = 64
A_WIDTH = A_HEADS * A_HEAD_DIM
A_HALF = 64
N_BUCKETS = 32
MAX_DISTANCE = 1024
EPS = 1e-6
NEG = -1e30

OFF_QK = 0
OFF_V = 2 * M_WIDTH
OFF_O = OFF_V + M_WIDTH
OFF_Z = OFF_O + M_WIDTH
OFF_I = OFF_Z + M_WIDTH
OFF_F = OFF_I + 2 * M_HEADS
OFF_AQKV = OFF_F + 2 * M_HEADS
OFF_AZ = OFF_AQKV + 3 * N_GROUPS * A_WIDTH
OFF_GATE = OFF_AZ + A_WIDTH

LANES = 128
F32_SUBLANES = 8
VMEM_LIMIT_BYTES = 56 * 1024 * 1024

GATE_ROWS = 32
PROJ_ROWS = 512
PROJ_HALO = 8
PROJ_COLS = 256
MLSTM_STEP_CHUNKS = 1
FINAL_ROWS = 512
ATT_QBLOCK = 128
ATT_STEP_ROWS = {1: 512, 4: 512, 16: 256}

f32 = jnp.float32
bf16 = jnp.bfloat16


def _const_spec(shape):
    nd = len(shape)
    return pl.BlockSpec(shape, lambda *_: (0,) * nd, pipeline_mode=pl.Buffered(1))


def _dot(a, b):
    return jnp.dot(a, b, preferred_element_type=f32)


def _dot_nt(a, b):
    return lax.dot_general(a, b, (((1,), (1,)), ((), ())), preferred_element_type=f32)


def _dot_tn(a, b):
    return lax.dot_general(a, b, (((0,), (0,)), ((), ())), preferred_element_type=f32)


def _dot_exact(a, b):
    return jnp.dot(a, b, preferred_element_type=f32, precision=lax.Precision.HIGHEST)


def _rms(x, w):
    return x * lax.rsqrt(jnp.mean(x * x, axis=-1, keepdims=True) + EPS) * w


def _sigmoid(x):
    return 0.5 + 0.5 * jnp.tanh(0.5 * x)


def _silu(x):
    half = 0.5 * x
    return half + half * jnp.tanh(half)


def _proj_body(x_ref, xp_ref, xn_ref, prew_ref, cw_ref, cb_ref, wqk_ref, wv_ref, wift_ref, gb_ref, wa_ref,
               q_ref, kt_ref, v_ref, gr_ref, *a_refs):
    i = pl.program_id(1)
    ni = pl.num_programs(1)
    tm = x_ref.shape[1]
    keep_prev = (i > 0).astype(f32)
    keep_next = (i < ni - 1).astype(f32)
    xa = jnp.concatenate([xp_ref[0] * keep_prev, x_ref[0], xn_ref[0] * keep_next], axis=0)
    hf = _rms(xa, prew_ref[...])
    h_ext = hf.astype(bf16)
    hm = hf[PROJ_HALO:PROJ_HALO + tm].astype(bf16)
    a_refs, slab_ref = a_refs[:-1], a_refs[-1]
    nslab = A_WIDTH // LANES

    def qk_item(c):
        cs = slice(c * PROJ_COLS, (c + 1) * PROJ_COLS)

        def epilogue(r):
            rows = tm + 2 * PROJ_HALO
            w = cw_ref[:, cs]
            y = (cb_ref[:, cs] + pltpu.roll(r, 1, axis=0) * w[0:1] + r * w[1:2]
                 + pltpu.roll(r, rows - 1, axis=0) * w[2:3])[PROJ_HALO:PROJ_HALO + tm]
            y = _silu(y)
            if c * PROJ_COLS < M_WIDTH:
                q_ref[0, :, cs] = y.astype(bf16)
            else:
                ks = slice(c * PROJ_COLS - M_WIDTH, (c + 1) * PROJ_COLS - M_WIDTH)
                kt_ref[0, ks, :] = (y * (M_HEAD_DIM ** -0.5)).T.astype(bf16)

        return (lambda: _dot(h_ext, wqk_ref[:, cs])), epilogue

    def v_item(c):
        cs = slice(c * PROJ_COLS, (c + 1) * PROJ_COLS)

        def epilogue(res):
            v_ref[0, :, cs] = res.astype(bf16)

        return (lambda: _dot(hm, wv_ref[:, cs])), epilogue

    def attn_item(n):
        a_ref = a_refs[n]
        dil = A_PATTERNS[n // 3][1]

        def epilogue(res):
            if dil == 1:
                a_ref[0] = res.astype(bf16)
                return
            for sl in range(nslab):
                slab_ref[sl] = res[:, sl * LANES:(sl + 1) * LANES]
            for r in range(dil):
                for sl in range(nslab):
                    c0 = r * A_WIDTH + sl * LANES
                    a_ref[0, :, c0:c0 + LANES] = slab_ref[sl, pl.ds(r, tm // dil, stride=dil), :].astype(bf16)

        return (lambda: _dot(hm, wa_ref[:, n * A_WIDTH:(n + 1) * A_WIDTH])), epilogue

    def gate_epilogue(gpre):
        ns = 2 * M_HEADS
        lc = M_CHUNK
        sub = lax.broadcasted_iota(jnp.int32, (2 * ns, tm), 0)
        lrow = jnp.where(sub >= ns, jax.nn.log_sigmoid(gpre + gb_ref[...]), gpre + gb_ref[...])
        tri_r = lax.broadcasted_iota(jnp.int32, (lc, lc), 0)
        tri_c = lax.broadcasted_iota(jnp.int32, (lc, lc), 1)
        upper = (tri_c >= tri_r).astype(f32)
        lower = (tri_c <= tri_r).astype(f32)
        fwd_rows = lax.broadcasted_iota(jnp.int32, (ns, lc), 0) < M_HEADS
        for c in range(tm // lc):
            ls = slice(c * lc, (c + 1) * lc)
            blk = lrow[:, ls]
            lf = blk[ns:]
            f_cum = jnp.where(fwd_rows, _dot_exact(lf, upper), _dot_exact(lf, lower))
            f_all = jnp.sum(lf, axis=-1, keepdims=True)
            gr_ref[0, 0:ns, ls] = blk[:ns] - f_cum
            gr_ref[0, ns:2 * ns, ls] = lf
            gr_ref[0, 2 * ns:3 * ns, ls] = jnp.broadcast_to(f_all, (ns, lc))
            gr_ref[0, 3 * ns:, ls] = jnp.zeros((GATE_ROWS - 3 * ns, lc), f32)

    n_qk = 2 * M_WIDTH // PROJ_COLS
    n_v = M_WIDTH // PROJ_COLS
    light = [attn_item(n) for n in range(len(a_refs))] + [v_item(c) for c in range(n_v)]
    items = [((lambda: _dot_nt(wift_ref[...], hm)), gate_epilogue)]
    for c in range(n_qk):
        items.append(qk_item(c))
        items.append(light.pop())
    items += light

    pending = None
    for matmul, epilogue in items:
        val = matmul()
        if pending is not None:
            pending()
        pending = functools.partial(epilogue, val)
    pending()


def _proj(x, pre_w, conv_w, conv_b, wqk, wv, wift, gate_b, wa):
    b_, s_, _ = x.shape
    tm = PROJ_ROWS
    hb = tm // PROJ_HALO
    n_halo_blocks = s_ // PROJ_HALO
    row_spec = lambda width: pl.BlockSpec((1, tm, width), lambda b, i: (b, i, 0))
    in_specs = [
        row_spec(D_MODEL),
        pl.BlockSpec((1, PROJ_HALO, D_MODEL), lambda b, i: (b, jnp.maximum(i * hb - 1, 0), 0)),
        pl.BlockSpec((1, PROJ_HALO, D_MODEL), lambda b, i: (b, jnp.minimum((i + 1) * hb, n_halo_blocks - 1), 0)),
        _const_spec(pre_w.shape), _const_spec(conv_w.shape), _const_spec(conv_b.shape),
        _const_spec(wqk.shape), _const_spec(wv.shape), _const_spec(wift.shape), _const_spec(gate_b.shape),
        _const_spec(wa.shape),
    ]
    act = lambda width: jax.ShapeDtypeStruct((b_, s_, width), bf16)
    col_spec = lambda height: pl.BlockSpec((1, height, tm), lambda b, i: (b, 0, i))
    out_shape = [act(M_WIDTH), jax.ShapeDtypeStruct((b_, M_WIDTH, s_), bf16), act(M_WIDTH),
                 jax.ShapeDtypeStruct((b_, GATE_ROWS, s_), f32)]
    out_specs = [row_spec(M_WIDTH), col_spec(M_WIDTH), row_spec(M_WIDTH), col_spec(GATE_ROWS)]
    for _, dil in A_PATTERNS:
        out_shape += [jax.ShapeDtypeStruct((b_, s_ // dil, dil * A_WIDTH), bf16)] * 3
        out_specs += [pl.BlockSpec((1, tm // dil, dil * A_WIDTH), lambda b, i: (b, i, 0))] * 3
    return pl.pallas_call(
        _proj_body,
        grid=(b_, s_ // tm),
        in_specs=in_specs,
        out_specs=out_specs,
        out_shape=out_shape,
        scratch_shapes=[pltpu.VMEM((A_WIDTH // LANES, tm, LANES), f32)],
        compiler_params=pltpu.CompilerParams(
            dimension_semantics=("parallel", "parallel"), vmem_limit_bytes=VMEM_LIMIT_BYTES),
        name="proj",
    )(x, x, x, pre_w, conv_w, conv_b, wqk, wv, wift, gate_b, wa)


def _mlstm_body(qf_ref, ktf_ref, vf_ref, grf_ref, qb_ref, ktb_ref, vb_ref, grb_ref,
                hf_ref, hb_ref, c_ref, n_ref, m_ref):
    j = pl.program_id(1)
    lc = M_CHUNK
    nh = M_HEADS
    ns = 2 * M_HEADS
    e = M_HEAD_DIM

    @pl.when(j == 0)
    def _():
        c_ref[...] = jnp.zeros_like(c_ref)
        n_ref[...] = jnp.zeros_like(n_ref)
        m_ref[...] = jnp.zeros_like(m_ref)

    row = lax.broadcasted_iota(jnp.int32, (lc, lc), 0)
    col = lax.broadcasted_iota(jnp.int32, (lc, lc), 1)

    def head_cols(hh):
        return slice(hh * e, (hh + 1) * e)

    for step_chunk in range(MLSTM_STEP_CHUNKS):
        streams = []
        for d in range(2):
            q_ref, kt_ref, v_ref, gr_ref, out_ref = (
                (qf_ref, ktf_ref, vf_ref, grf_ref, hf_ref) if d == 0 else (qb_ref, ktb_ref, vb_ref, grb_ref, hb_ref))
            mask = (col <= row) if d == 0 else (col >= row)
            chunk = step_chunk if d == 0 else MLSTM_STEP_CHUNKS - 1 - step_chunk
            tok = slice(chunk * lc, (chunk + 1) * lc)
            for hh in range(nh):
                streams.append((d * nh + hh, hh, q_ref, kt_ref, v_ref, gr_ref, out_ref, mask, tok))

        gated = []
        for ci, hh, q_ref, kt_ref, v_ref, gr_ref, out_ref, mask, tok in streams:
            r_row = gr_ref[0, ci:ci + 1, tok]
            lf_row = gr_ref[0, ns + ci:ns + ci + 1, tok]
            m_sc = m_ref[ci][:, 0:1]
            s_aug = _dot(q_ref[0, tok, head_cols(hh)],
                         jnp.concatenate([kt_ref[0, head_cols(hh), tok], n_ref[ci].astype(bf16)], axis=1))
            r_mat = jnp.where(mask, r_row, NEG)
            u = jnp.maximum(m_sc, jnp.max(r_mat, axis=-1, keepdims=True))
            f_col = jnp.sum(jnp.where(mask, lf_row, 0.0), axis=-1, keepdims=True)
            p = s_aug[:, :lc] * jnp.exp(r_mat - u)
            wa = jnp.exp(m_sc - u)
            den = jnp.sum(p, axis=-1, keepdims=True) + wa * s_aug[:, lc:]
            inv = 1.0 / jnp.maximum(jnp.abs(den), jnp.exp(-(f_col + u)))
            gated.append((p.astype(bf16), wa, inv))

        for (ci, hh, q_ref, kt_ref, v_ref, gr_ref, out_ref, mask, tok), (p, wa, inv) in zip(streams, gated):
            acc = (_dot(p, v_ref[0, tok, head_cols(hh)])
                   + wa * _dot(q_ref[0, tok, head_cols(hh)], c_ref[ci].astype(bf16)))
            for part in range(e // LANES):
                ps = slice(part * LANES, (part + 1) * LANES)
                out_ref[0, tok, hh * e + part * LANES:hh * e + (part + 1) * LANES] = (acc[:, ps] * inv).astype(bf16)

        for ci, hh, q_ref, kt_ref, v_ref, gr_ref, out_ref, mask, tok in streams:
            r_row = gr_ref[0, ci:ci + 1, tok]
            f_all = gr_ref[0, 2 * ns + ci:2 * ns + ci + 1, tok]
            m_old = m_ref[ci]
            g = f_all + r_row
            m_new = jnp.maximum(f_all + m_old, jnp.max(g, axis=-1, keepdims=True))
            kw = kt_ref[0, head_cols(hh), tok].astype(f32) * jnp.exp(g - m_new)
            decay = jnp.exp(f_all + m_old - m_new)[:, 0:1]
            c_ref[ci] = decay * c_ref[ci] + _dot(kw.astype(bf16), v_ref[0, tok, head_cols(hh)])
            n_ref[ci] = decay * n_ref[ci] + jnp.sum(kw, axis=-1, keepdims=True)
            m_ref[ci] = m_new


def _mlstm(q, kt, v, grow):
    b_, s_, _ = q.shape
    lc = M_CHUNK * MLSTM_STEP_CHUNKS
    nc = s_ // lc
    fwd = pl.BlockSpec((1, lc, M_WIDTH), lambda b, j: (b, j, 0))
    bwd = pl.BlockSpec((1, lc, M_WIDTH), lambda b, j: (b, nc - 1 - j, 0))
    fwd_t = lambda height: pl.BlockSpec((1, height, lc), lambda b, j: (b, 0, j))
    bwd_t = lambda height: pl.BlockSpec((1, height, lc), lambda b, j: (b, 0, nc - 1 - j))
    out = jax.ShapeDtypeStruct((b_, s_, M_WIDTH), bf16)
    return pl.pallas_call(
        _mlstm_body,
        grid=(b_, nc),
        in_specs=[fwd, fwd_t(M_WIDTH), fwd, fwd_t(GATE_ROWS), bwd, bwd_t(M_WIDTH), bwd, bwd_t(GATE_ROWS)],
        out_specs=[fwd, bwd],
        out_shape=[out, out],
        scratch_shapes=[pltpu.VMEM((2 * M_HEADS, M_HEAD_DIM, M_HEAD_DIM), f32),
                        pltpu.VMEM((2 * M_HEADS, M_HEAD_DIM, LANES), f32),
                        pltpu.VMEM((2 * M_HEADS, 1, LANES), f32)],
        compiler_params=pltpu.CompilerParams(
            dimension_semantics=("parallel", "arbitrary"), vmem_limit_bytes=VMEM_LIMIT_BYTES),
        name="mlstm",
    )(q, kt, v, grow, q, kt, v, grow)


def _attn_block(q_ref, kwin, vwin, bias_ref, r0, first_key, sub_len):
    qb = ATT_QBLOCK
    win = qb + 2 * A_HALF
    lane_q = lax.broadcasted_iota(jnp.int32, (qb, LANES), 1)
    lo_q = lane_q < A_HEAD_DIM
    lo_w = lax.broadcasted_iota(jnp.int32, (win, LANES), 1) < A_HEAD_DIM
    zq = jnp.zeros((qb, LANES), bf16)
    zw = jnp.zeros((win, LANES), bf16)
    kpos = first_key + lax.broadcasted_iota(jnp.int32, (1, win), 1)
    edge = jnp.where((kpos >= 0) & (kpos < sub_len), 0.0, NEG).astype(f32)
    stats = jnp.zeros((qb, LANES), f32)
    outs = []
    npair = A_HEADS // 2

    def scores(p):
        ps = slice(p * LANES, (p + 1) * LANES)
        qp = q_ref[0, r0:r0 + qb, ps]
        q2 = jnp.concatenate([jnp.where(lo_q, qp, zq), jnp.where(lo_q, zq, qp)], axis=0)
        return _dot_nt(q2, kwin[r0:r0 + win, ps])

    raw = scores(0)
    for p in range(npair):
        ps = slice(p * LANES, (p + 1) * LANES)
        vp = vwin[r0:r0 + win, ps]
        s = raw + bias_ref[p] + edge
        if p + 1 < npair:
            raw = scores(p + 1)
        mx = jnp.max(s, axis=-1, keepdims=True)
        pe = jnp.exp(s - mx)
        den = jnp.sum(pe, axis=-1, keepdims=True)
        pb = pe.astype(bf16)
        outs.append(_dot(pb[:qb], jnp.where(lo_w, vp, zw)) + _dot(pb[qb:], jnp.where(lo_w, zw, vp)))
        stats = jnp.where(lane_q == 2 * p, mx[:qb], stats)
        stats = jnp.where(lane_q == 2 * p + 1, mx[qb:], stats)
        stats = jnp.where(lane_q == A_HEADS + 2 * p, den[:qb], stats)
        stats = jnp.where(lane_q == A_HEADS + 2 * p + 1, den[qb:], stats)
    return outs, stats


def _attn_dilated_body(q_ref, k_ref, kp_ref, kn_ref, v_ref, vp_ref, vn_ref, bias_ref, o_ref, st_ref,
                       o_scr, st_scr, *, sub_len, step_rows, dil):
    i = pl.program_id(1)
    r = pl.program_id(2)
    qb = ATT_QBLOCK
    kwin = jnp.concatenate([kp_ref[0], k_ref[0], kn_ref[0]], axis=0)
    vwin = jnp.concatenate([vp_ref[0], v_ref[0], vn_ref[0]], axis=0)
    for blk in range(step_rows // qb):
        r0 = blk * qb
        outs, stats = _attn_block(q_ref, kwin, vwin, bias_ref, r0, i * step_rows + r0 - A_HALF, sub_len)
        rows = pl.ds(r0 * dil + r, qb, stride=dil)
        for p, o in enumerate(outs):
            o_scr[p, rows, :] = o
        st_scr[rows, :] = stats

    @pl.when(r == dil - 1)
    def _():
        for p in range(A_HEADS // 2):
            o_ref[0, :, p * LANES:(p + 1) * LANES] = o_scr[p].astype(bf16)
        st_ref[0] = st_scr[...]


def _attn_mix_body(q_ref, k_ref, kp_ref, kn_ref, v_ref, vp_ref, vn_ref, bias_ref,
                   o1_ref, st1_ref, o2_ref, st2_ref, expand_ref, y_ref, *, sub_len, step_rows):
    i = pl.program_id(1)
    qb = ATT_QBLOCK
    kwin = jnp.concatenate([kp_ref[0], k_ref[0], kn_ref[0]], axis=0)
    vwin = jnp.concatenate([vp_ref[0], v_ref[0], vn_ref[0]], axis=0)
    head_lane = lax.broadcasted_iota(jnp.int32, (qb, LANES), 1) < A_HEADS
    for blk in range(step_rows // qb):
        r0 = blk * qb
        outs, st0 = _attn_block(q_ref, kwin, vwin, bias_ref, r0, i * step_rows + r0 - A_HALF, sub_len)
        st1 = st1_ref[0, r0:r0 + qb, :]
        st2 = st2_ref[0, r0:r0 + qb, :]
        dn0, dn1, dn2 = [pltpu.roll(st, LANES - A_HEADS, axis=1) for st in (st0, st1, st2)]
        top = jnp.maximum(jnp.maximum(st0, st1), st2)
        w0, w1, w2 = jnp.exp(st0 - top), jnp.exp(st1 - top), jnp.exp(st2 - top)
        dsum = w0 * dn0 + w1 * dn1 + w2 * dn2
        o_groups = (jnp.concatenate(outs, axis=1),
                    o1_ref[0, r0:r0 + qb, :].astype(f32), o2_ref[0, r0:r0 + qb, :].astype(f32))
        coef = []
        for w in (w0, w1, w2):
            c = jnp.where(head_lane, w / dsum, 0.0)
            c_hi = c.astype(bf16)
            coef.append(jnp.concatenate([c_hi, (c - c_hi.astype(f32)).astype(bf16)], axis=1))
        wide = _dot(jnp.concatenate(coef, axis=0), expand_ref[...])
        y = (wide[:qb] * o_groups[0] + wide[qb:2 * qb] * o_groups[1] + wide[2 * qb:] * o_groups[2])
        y_ref[0, r0:r0 + qb, :] = y.astype(bf16)


def _attn_halo_specs(step_rows, sub_len, width, index):
    hpb = step_rows // A_HALF
    last = sub_len // A_HALF - 1
    before = pl.BlockSpec((1, A_HALF, width), lambda *g: index(g, jnp.maximum(g[1] * hpb - 1, 0)))
    after = pl.BlockSpec((1, A_HALF, width), lambda *g: index(g, jnp.minimum((g[1] + 1) * hpb, last)))
    return before, after


def _attn_dilated(q, k, v, bias, dil):
    b_, sub_len, _ = q.shape
    s_ = sub_len * dil
    step_rows = min(ATT_STEP_ROWS[dil], sub_len)
    tokens = step_rows * dil
    main = pl.BlockSpec((1, step_rows, A_WIDTH), lambda b, i, r: (b, i, r))
    before, after = _attn_halo_specs(step_rows, sub_len, A_WIDTH, lambda g, row: (g[0], row, g[2]))
    return pl.pallas_call(
        functools.partial(_attn_dilated_body, sub_len=sub_len, step_rows=step_rows, dil=dil),
        grid=(b_, sub_len // step_rows, dil),
        in_specs=[main, main, before, after, main, before, after, _const_spec(bias.shape)],
        out_specs=[pl.BlockSpec((1, tokens, A_WIDTH), lambda b, i, r: (b, i, 0)),
                   pl.BlockSpec((1, tokens, LANES), lambda b, i, r: (b, i, 0))],
        out_shape=[jax.ShapeDtypeStruct((b_, s_, A_WIDTH), bf16), jax.ShapeDtypeStruct((b_, s_, LANES), f32)],
        scratch_shapes=[pltpu.VMEM((A_WIDTH // LANES, tokens, LANES), f32), pltpu.VMEM((tokens, LANES), f32)],
        compiler_params=pltpu.CompilerParams(
            dimension_semantics=("parallel", "parallel", "arbitrary"), vmem_limit_bytes=VMEM_LIMIT_BYTES),
        name=f"attn_d{dil}",
    )(q, k, k, k, v, v, v, bias)


def _attn_mix(q, k, v, bias, o1, st1, o2, st2, expand):
    b_, s_, _ = q.shape
    step_rows = min(ATT_STEP_ROWS[1], s_)
    main = lambda width: pl.BlockSpec((1, step_rows, width), lambda b, i: (b, i, 0))
    before, after = _attn_halo_specs(step_rows, s_, A_WIDTH, lambda g, row: (g[0], row, 0))
    return pl.pallas_call(
        functools.partial(_attn_mix_body, sub_len=s_, step_rows=step_rows),
        grid=(b_, s_ // step_rows),
        in_specs=[main(A_WIDTH), main(A_WIDTH), before, after, main(A_WIDTH), before, after, _const_spec(bias.shape),
                  main(A_WIDTH), main(LANES), main(A_WIDTH), main(LANES), _const_spec(expand.shape)],
        out_specs=main(A_WIDTH),
        out_shape=jax.ShapeDtypeStruct((b_, s_, A_WIDTH), bf16),
        compiler_params=pltpu.CompilerParams(
            dimension_semantics=("parallel", "parallel"), vmem_limit_bytes=VMEM_LIMIT_BYTES),
        name="attn_mix",
    )(q, k, k, k, v, v, v, bias, o1, st1, o2, st2, expand)


def _final_body(x_ref, hf_ref, hb_ref, ya_ref, prew_ref, w4_ref, hnw_ref, wpm_ref, wpa_ref, wout_ref, postw_ref,
                y_ref):
    x = x_ref[0]
    h = _rms(x, prew_ref[...]).astype(bf16)
    hsum = hf_ref[0].astype(f32) + hb_ref[0].astype(f32)
    off_z, off_az, off_ga = M_WIDTH, 2 * M_WIDTH, 2 * M_WIDTH + A_WIDTH
    off_gb = off_ga + D_MODEL
    parts = []
    for hh in range(M_HEADS):
        hs = slice(hh * M_HEAD_DIM, (hh + 1) * M_HEAD_DIM)
        o = _dot(h, w4_ref[:, hs])
        z = _dot(h, w4_ref[:, off_z + hh * M_HEAD_DIM:off_z + (hh + 1) * M_HEAD_DIM])
        hx = hsum[:, hs]
        hn = hx * lax.rsqrt(jnp.mean(hx * hx, axis=-1, keepdims=True) + EPS) * hnw_ref[:, hs]
        parts.append((hn * _sigmoid(o) * _silu(z)).astype(bf16))
    ym = jnp.concatenate(parts, axis=1)
    az = _dot(h, w4_ref[:, off_az:off_az + A_WIDTH])
    ya = (ya_ref[0].astype(f32) * _silu(az)).astype(bf16)
    pm = _dot(ym, wpm_ref[...])
    pa = _dot(ya, wpa_ref[...])
    ga = _sigmoid(_dot(h, w4_ref[:, off_ga:off_ga + D_MODEL]))
    gb = _sigmoid(_dot(h, w4_ref[:, off_gb:off_gb + D_MODEL]))
    merged = (ga * pm + gb * pa).astype(bf16)
    out = _dot(merged, wout_ref[...])
    y_ref[0] = x + _rms(out, postw_ref[...])


def _final(x, hf, hb, ya, pre_w, w4, hnw, wpm, wpa, wout, post_w):
    b_, s_, _ = x.shape
    tm = FINAL_ROWS
    row_spec = lambda width: pl.BlockSpec((1, tm, width), lambda b, i: (b, i, 0))
    consts = [pre_w, w4, hnw, wpm, wpa, wout, post_w]
    return pl.pallas_call(
        _final_body,
        grid=(b_, s_ // tm),
        in_specs=[row_spec(D_MODEL), row_spec(M_WIDTH), row_spec(M_WIDTH), row_spec(A_WIDTH)]
                 + [_const_spec(c.shape) for c in consts],
        out_specs=row_spec(D_MODEL),
        out_shape=jax.ShapeDtypeStruct(x.shape, x.dtype),
        compiler_params=pltpu.CompilerParams(
            dimension_semantics=("parallel", "parallel"), vmem_limit_bytes=VMEM_LIMIT_BYTES),
        name="final",
    )(x, hf, hb, ya, *consts)


def _t5_bucket(rel):
    nb = N_BUCKETS // 2
    exact = nb // 2
    n = np.abs(rel)
    large = exact + (np.log(np.maximum(n, 1) / exact) / math.log(MAX_DISTANCE / exact) * (nb - exact)).astype(np.int32)
    large = np.minimum(large, nb - 1)
    return (rel > 0).astype(np.int32) * nb + np.where(n < exact, n, large)


def _attn_bias(rel_table, g, dil):
    win = ATT_QBLOCK + 2 * A_HALF
    off = np.arange(win)[None, :] - A_HALF - np.arange(ATT_QBLOCK)[:, None]
    band = np.abs(off) <= A_HALF
    bucket = np.where(band, _t5_bucket(off * dil), -1)
    onehot = jnp.asarray(bucket[None] == np.arange(N_BUCKETS)[:, None, None])
    table = rel_table.astype(f32)[:, g, :]
    bias = jnp.sum(jnp.where(onehot[:, None], table[:, :, None, None], 0.0), axis=0)
    bias = jnp.where(band[None], bias, NEG)
    return bias.reshape(A_HEADS // 2, 2 * ATT_QBLOCK, win)


def _head_expand_matrix():
    e = np.zeros((2 * LANES, A_WIDTH), np.float32)
    for h in range(A_HEADS):
        e[h, h * A_HEAD_DIM:(h + 1) * A_HEAD_DIM] = 1.0
        e[LANES + h, h * A_HEAD_DIM:(h + 1) * A_HEAD_DIM] = 1.0
    return jnp.asarray(e, bf16)


def _layer(x, p):
    q, kt, v, grow, *a = _proj(x, p["pre_w"], p["conv_w"], p["conv_b"], p["wqk"], p["wv"], p["wift"], p["bias_r"],
                               p["wa"])
    hf, hb = _mlstm(q, kt, v, grow)
    dilated = []
    for g, (_, dil) in enumerate(A_PATTERNS):
        if dil > 1:
            dilated += _attn_dilated(*a[3 * g:3 * g + 3], p["attn_bias"][g], dil)
    ya = _attn_mix(*a[0:3], p["attn_bias"][0], *dilated, _head_expand_matrix())
    return _final(x, hf, hb, ya, p["pre_w"], p["w4"], p["hnw"], p["wpm"], p["wpa"], p["wout"], p["post_w"])


def kernel(x_prompt, x_sample, pre_norm_w, w_in, m_conv_w, m_conv_b, m_igate_b, m_fgate_b, m_head_norm_w,
           w_proj_m, w_proj_a, w_out, post_norm_w, rel_bias_table):
    depth = pre_norm_w.shape[0]
    params = []
    for l in range(depth):
        w = w_in[l]
        wa = w[:, OFF_AQKV:OFF_AZ].reshape(D_MODEL, 3, N_GROUPS, A_WIDTH)
        wa = wa * jnp.asarray([A_HEAD_DIM ** -0.5, 1.0, 1.0], w.dtype)[None, :, None, None]
        wa = wa.transpose(0, 2, 1, 3).reshape(D_MODEL, 3 * N_GROUPS * A_WIDTH)
        gate_b = jnp.concatenate([m_igate_b[l].reshape(-1), m_fgate_b[l].reshape(-1)]).astype(f32)
        params.append(dict(
            pre_w=pre_norm_w[l].reshape(1, D_MODEL), post_w=post_norm_w[l].reshape(1, D_MODEL),
            conv_w=m_conv_w[l], conv_b=m_conv_b[l].reshape(1, 2 * M_WIDTH),
            wqk=w[:, OFF_QK:OFF_V].astype(bf16), wv=w[:, OFF_V:OFF_O].astype(bf16),
            wift=w[:, OFF_I:OFF_AQKV].T.astype(bf16), wa=wa.astype(bf16),
            bias_r=gate_b.reshape(4 * M_HEADS, 1),
            w4=jnp.concatenate([w[:, OFF_O:OFF_I], w[:, OFF_AZ:]], axis=1).astype(bf16),
            hnw=m_head_norm_w[l].reshape(1, M_WIDTH),
            wpm=w_proj_m[l].astype(bf16), wpa=w_proj_a[l].astype(bf16), wout=w_out[l].astype(bf16),
            attn_bias=[_attn_bias(rel_bias_table, g, dil) for g, (_, dil) in enumerate(A_PATTERNS)],
        ))

    def trunk(x):
        for p in params:
            x = _layer(x, p)
        return x

    return (trunk(x_prompt), trunk(x_sample))
```

```python
import functools
import math

import numpy as np
import jax
import jax.numpy as jnp
from jax import lax
from jax.experimental import pallas as pl
from jax.experimental.pallas import tpu as pltpu

D_MODEL = 1024
M_HEADS = 4
M_HEAD_DIM = 256
M_WIDTH = M_HEADS * M_HEAD_DIM
M_CHUNK = 128
A_PATTERNS = ((128, 1), (512, 4), (2048, 16))
N_GROUPS = 3
A_HEADS = 8
A_HEAD_DIM = 64
A_WIDTH = A_HEADS * A_HEAD_DIM
A_HALF = 64
N_BUCKETS = 32
MAX_DISTANCE = 1024
EPS = 1e-6
NEG = -1e30
LOG2E = math.log2(math.e)

OFF_QK = 0
OFF_V = 2 * M_WIDTH
OFF_O = OFF_V + M_WIDTH
OFF_Z = OFF_O + M_WIDTH
OFF_I = OFF_Z + M_WIDTH
OFF_F = OFF_I + 2 * M_HEADS
OFF_AQKV = OFF_F + 2 * M_HEADS
OFF_AZ = OFF_AQKV + 3 * N_GROUPS * A_WIDTH
OFF_GATE = OFF_AZ + A_WIDTH

LANES = 128
F32_SUBLANES = 8
STRIDE_ONE_OP = 4
VMEM_LIMIT_BYTES = 56 * 1024 * 1024

GATE_ROWS = 32
PROJ_ROWS = 512
PROJ_HALO = 8
PROJ_COLS = 256
MLSTM_STEP_CHUNKS = 1
FINAL_ROWS = 512
ATT_QBLOCK = 128
ATT_STEP_ROWS = {1: 512, 4: 512, 16: 256}

f32 = jnp.float32
bf16 = jnp.bfloat16


def _const_spec(shape):
    nd = len(shape)
    return pl.BlockSpec(shape, lambda *_: (0,) * nd, pipeline_mode=pl.Buffered(1))


def _dot(a, b):
    return jnp.dot(a, b, preferred_element_type=f32)


def _dot_nt(a, b):
    return lax.dot_general(a, b, (((1,), (1,)), ((), ())), preferred_element_type=f32)


def _dot_tn(a, b):
    return lax.dot_general(a, b, (((0,), (0,)), ((), ())), preferred_element_type=f32)


def _dot_exact(a, b):
    return jnp.dot(a, b, preferred_element_type=f32, precision=lax.Precision.HIGHEST)


def _rms(x, w):
    return x * lax.rsqrt(jnp.mean(x * x, axis=-1, keepdims=True) + EPS) * w


def _sigmoid(x):
    return 0.5 + 0.5 * jnp.tanh(0.5 * x)


def _silu(x):
    half = 0.5 * x
    return half + half * jnp.tanh(half)


def _proj_body(x_ref, xp_ref, xn_ref, prew_ref, cw_ref, cb_ref, wqk_ref, wv_ref, wift_ref, gb_ref, wa_ref,
               q_ref, kt_ref, v_ref, gr_ref, *a_refs):
    i = pl.program_id(1)
    ni = pl.num_programs(1)
    tm = x_ref.shape[1]
    keep_prev = (i > 0).astype(f32)
    keep_next = (i < ni - 1).astype(f32)
    xa = jnp.concatenate([xp_ref[0] * keep_prev, x_ref[0], xn_ref[0] * keep_next], axis=0)
    hf = _rms(xa, prew_ref[...])
    h_ext = hf.astype(bf16)
    hm = hf[PROJ_HALO:PROJ_HALO + tm].astype(bf16)
    a_refs, (slab_ref, slab2_ref, conv_ref, act_ref) = a_refs[:-4], a_refs[-4:]
    nslab = A_WIDTH // LANES

    def qk_item(c):
        cs = slice(c * PROJ_COLS, (c + 1) * PROJ_COLS)

        def epilogue(r):
            half = tm // 2
            for sl in range(PROJ_COLS // LANES):
                conv_ref[sl] = r[:, sl * LANES:(sl + 1) * LANES]
            for sl in range(PROJ_COLS // LANES):
                col = slice(c * PROJ_COLS + sl * LANES, c * PROJ_COLS + (sl + 1) * LANES)
                w = cw_ref[:, col]
                even = conv_ref[sl, pl.ds(PROJ_HALO, half, stride=2), :]
                odd = conv_ref[sl, pl.ds(PROJ_HALO + 1, half, stride=2), :]
                odd_prev = conv_ref[sl, pl.ds(PROJ_HALO - 1, half, stride=2), :]
                even_next = conv_ref[sl, pl.ds(PROJ_HALO + 2, half, stride=2), :]
                y_even = cb_ref[:, col] + odd_prev * w[0:1] + even * w[1:2] + odd * w[2:3]
                y_odd = cb_ref[:, col] + even * w[0:1] + odd * w[1:2] + even_next * w[2:3]
                act_ref[sl, pl.ds(0, half, stride=2), :] = _silu(y_even)
                act_ref[sl, pl.ds(1, half, stride=2), :] = _silu(y_odd)
                if c * PROJ_COLS < M_WIDTH:
                    q_ref[0, :, col] = act_ref[sl].astype(bf16)
                else:
                    rows = slice(col.start - M_WIDTH, col.stop - M_WIDTH)
                    kt_ref[0, rows, :] = (act_ref[sl] * (M_HEAD_DIM ** -0.5)).T.astype(bf16)

        return (lambda: _dot(h_ext, wqk_ref[:, cs])), epilogue

    def v_item(c):
        cs = slice(c * PROJ_COLS, (c + 1) * PROJ_COLS)

        def epilogue(res):
            v_ref[0, :, cs] = res.astype(bf16)

        return (lambda: _dot(hm, wv_ref[:, cs])), epilogue

    def attn_item(n):
        a_ref = a_refs[n]
        dil = A_PATTERNS[n // 3][1]

        def epilogue(res):
            if dil == 1:
                a_ref[0] = res.astype(bf16)
                return
            for sl in range(nslab):
                slab_ref[sl] = res[:, sl * LANES:(sl + 1) * LANES]
            src_ref, groups = slab_ref, [(0, 0)]
            stride = dil
            if dil > STRIDE_ONE_OP:
                stride = dil // STRIDE_ONE_OP
                part = tm // STRIDE_ONE_OP
                for r0 in range(STRIDE_ONE_OP):
                    for sl in range(nslab):
                        slab2_ref[sl, r0 * part:(r0 + 1) * part, :] = (
                            slab_ref[sl, pl.ds(r0, part, stride=STRIDE_ONE_OP), :])
                src_ref = slab2_ref
                groups = [(r0 * part, r0) for r0 in range(STRIDE_ONE_OP)]
            for base, r0 in groups:
                for r1 in range(stride):
                    r = r1 * (dil // stride) + r0
                    for sl in range(nslab):
                        c0 = r * A_WIDTH + sl * LANES
                        a_ref[0, :, c0:c0 + LANES] = (
                            src_ref[sl, pl.ds(base + r1, tm // dil, stride=stride), :].astype(bf16))

        return (lambda: _dot(hm, wa_ref[:, n * A_WIDTH:(n + 1) * A_WIDTH])), epilogue

    def gate_epilogue(gpre):
        ns = 2 * M_HEADS
        lc = M_CHUNK
        sub = lax.broadcasted_iota(jnp.int32, (2 * ns, tm), 0)
        lrow = jnp.where(sub >= ns, jax.nn.log_sigmoid(gpre + gb_ref[...]), gpre + gb_ref[...])
        tri_r = lax.broadcasted_iota(jnp.int32, (lc, lc), 0)
        tri_c = lax.broadcasted_iota(jnp.int32, (lc, lc), 1)
        upper = (tri_c >= tri_r).astype(f32)
        lower = (tri_c <= tri_r).astype(f32)
        fwd_rows = lax.broadcasted_iota(jnp.int32, (ns, lc), 0) < M_HEADS
        for c in range(tm // lc):
            ls = slice(c * lc, (c + 1) * lc)
            blk = lrow[:, ls]
            lf = blk[ns:]
            f_cum = jnp.where(fwd_rows, _dot_exact(lf, upper), _dot_exact(lf, lower))
            f_all = jnp.sum(lf, axis=-1, keepdims=True)
            gr_ref[0, 0:ns, ls] = (blk[:ns] - f_cum) * LOG2E
            gr_ref[0, ns:2 * ns, ls] = lf * LOG2E
            gr_ref[0, 2 * ns:3 * ns, ls] = jnp.broadcast_to(f_all * LOG2E, (ns, lc))
            gr_ref[0, 3 * ns:, ls] = jnp.zeros((GATE_ROWS - 3 * ns, lc), f32)

    n_qk = 2 * M_WIDTH // PROJ_COLS
    n_v = M_WIDTH // PROJ_COLS
    light = [attn_item(n) for n in range(len(a_refs))] + [v_item(c) for c in range(n_v)]
    items = [((lambda: _dot_nt(wift_ref[...], hm)), gate_epilogue)]
    for c in range(n_qk):
        items.append(qk_item(c))
        items.append(light.pop())
    items += light

    pending = None
    for matmul, epilogue in items:
        val = matmul()
        if pending is not None:
            pending()
        pending = functools.partial(epilogue, val)
    pending()


def _proj(x, pre_w, conv_w, conv_b, wqk, wv, wift, gate_b, wa):
    b_, s_, _ = x.shape
    tm = PROJ_ROWS
    hb = tm // PROJ_HALO
    n_halo_blocks = s_ // PROJ_HALO
    row_spec = lambda width: pl.BlockSpec((1, tm, width), lambda b, i: (b, i, 0))
    in_specs = [
        row_spec(D_MODEL),
        pl.BlockSpec((1, PROJ_HALO, D_MODEL), lambda b, i: (b, jnp.maximum(i * hb - 1, 0), 0)),
        pl.BlockSpec((1, PROJ_HALO, D_MODEL), lambda b, i: (b, jnp.minimum((i + 1) * hb, n_halo_blocks - 1), 0)),
        _const_spec(pre_w.shape), _const_spec(conv_w.shape), _const_spec(conv_b.shape),
        _const_spec(wqk.shape), _const_spec(wv.shape), _const_spec(wift.shape), _const_spec(gate_b.shape),
        _const_spec(wa.shape),
    ]
    act = lambda width: jax.ShapeDtypeStruct((b_, s_, width), bf16)
    col_spec = lambda height: pl.BlockSpec((1, height, tm), lambda b, i: (b, 0, i))
    out_shape = [act(M_WIDTH), jax.ShapeDtypeStruct((b_, M_WIDTH, s_), bf16), act(M_WIDTH),
                 jax.ShapeDtypeStruct((b_, GATE_ROWS, s_), f32)]
    out_specs = [row_spec(M_WIDTH), col_spec(M_WIDTH), row_spec(M_WIDTH), col_spec(GATE_ROWS)]
    for _, dil in A_PATTERNS:
        out_shape += [jax.ShapeDtypeStruct((b_, s_ // dil, dil * A_WIDTH), bf16)] * 3
        out_specs += [pl.BlockSpec((1, tm // dil, dil * A_WIDTH), lambda b, i: (b, i, 0))] * 3
    return pl.pallas_call(
        _proj_body,
        grid=(b_, s_ // tm),
        in_specs=in_specs,
        out_specs=out_specs,
        out_shape=out_shape,
        scratch_shapes=[pltpu.VMEM((A_WIDTH // LANES, tm, LANES), f32)] * 2
                       + [pltpu.VMEM((PROJ_COLS // LANES, tm + 2 * PROJ_HALO, LANES), f32),
                          pltpu.VMEM((PROJ_COLS // LANES, tm, LANES), f32)],
        compiler_params=pltpu.CompilerParams(
            dimension_semantics=("parallel", "parallel"), vmem_limit_bytes=VMEM_LIMIT_BYTES),
        name="proj",
    )(x, x, x, pre_w, conv_w, conv_b, wqk, wv, wift, gate_b, wa)


def _mlstm_body(qf_ref, ktf_ref, vf_ref, grf_ref, qb_ref, ktb_ref, vb_ref, grb_ref,
                hf_ref, hb_ref, c_ref, n_ref, m_ref):
    j = pl.program_id(1)
    lc = M_CHUNK
    nh = M_HEADS
    ns = 2 * M_HEADS
    e = M_HEAD_DIM

    @pl.when(j == 0)
    def _():
        c_ref[...] = jnp.zeros_like(c_ref)
        n_ref[...] = jnp.zeros_like(n_ref)
        m_ref[...] = jnp.zeros_like(m_ref)

    row = lax.broadcasted_iota(jnp.int32, (lc, lc), 0)
    col = lax.broadcasted_iota(jnp.int32, (lc, lc), 1)

    def head_cols(hh):
        return slice(hh * e, (hh + 1) * e)

    for step_chunk in range(MLSTM_STEP_CHUNKS):
        streams = []
        for d in range(2):
            q_ref, kt_ref, v_ref, gr_ref, out_ref = (
                (qf_ref, ktf_ref, vf_ref, grf_ref, hf_ref) if d == 0 else (qb_ref, ktb_ref, vb_ref, grb_ref, hb_ref))
            mask = (col <= row) if d == 0 else (col >= row)
            chunk = step_chunk if d == 0 else MLSTM_STEP_CHUNKS - 1 - step_chunk
            tok = slice(chunk * lc, (chunk + 1) * lc)
            for hh in range(nh):
                streams.append((d * nh + hh, hh, q_ref, kt_ref, v_ref, gr_ref, out_ref, mask, tok))

        gated = []
        for ci, hh, q_ref, kt_ref, v_ref, gr_ref, out_ref, mask, tok in streams:
            r_row = gr_ref[0, ci:ci + 1, tok]
            lf_row = gr_ref[0, ns + ci:ns + ci + 1, tok]
            m_sc = m_ref[ci][:, 0:1]
            s_aug = _dot(q_ref[0, tok, head_cols(hh)],
                         jnp.concatenate([kt_ref[0, head_cols(hh), tok], n_ref[ci].astype(bf16)], axis=1))
            r_mat = jnp.where(mask, r_row, NEG)
            u = jnp.maximum(m_sc, jnp.max(r_mat, axis=-1, keepdims=True))
            f_col = jnp.sum(jnp.where(mask, lf_row, 0.0), axis=-1, keepdims=True)
            p = s_aug[:, :lc] * jnp.exp2(r_mat - u)
            wa = jnp.exp2(m_sc - u)
            den = jnp.sum(p, axis=-1, keepdims=True) + wa * s_aug[:, lc:]
            inv = 1.0 / jnp.maximum(jnp.abs(den), jnp.exp2(-(f_col + u)))
            gated.append((p.astype(bf16), wa, inv))

        for (ci, hh, q_ref, kt_ref, v_ref, gr_ref, out_ref, mask, tok), (p, wa, inv) in zip(streams, gated):
            acc = (_dot(p, v_ref[0, tok, head_cols(hh)])
                   + wa * _dot(q_ref[0, tok, head_cols(hh)], c_ref[ci].astype(bf16)))
            for part in range(e // LANES):
                ps = slice(part * LANES, (part + 1) * LANES)
                out_ref[0, tok, hh * e + part * LANES:hh * e + (part + 1) * LANES] = (acc[:, ps] * inv).astype(bf16)

        for ci, hh, q_ref, kt_ref, v_ref, gr_ref, out_ref, mask, tok in streams:
            r_row = gr_ref[0, ci:ci + 1, tok]
            f_all = gr_ref[0, 2 * ns + ci:2 * ns + ci + 1, tok]
            m_old = m_ref[ci]
            g = f_all + r_row
            m_new = jnp.maximum(f_all + m_old, jnp.max(g, axis=-1, keepdims=True))
            kw = kt_ref[0, head_cols(hh), tok].astype(f32) * jnp.exp2(g - m_new)
            decay = jnp.exp2(f_all + m_old - m_new)[:, 0:1]
            c_ref[ci] = decay * c_ref[ci] + _dot(kw.astype(bf16), v_ref[0, tok, head_cols(hh)])
            n_ref[ci] = decay * n_ref[ci] + jnp.sum(kw, axis=-1, keepdims=True)
            m_ref[ci] = m_new


def _mlstm(q, kt, v, grow):
    b_, s_, _ = q.shape
    lc = M_CHUNK * MLSTM_STEP_CHUNKS
    nc = s_ // lc
    fwd = pl.BlockSpec((1, lc, M_WIDTH), lambda b, j: (b, j, 0))
    bwd = pl.BlockSpec((1, lc, M_WIDTH), lambda b, j: (b, nc - 1 - j, 0))
    fwd_t = lambda height: pl.BlockSpec((1, height, lc), lambda b, j: (b, 0, j))
    bwd_t = lambda height: pl.BlockSpec((1, height, lc), lambda b, j: (b, 0, nc - 1 - j))
    out = jax.ShapeDtypeStruct((b_, s_, M_WIDTH), bf16)
    return pl.pallas_call(
        _mlstm_body,
        grid=(b_, nc),
        in_specs=[fwd, fwd_t(M_WIDTH), fwd, fwd_t(GATE_ROWS), bwd, bwd_t(M_WIDTH), bwd, bwd_t(GATE_ROWS)],
        out_specs=[fwd, bwd],
        out_shape=[out, out],
        scratch_shapes=[pltpu.VMEM((2 * M_HEADS, M_HEAD_DIM, M_HEAD_DIM), f32),
                        pltpu.VMEM((2 * M_HEADS, M_HEAD_DIM, LANES), f32),
                        pltpu.VMEM((2 * M_HEADS, 1, LANES), f32)],
        compiler_params=pltpu.CompilerParams(
            dimension_semantics=("parallel", "arbitrary"), vmem_limit_bytes=VMEM_LIMIT_BYTES),
        name="mlstm",
    )(q, kt, v, grow, q, kt, v, grow)


def _attn_blocks(q_ref, kwin, vwin, bias_ref, first_keys, sub_len):
    qb = ATT_QBLOCK
    win = qb + 2 * A_HALF
    lane_q = lax.broadcasted_iota(jnp.int32, (qb, LANES), 1)
    lo_q = lane_q < A_HEAD_DIM
    lo_w = lax.broadcasted_iota(jnp.int32, (win, LANES), 1) < A_HEAD_DIM
    zq = jnp.zeros((qb, LANES), bf16)
    zw = jnp.zeros((win, LANES), bf16)
    npair = A_HEADS // 2
    units = [(b, p, b * qb, slice(p * LANES, (p + 1) * LANES)) for b in range(len(first_keys)) for p in range(npair)]

    raws = []
    for b, p, r0, ps in units:
        qp = q_ref[0, r0:r0 + qb, ps]
        q2 = jnp.concatenate([jnp.where(lo_q, qp, zq), jnp.where(lo_q, zq, qp)], axis=0)
        raws.append(_dot_nt(q2, kwin[r0:r0 + win, ps]))

    edges = []
    for first_key in first_keys:
        kpos = first_key + lax.broadcasted_iota(jnp.int32, (1, win), 1)
        edges.append(jnp.where((kpos >= 0) & (kpos < sub_len), 0.0, NEG).astype(f32))

    probs = []
    stats = [jnp.zeros((qb, LANES), f32) for _ in first_keys]
    for (b, p, r0, ps), raw in zip(units, raws):
        s = raw + bias_ref[p] + edges[b]
        mx = jnp.max(s, axis=-1, keepdims=True)
        pe = jnp.exp2(s - mx)
        den = jnp.sum(pe, axis=-1, keepdims=True)
        probs.append(pe.astype(bf16))
        st = stats[b]
        st = jnp.where(lane_q == 2 * p, mx[:qb], st)
        st = jnp.where(lane_q == 2 * p + 1, mx[qb:], st)
        st = jnp.where(lane_q == A_HEADS + 2 * p, den[:qb], st)
        stats[b] = jnp.where(lane_q == A_HEADS + 2 * p + 1, den[qb:], st)

    outs = [[] for _ in first_keys]
    for (b, p, r0, ps), pb in zip(units, probs):
        vp = vwin[r0:r0 + win, ps]
        outs[b].append(_dot(pb[:qb], jnp.where(lo_w, vp, zw)) + _dot(pb[qb:], jnp.where(lo_w, zw, vp)))
    return list(zip(outs, stats))


def _attn_dilated_body(q_ref, k_ref, kp_ref, kn_ref, v_ref, vp_ref, vn_ref, bias_ref, o_ref, st_ref,
                       o_scr, st_scr, *, sub_len, step_rows, dil):
    i = pl.program_id(1)
    r = pl.program_id(2)
    qb = ATT_QBLOCK
    kwin = jnp.concatenate([kp_ref[0], k_ref[0], kn_ref[0]], axis=0)
    vwin = jnp.concatenate([vp_ref[0], v_ref[0], vn_ref[0]], axis=0)
    first_keys = [i * step_rows + blk * qb - A_HALF for blk in range(step_rows // qb)]
    for blk, (outs, stats) in enumerate(_attn_blocks(q_ref, kwin, vwin, bias_ref, first_keys, sub_len)):
        r0 = blk * qb
        rows = pl.ds(r0 * dil + r, qb, stride=dil)
        for p, o in enumerate(outs):
            o_scr[p, rows, :] = o
        st_scr[rows, :] = stats

    @pl.when(r == dil - 1)
    def _():
        for p in range(A_HEADS // 2):
            o_ref[0, :, p * LANES:(p + 1) * LANES] = o_scr[p].astype(bf16)
        st_ref[0] = st_scr[...]


def _attn_mix_body(q_ref, k_ref, kp_ref, kn_ref, v_ref, vp_ref, vn_ref, bias_ref,
                   o1_ref, st1_ref, o2_ref, st2_ref, expand_ref, y_ref, *, sub_len, step_rows):
    i = pl.program_id(1)
    qb = ATT_QBLOCK
    kwin = jnp.concatenate([kp_ref[0], k_ref[0], kn_ref[0]], axis=0)
    vwin = jnp.concatenate([vp_ref[0], v_ref[0], vn_ref[0]], axis=0)
    head_lane = lax.broadcasted_iota(jnp.int32, (qb, LANES), 1) < A_HEADS
    first_keys = [i * step_rows + blk * qb - A_HALF for blk in range(step_rows // qb)]
    for blk, (outs, st0) in enumerate(_attn_blocks(q_ref, kwin, vwin, bias_ref, first_keys, sub_len)):
        r0 = blk * qb
        st1 = st1_ref[0, r0:r0 + qb, :]
        st2 = st2_ref[0, r0:r0 + qb, :]
        dn0, dn1, dn2 = [pltpu.roll(st, LANES - A_HEADS, axis=1) for st in (st0, st1, st2)]
        top = jnp.maximum(jnp.maximum(st0, st1), st2)
        w0, w1, w2 = jnp.exp2(st0 - top), jnp.exp2(st1 - top), jnp.exp2(st2 - top)
        dsum = w0 * dn0 + w1 * dn1 + w2 * dn2
        o_groups = (jnp.concatenate(outs, axis=1),
                    o1_ref[0, r0:r0 + qb, :].astype(f32), o2_ref[0, r0:r0 + qb, :].astype(f32))
        coef = []
        for w in (w0, w1, w2):
            c = jnp.where(head_lane, w / dsum, 0.0)
            c_hi = c.astype(bf16)
            coef.append(jnp.concatenate([c_hi, (c - c_hi.astype(f32)).astype(bf16)], axis=1))
        wide = _dot(jnp.concatenate(coef, axis=0), expand_ref[...])
        y = (wide[:qb] * o_groups[0] + wide[qb:2 * qb] * o_groups[1] + wide[2 * qb:] * o_groups[2])
        y_ref[0, r0:r0 + qb, :] = y.astype(bf16)


def _attn_halo_specs(step_rows, sub_len, width, index):
    hpb = step_rows // A_HALF
    last = sub_len // A_HALF - 1
    before = pl.BlockSpec((1, A_HALF, width), lambda *g: index(g, jnp.maximum(g[1] * hpb - 1, 0)))
    after = pl.BlockSpec((1, A_HALF, width), lambda *g: index(g, jnp.minimum((g[1] + 1) * hpb, last)))
    return before, after


def _attn_dilated(q, k, v, bias, dil):
    b_, sub_len, _ = q.shape
    s_ = sub_len * dil
    step_rows = min(ATT_STEP_ROWS[dil], sub_len)
    tokens = step_rows * dil
    main = pl.BlockSpec((1, step_rows, A_WIDTH), lambda b, i, r: (b, i, r))
    before, after = _attn_halo_specs(step_rows, sub_len, A_WIDTH, lambda g, row: (g[0], row, g[2]))
    return pl.pallas_call(
        functools.partial(_attn_dilated_body, sub_len=sub_len, step_rows=step_rows, dil=dil),
        grid=(b_, sub_len // step_rows, dil),
        in_specs=[main, main, before, after, main, before, after, _const_spec(bias.shape)],
        out_specs=[pl.BlockSpec((1, tokens, A_WIDTH), lambda b, i, r: (b, i, 0)),
                   pl.BlockSpec((1, tokens, LANES), lambda b, i, r: (b, i, 0))],
        out_shape=[jax.ShapeDtypeStruct((b_, s_, A_WIDTH), bf16), jax.ShapeDtypeStruct((b_, s_, LANES), f32)],
        scratch_shapes=[pltpu.VMEM((A_WIDTH // LANES, tokens, LANES), f32), pltpu.VMEM((tokens, LANES), f32)],
        compiler_params=pltpu.CompilerParams(
            dimension_semantics=("parallel", "parallel", "arbitrary"), vmem_limit_bytes=VMEM_LIMIT_BYTES),
        name=f"attn_d{dil}",
    )(q, k, k, k, v, v, v, bias)


def _attn_mix(q, k, v, bias, o1, st1, o2, st2, expand):
    b_, s_, _ = q.shape
    step_rows = min(ATT_STEP_ROWS[1], s_)
    main = lambda width: pl.BlockSpec((1, step_rows, width), lambda b, i: (b, i, 0))
    before, after = _attn_halo_specs(step_rows, s_, A_WIDTH, lambda g, row: (g[0], row, 0))
    return pl.pallas_call(
        functools.partial(_attn_mix_body, sub_len=s_, step_rows=step_rows),
        grid=(b_, s_ // step_rows),
        in_specs=[main(A_WIDTH), main(A_WIDTH), before, after, main(A_WIDTH), before, after, _const_spec(bias.shape),
                  main(A_WIDTH), main(LANES), main(A_WIDTH), main(LANES), _const_spec(expand.shape)],
        out_specs=main(A_WIDTH),
        out_shape=jax.ShapeDtypeStruct((b_, s_, A_WIDTH), bf16),
        compiler_params=pltpu.CompilerParams(
            dimension_semantics=("parallel", "parallel"), vmem_limit_bytes=VMEM_LIMIT_BYTES),
        name="attn_mix",
    )(q, k, k, k, v, v, v, bias, o1, st1, o2, st2, expand)


def _final_body(x_ref, hf_ref, hb_ref, ya_ref, prew_ref, w4_ref, hnw_ref, wpm_ref, wpa_ref, wout_ref, postw_ref,
                y_ref):
    x = x_ref[0]
    h = _rms(x, prew_ref[...]).astype(bf16)
    hsum = hf_ref[0].astype(f32) + hb_ref[0].astype(f32)
    off_z, off_az, off_ga = M_WIDTH, 2 * M_WIDTH, 2 * M_WIDTH + A_WIDTH
    off_gb = off_ga + D_MODEL
    parts = []
    for hh in range(M_HEADS):
        hs = slice(hh * M_HEAD_DIM, (hh + 1) * M_HEAD_DIM)
        o = _dot(h, w4_ref[:, hs])
        z = _dot(h, w4_ref[:, off_z + hh * M_HEAD_DIM:off_z + (hh + 1) * M_HEAD_DIM])
        hx = hsum[:, hs]
        hn = hx * lax.rsqrt(jnp.mean(hx * hx, axis=-1, keepdims=True) + EPS) * hnw_ref[:, hs]
        parts.append((hn * _sigmoid(o) * _silu(z)).astype(bf16))
    ym = jnp.concatenate(parts, axis=1)
    az = _dot(h, w4_ref[:, off_az:off_az + A_WIDTH])
    ya = (ya_ref[0].astype(f32) * _silu(az)).astype(bf16)
    pm = _dot(ym, wpm_ref[...])
    pa = _dot(ya, wpa_ref[...])
    ga = _sigmoid(_dot(h, w4_ref[:, off_ga:off_ga + D_MODEL]))
    gb = _sigmoid(_dot(h, w4_ref[:, off_gb:off_gb + D_MODEL]))
    merged = (ga * pm + gb * pa).astype(bf16)
    out = _dot(merged, wout_ref[...])
    y_ref[0] = x + _rms(out, postw_ref[...])


def _final(x, hf, hb, ya, pre_w, w4, hnw, wpm, wpa, wout, post_w):
    b_, s_, _ = x.shape
    tm = FINAL_ROWS
    row_spec = lambda width: pl.BlockSpec((1, tm, width), lambda b, i: (b, i, 0))
    consts = [pre_w, w4, hnw, wpm, wpa, wout, post_w]
    return pl.pallas_call(
        _final_body,
        grid=(b_, s_ // tm),
        in_specs=[row_spec(D_MODEL), row_spec(M_WIDTH), row_spec(M_WIDTH), row_spec(A_WIDTH)]
                 + [_const_spec(c.shape) for c in consts],
        out_specs=row_spec(D_MODEL),
        out_shape=jax.ShapeDtypeStruct(x.shape, x.dtype),
        compiler_params=pltpu.CompilerParams(
            dimension_semantics=("parallel", "parallel"), vmem_limit_bytes=VMEM_LIMIT_BYTES),
        name="final",
    )(x, hf, hb, ya, *consts)


def _t5_bucket(rel):
    nb = N_BUCKETS // 2
    exact = nb // 2
    n = np.abs(rel)
    large = exact + (np.log(np.maximum(n, 1) / exact) / math.log(MAX_DISTANCE / exact) * (nb - exact)).astype(np.int32)
    large = np.minimum(large, nb - 1)
    return (rel > 0).astype(np.int32) * nb + np.where(n < exact, n, large)


def _attn_bias(rel_table, g, dil):
    win = ATT_QBLOCK + 2 * A_HALF
    off = np.arange(win)[None, :] - A_HALF - np.arange(ATT_QBLOCK)[:, None]
    band = np.abs(off) <= A_HALF
    bucket = np.where(band, _t5_bucket(off * dil), -1)
    onehot = jnp.asarray(bucket[None] == np.arange(N_BUCKETS)[:, None, None])
    table = rel_table.astype(f32)[:, g, :]
    bias = jnp.sum(jnp.where(onehot[:, None], table[:, :, None, None], 0.0), axis=0)
    bias = jnp.where(band[None], bias * LOG2E, NEG)
    return bias.reshape(A_HEADS // 2, 2 * ATT_QBLOCK, win)


def _head_expand_matrix():
    e = np.zeros((2 * LANES, A_WIDTH), np.float32)
    for h in range(A_HEADS):
        e[h, h * A_HEAD_DIM:(h + 1) * A_HEAD_DIM] = 1.0
        e[LANES + h, h * A_HEAD_DIM:(h + 1) * A_HEAD_DIM] = 1.0
    return jnp.asarray(e, bf16)


def _layer(x, p):
    q, kt, v, grow, *a = _proj(x, p["pre_w"], p["conv_w"], p["conv_b"], p["wqk"], p["wv"], p["wift"], p["bias_r"],
                               p["wa"])
    hf, hb = _mlstm(q, kt, v, grow)
    dilated = []
    for g, (_, dil) in enumerate(A_PATTERNS):
        if dil > 1:
            dilated += _attn_dilated(*a[3 * g:3 * g + 3], p["attn_bias"][g], dil)
    ya = _attn_mix(*a[0:3], p["attn_bias"][0], *dilated, _head_expand_matrix())
    return _final(x, hf, hb, ya, p["pre_w"], p["w4"], p["hnw"], p["wpm"], p["wpa"], p["wout"], p["post_w"])


def kernel(x_prompt, x_sample, pre_norm_w, w_in, m_conv_w, m_conv_b, m_igate_b, m_fgate_b, m_head_norm_w,
           w_proj_m, w_proj_a, w_out, post_norm_w, rel_bias_table):
    depth = pre_norm_w.shape[0]
    params = []
    for l in range(depth):
        w = w_in[l]
        wa = w[:, OFF_AQKV:OFF_AZ].reshape(D_MODEL, 3, N_GROUPS, A_WIDTH)
        wa = wa * jnp.asarray([A_HEAD_DIM ** -0.5 * LOG2E, 1.0, 1.0], w.dtype)[None, :, None, None]
        wa = wa.transpose(0, 2, 1, 3).reshape(D_MODEL, 3 * N_GROUPS * A_WIDTH)
        gate_b = jnp.concatenate([m_igate_b[l].reshape(-1), m_fgate_b[l].reshape(-1)]).astype(f32)
        params.append(dict(
            pre_w=pre_norm_w[l].reshape(1, D_MODEL), post_w=post_norm_w[l].reshape(1, D_MODEL),
            conv_w=m_conv_w[l], conv_b=m_conv_b[l].reshape(1, 2 * M_WIDTH),
            wqk=w[:, OFF_QK:OFF_V].astype(bf16), wv=w[:, OFF_V:OFF_O].astype(bf16),
            wift=w[:, OFF_I:OFF_AQKV].T.astype(bf16), wa=wa.astype(bf16),
            bias_r=gate_b.reshape(4 * M_HEADS, 1),
            w4=jnp.concatenate([w[:, OFF_O:OFF_I], w[:, OFF_AZ:]], axis=1).astype(bf16),
            hnw=m_head_norm_w[l].reshape(1, M_WIDTH),
            wpm=w_proj_m[l].astype(bf16), wpa=w_proj_a[l].astype(bf16), wout=w_out[l].astype(bf16),
            attn_bias=[_attn_bias(rel_bias_table, g, dil) for g, (_, dil) in enumerate(A_PATTERNS)],
        ))

    def trunk(x):
        for p in params:
            x = _layer(x, p)
        return x

    return (trunk(x_prompt), trunk(x_sample))
```

```python
import functools
import math

import numpy as np
import jax
import jax.numpy as jnp
from jax import lax
from jax.experimental import pallas as pl
from jax.experimental.pallas import tpu as pltpu

D_MODEL = 1024
M_HEADS = 4
M_HEAD_DIM = 256
M_WIDTH = M_HEADS * M_HEAD_DIM
M_CHUNK = 128
A_PATTERNS = ((128, 1), (512, 4), (2048, 16))
N_GROUPS = 3
A_HEADS = 8
A_HEAD_DIM = 64
A_WIDTH = A_HEADS * A_HEAD_DIM
A_HALF = 64
N_BUCKETS = 32
MAX_DISTANCE = 1024
EPS = 1e-6
NEG = -1e30
LOG2E = math.log2(math.e)

OFF_QK = 0
OFF_V = 2 * M_WIDTH
OFF_O = OFF_V + M_WIDTH
OFF_Z = OFF_O + M_WIDTH
OFF_I = OFF_Z + M_WIDTH
OFF_F = OFF_I + 2 * M_HEADS
OFF_AQKV = OFF_F + 2 * M_HEADS
OFF_AZ = OFF_AQKV + 3 * N_GROUPS * A_WIDTH
OFF_GATE = OFF_AZ + A_WIDTH

LANES = 128
F32_SUBLANES = 8
STRIDE_ONE_OP = 4
VMEM_LIMIT_BYTES = 56 * 1024 * 1024

GATE_ROWS = 32
PROJ_ROWS = 512
PROJ_HALO = 8
PROJ_COLS = 256
MLSTM_STEP_CHUNKS = 4
FINAL_ROWS = 512
ATT_QBLOCK = 128
ATT_STEP_ROWS = {1: 512, 4: 512, 16: 256}

f32 = jnp.float32
bf16 = jnp.bfloat16


def _const_spec(shape):
    nd = len(shape)
    return pl.BlockSpec(shape, lambda *_: (0,) * nd, pipeline_mode=pl.Buffered(1))


def _dot(a, b):
    return jnp.dot(a, b, preferred_element_type=f32)


def _dot_nt(a, b):
    return lax.dot_general(a, b, (((1,), (1,)), ((), ())), preferred_element_type=f32)


def _dot_tn(a, b):
    return lax.dot_general(a, b, (((0,), (0,)), ((), ())), preferred_element_type=f32)


def _dot_exact(a, b):
    return jnp.dot(a, b, preferred_element_type=f32, precision=lax.Precision.HIGHEST)


def _rms(x, w):
    return x * lax.rsqrt(jnp.mean(x * x, axis=-1, keepdims=True) + EPS) * w


def _sigmoid(x):
    return 0.5 + 0.5 * jnp.tanh(0.5 * x)


def _silu(x):
    half = 0.5 * x
    return half + half * jnp.tanh(half)


def _proj_body(x_ref, xp_ref, xn_ref, prew_ref, cw_ref, cb_ref, wqk_ref, wv_ref, wift_ref, gb_ref, wa_ref,
               q_ref, kt_ref, v_ref, gr_ref, *a_refs):
    i = pl.program_id(1)
    ni = pl.num_programs(1)
    tm = x_ref.shape[1]
    keep_prev = (i > 0).astype(f32)
    keep_next = (i < ni - 1).astype(f32)
    xa = jnp.concatenate([xp_ref[0] * keep_prev, x_ref[0], xn_ref[0] * keep_next], axis=0)
    hf = _rms(xa, prew_ref[...])
    h_ext = hf.astype(bf16)
    hm = hf[PROJ_HALO:PROJ_HALO + tm].astype(bf16)
    a_refs, (slab_ref, slab2_ref, conv_ref, act_ref) = a_refs[:-4], a_refs[-4:]
    nslab = A_WIDTH // LANES

    def qk_item(c):
        cs = slice(c * PROJ_COLS, (c + 1) * PROJ_COLS)

        def epilogue(r):
            half = tm // 2
            for sl in range(PROJ_COLS // LANES):
                conv_ref[sl] = r[:, sl * LANES:(sl + 1) * LANES]
            for sl in range(PROJ_COLS // LANES):
                col = slice(c * PROJ_COLS + sl * LANES, c * PROJ_COLS + (sl + 1) * LANES)
                w = cw_ref[:, col]
                even = conv_ref[sl, pl.ds(PROJ_HALO, half, stride=2), :]
                odd = conv_ref[sl, pl.ds(PROJ_HALO + 1, half, stride=2), :]
                odd_prev = conv_ref[sl, pl.ds(PROJ_HALO - 1, half, stride=2), :]
                even_next = conv_ref[sl, pl.ds(PROJ_HALO + 2, half, stride=2), :]
                y_even = cb_ref[:, col] + odd_prev * w[0:1] + even * w[1:2] + odd * w[2:3]
                y_odd = cb_ref[:, col] + even * w[0:1] + odd * w[1:2] + even_next * w[2:3]
                act_ref[sl, pl.ds(0, half, stride=2), :] = _silu(y_even)
                act_ref[sl, pl.ds(1, half, stride=2), :] = _silu(y_odd)
                if c * PROJ_COLS < M_WIDTH:
                    q_ref[0, :, col] = act_ref[sl].astype(bf16)
                else:
                    rows = slice(col.start - M_WIDTH, col.stop - M_WIDTH)
                    kt = (act_ref[sl] * (M_HEAD_DIM ** -0.5)).T.astype(bf16)
                    for ch in range(tm // M_CHUNK):
                        kt_ref[0, ch, rows, :] = kt[:, ch * M_CHUNK:(ch + 1) * M_CHUNK]

        return (lambda: _dot(h_ext, wqk_ref[:, cs])), epilogue

    def v_item(c):
        cs = slice(c * PROJ_COLS, (c + 1) * PROJ_COLS)

        def epilogue(res):
            v_ref[0, :, cs] = res.astype(bf16)

        return (lambda: _dot(hm, wv_ref[:, cs])), epilogue

    def attn_item(n):
        g, comp = divmod(n, 3)
        a_ref = a_refs[g]
        dil = A_PATTERNS[g][1]
        w_cols = slice((comp * N_GROUPS + g) * A_WIDTH, (comp * N_GROUPS + g + 1) * A_WIDTH)

        def epilogue(res):
            if dil == 1:
                a_ref[0, :, comp * A_WIDTH:(comp + 1) * A_WIDTH] = res.astype(bf16)
                return
            for sl in range(nslab):
                slab_ref[sl] = res[:, sl * LANES:(sl + 1) * LANES]
            src_ref, groups = slab_ref, [(0, 0)]
            stride = dil
            if dil > STRIDE_ONE_OP:
                stride = dil // STRIDE_ONE_OP
                part = tm // STRIDE_ONE_OP
                for r0 in range(STRIDE_ONE_OP):
                    for sl in range(nslab):
                        slab2_ref[sl, r0 * part:(r0 + 1) * part, :] = (
                            slab_ref[sl, pl.ds(r0, part, stride=STRIDE_ONE_OP), :])
                src_ref = slab2_ref
                groups = [(r0 * part, r0) for r0 in range(STRIDE_ONE_OP)]
            for base, r0 in groups:
                for r1 in range(stride):
                    r = r1 * (dil // stride) + r0
                    for sl in range(nslab):
                        c0 = (r * 3 + comp) * A_WIDTH + sl * LANES
                        a_ref[0, :, c0:c0 + LANES] = (
                            src_ref[sl, pl.ds(base + r1, tm // dil, stride=stride), :].astype(bf16))

        return (lambda: _dot(hm, wa_ref[:, w_cols])), epilogue

    def gate_epilogue(gpre):
        ns = 2 * M_HEADS
        lc = M_CHUNK
        sub = lax.broadcasted_iota(jnp.int32, (2 * ns, tm), 0)
        lrow = jnp.where(sub >= ns, jax.nn.log_sigmoid(gpre + gb_ref[...]), gpre + gb_ref[...])
        tri_r = lax.broadcasted_iota(jnp.int32, (lc, lc), 0)
        tri_c = lax.broadcasted_iota(jnp.int32, (lc, lc), 1)
        upper = (tri_c >= tri_r).astype(f32)
        lower = (tri_c <= tri_r).astype(f32)
        fwd_rows = lax.broadcasted_iota(jnp.int32, (ns, lc), 0) < M_HEADS
        for c in range(tm // lc):
            ls = slice(c * lc, (c + 1) * lc)
            blk = lrow[:, ls]
            lf = blk[ns:]
            f_cum = jnp.where(fwd_rows, _dot_exact(lf, upper), _dot_exact(lf, lower))
            f_all = jnp.sum(lf, axis=-1, keepdims=True)
            gr_ref[0, c, 0:ns, :] = (blk[:ns] - f_cum) * LOG2E
            gr_ref[0, c, ns:2 * ns, :] = lf * LOG2E
            gr_ref[0, c, 2 * ns:3 * ns, :] = jnp.broadcast_to(f_all * LOG2E, (ns, lc))
            gr_ref[0, c, 3 * ns:, :] = jnp.zeros((GATE_ROWS - 3 * ns, lc), f32)

    n_qk = 2 * M_WIDTH // PROJ_COLS
    n_v = M_WIDTH // PROJ_COLS
    light = [attn_item(n) for n in range(3 * N_GROUPS)] + [v_item(c) for c in range(n_v)]
    items = [((lambda: _dot_nt(wift_ref[...], hm)), gate_epilogue)]
    for c in range(n_qk):
        items.append(qk_item(c))
        items.append(light.pop())
    items += light

    pending = None
    for matmul, epilogue in items:
        val = matmul()
        if pending is not None:
            pending()
        pending = functools.partial(epilogue, val)
    pending()


def _proj(x, pre_w, conv_w, conv_b, wqk, wv, wift, gate_b, wa):
    b_, s_, _ = x.shape
    tm = PROJ_ROWS
    hb = tm // PROJ_HALO
    n_halo_blocks = s_ // PROJ_HALO
    row_spec = lambda width: pl.BlockSpec((1, tm, width), lambda b, i: (b, i, 0))
    in_specs = [
        row_spec(D_MODEL),
        pl.BlockSpec((1, PROJ_HALO, D_MODEL), lambda b, i: (b, jnp.maximum(i * hb - 1, 0), 0)),
        pl.BlockSpec((1, PROJ_HALO, D_MODEL), lambda b, i: (b, jnp.minimum((i + 1) * hb, n_halo_blocks - 1), 0)),
        _const_spec(pre_w.shape), _const_spec(conv_w.shape), _const_spec(conv_b.shape),
        _const_spec(wqk.shape), _const_spec(wv.shape), _const_spec(wift.shape), _const_spec(gate_b.shape),
        _const_spec(wa.shape),
    ]
    act = lambda width: jax.ShapeDtypeStruct((b_, s_, width), bf16)
    chunk_spec = lambda height: pl.BlockSpec((1, tm // M_CHUNK, height, M_CHUNK), lambda b, i: (b, i, 0, 0))
    out_shape = [act(M_WIDTH), jax.ShapeDtypeStruct((b_, s_ // M_CHUNK, M_WIDTH, M_CHUNK), bf16), act(M_WIDTH),
                 jax.ShapeDtypeStruct((b_, s_ // M_CHUNK, GATE_ROWS, M_CHUNK), f32)]
    out_specs = [row_spec(M_WIDTH), chunk_spec(M_WIDTH), row_spec(M_WIDTH), chunk_spec(GATE_ROWS)]
    for _, dil in A_PATTERNS:
        out_shape.append(jax.ShapeDtypeStruct((b_, s_ // dil, dil * 3 * A_WIDTH), bf16))
        out_specs.append(pl.BlockSpec((1, tm // dil, dil * 3 * A_WIDTH), lambda b, i: (b, i, 0)))
    return pl.pallas_call(
        _proj_body,
        grid=(b_, s_ // tm),
        in_specs=in_specs,
        out_specs=out_specs,
        out_shape=out_shape,
        scratch_shapes=[pltpu.VMEM((A_WIDTH // LANES, tm, LANES), f32)] * 2
                       + [pltpu.VMEM((PROJ_COLS // LANES, tm + 2 * PROJ_HALO, LANES), f32),
                          pltpu.VMEM((PROJ_COLS // LANES, tm, LANES), f32)],
        compiler_params=pltpu.CompilerParams(
            dimension_semantics=("parallel", "parallel"), vmem_limit_bytes=VMEM_LIMIT_BYTES),
        name="proj",
    )(x, x, x, pre_w, conv_w, conv_b, wqk, wv, wift, gate_b, wa)


def _mlstm_body(qf_ref, ktf_ref, vf_ref, grf_ref, qb_ref, ktb_ref, vb_ref, grb_ref,
                hf_ref, hb_ref, c_ref, n_ref, m_ref):
    j = pl.program_id(1)
    lc = M_CHUNK
    nh = M_HEADS
    ns = 2 * M_HEADS
    e = M_HEAD_DIM

    @pl.when(j == 0)
    def _():
        c_ref[...] = jnp.zeros_like(c_ref)
        n_ref[...] = jnp.zeros_like(n_ref)
        m_ref[...] = jnp.zeros_like(m_ref)

    row = lax.broadcasted_iota(jnp.int32, (lc, lc), 0)
    col = lax.broadcasted_iota(jnp.int32, (lc, lc), 1)

    def head_cols(hh):
        return slice(hh * e, (hh + 1) * e)

    def step_chunk(step, carry):
        streams = []
        for d in range(2):
            q_ref, kt_ref, v_ref, gr_ref, out_ref = (
                (qf_ref, ktf_ref, vf_ref, grf_ref, hf_ref) if d == 0 else (qb_ref, ktb_ref, vb_ref, grb_ref, hb_ref))
            mask = (col <= row) if d == 0 else (col >= row)
            chunk = step if d == 0 else MLSTM_STEP_CHUNKS - 1 - step
            tok = pl.ds(pl.multiple_of(chunk * lc, lc), lc)
            for hh in range(nh):
                streams.append((d * nh + hh, hh, q_ref, kt_ref, v_ref, gr_ref, out_ref, mask, chunk, tok))

        gated = []
        for ci, hh, q_ref, kt_ref, v_ref, gr_ref, out_ref, mask, chunk, tok in streams:
            r_row = gr_ref[0, chunk, ci:ci + 1, :]
            lf_row = gr_ref[0, chunk, ns + ci:ns + ci + 1, :]
            m_sc = m_ref[ci][:, 0:1]
            s_aug = _dot(q_ref[0, tok, head_cols(hh)],
                         jnp.concatenate([kt_ref[0, chunk, head_cols(hh), :], n_ref[ci].astype(bf16)], axis=1))
            r_mat = jnp.where(mask, r_row, NEG)
            u = jnp.maximum(m_sc, jnp.max(r_mat, axis=-1, keepdims=True))
            f_col = jnp.sum(jnp.where(mask, lf_row, 0.0), axis=-1, keepdims=True)
            p = s_aug[:, :lc] * jnp.exp2(r_mat - u)
            wa = jnp.exp2(m_sc - u)
            den = jnp.sum(p, axis=-1, keepdims=True) + wa * s_aug[:, lc:]
            inv = 1.0 / jnp.maximum(jnp.abs(den), jnp.exp2(-(f_col + u)))
            gated.append((p.astype(bf16), wa, inv))

        for (ci, hh, q_ref, kt_ref, v_ref, gr_ref, out_ref, mask, chunk, tok), (p, wa, inv) in zip(streams, gated):
            acc = (_dot(p, v_ref[0, tok, head_cols(hh)])
                   + wa * _dot(q_ref[0, tok, head_cols(hh)], c_ref[ci].astype(bf16)))
            for part in range(e // LANES):
                ps = slice(part * LANES, (part + 1) * LANES)
                out_ref[0, tok, hh * e + part * LANES:hh * e + (part + 1) * LANES] = (acc[:, ps] * inv).astype(bf16)

        for ci, hh, q_ref, kt_ref, v_ref, gr_ref, out_ref, mask, chunk, tok in streams:
            r_row = gr_ref[0, chunk, ci:ci + 1, :]
            f_all = gr_ref[0, chunk, 2 * ns + ci:2 * ns + ci + 1, :]
            m_old = m_ref[ci]
            g = f_all + r_row
            m_new = jnp.maximum(f_all + m_old, jnp.max(g, axis=-1, keepdims=True))
            kw = kt_ref[0, chunk, head_cols(hh), :].astype(f32) * jnp.exp2(g - m_new)
            decay = jnp.exp2(f_all + m_old - m_new)[:, 0:1]
            c_ref[ci] = decay * c_ref[ci] + _dot(kw.astype(bf16), v_ref[0, tok, head_cols(hh)])
            n_ref[ci] = decay * n_ref[ci] + jnp.sum(kw, axis=-1, keepdims=True)
            m_ref[ci] = m_new
        return carry

    lax.fori_loop(0, MLSTM_STEP_CHUNKS, step_chunk, 0)


def _mlstm(q, kt, v, grow):
    b_, s_, _ = q.shape
    sc = MLSTM_STEP_CHUNKS
    lc = M_CHUNK * sc
    nc = s_ // lc
    fwd = pl.BlockSpec((1, lc, M_WIDTH), lambda b, j: (b, j, 0))
    bwd = pl.BlockSpec((1, lc, M_WIDTH), lambda b, j: (b, nc - 1 - j, 0))
    fwd_t = lambda height: pl.BlockSpec((1, sc, height, M_CHUNK), lambda b, j: (b, j, 0, 0))
    bwd_t = lambda height: pl.BlockSpec((1, sc, height, M_CHUNK), lambda b, j: (b, nc - 1 - j, 0, 0))
    out = jax.ShapeDtypeStruct((b_, s_, M_WIDTH), bf16)
    return pl.pallas_call(
        _mlstm_body,
        grid=(b_, nc),
        in_specs=[fwd, fwd_t(M_WIDTH), fwd, fwd_t(GATE_ROWS), bwd, bwd_t(M_WIDTH), bwd, bwd_t(GATE_ROWS)],
        out_specs=[fwd, bwd],
        out_shape=[out, out],
        scratch_shapes=[pltpu.VMEM((2 * M_HEADS, M_HEAD_DIM, M_HEAD_DIM), f32),
                        pltpu.VMEM((2 * M_HEADS, M_HEAD_DIM, LANES), f32),
                        pltpu.VMEM((2 * M_HEADS, 1, LANES), f32)],
        compiler_params=pltpu.CompilerParams(
            dimension_semantics=("parallel", "arbitrary"), vmem_limit_bytes=VMEM_LIMIT_BYTES),
        name="mlstm",
    )(q, kt, v, grow, q, kt, v, grow)


def _attn_blocks(q_ref, kwin, vwin, bias_ref, first_keys, sub_len):
    qb = ATT_QBLOCK
    win = qb + 2 * A_HALF
    lane_q = lax.broadcasted_iota(jnp.int32, (qb, LANES), 1)
    lo_q = lane_q < A_HEAD_DIM
    lo_w = lax.broadcasted_iota(jnp.int32, (win, LANES), 1) < A_HEAD_DIM
    zq = jnp.zeros((qb, LANES), bf16)
    zw = jnp.zeros((win, LANES), bf16)
    npair = A_HEADS // 2
    units = [(b, p, b * qb, slice(p * LANES, (p + 1) * LANES)) for b in range(len(first_keys)) for p in range(npair)]

    raws = []
    for b, p, r0, ps in units:
        qp = q_ref[0, r0:r0 + qb, ps]
        q2 = jnp.concatenate([jnp.where(lo_q, qp, zq), jnp.where(lo_q, zq, qp)], axis=0)
        raws.append(_dot_nt(q2, kwin[r0:r0 + win, ps]))

    edges = []
    for first_key in first_keys:
        kpos = first_key + lax.broadcasted_iota(jnp.int32, (1, win), 1)
        edges.append(jnp.where((kpos >= 0) & (kpos < sub_len), 0.0, NEG).astype(f32))

    probs = []
    stats = [jnp.zeros((qb, LANES), f32) for _ in first_keys]
    for (b, p, r0, ps), raw in zip(units, raws):
        s = raw + bias_ref[p] + edges[b]
        mx = jnp.max(s, axis=-1, keepdims=True)
        pe = jnp.exp2(s - mx)
        den = jnp.sum(pe, axis=-1, keepdims=True)
        probs.append(pe.astype(bf16))
        st = stats[b]
        st = jnp.where(lane_q == 2 * p, mx[:qb], st)
        st = jnp.where(lane_q == 2 * p + 1, mx[qb:], st)
        st = jnp.where(lane_q == A_HEADS + 2 * p, den[:qb], st)
        stats[b] = jnp.where(lane_q == A_HEADS + 2 * p + 1, den[qb:], st)

    outs = [[] for _ in first_keys]
    for (b, p, r0, ps), pb in zip(units, probs):
        vp = vwin[r0:r0 + win, ps]
        outs[b].append(_dot(pb[:qb], jnp.where(lo_w, vp, zw)) + _dot(pb[qb:], jnp.where(lo_w, zw, vp)))
    return list(zip(outs, stats))


def _attn_dilated_body(q_ref, k_ref, kp_ref, kn_ref, v_ref, vp_ref, vn_ref, bias_ref, o_ref, st_ref,
                       o_scr, st_scr, *, sub_len, step_rows, dil):
    i = pl.program_id(1)
    r = pl.program_id(2)
    qb = ATT_QBLOCK
    kwin = jnp.concatenate([kp_ref[0], k_ref[0], kn_ref[0]], axis=0)
    vwin = jnp.concatenate([vp_ref[0], v_ref[0], vn_ref[0]], axis=0)
    first_keys = [i * step_rows + blk * qb - A_HALF for blk in range(step_rows // qb)]
    for blk, (outs, stats) in enumerate(_attn_blocks(q_ref, kwin, vwin, bias_ref, first_keys, sub_len)):
        r0 = blk * qb
        rows = pl.ds(r0 * dil + r, qb, stride=dil)
        for p, o in enumerate(outs):
            o_scr[p, rows, :] = o
        st_scr[rows, :] = stats

    @pl.when(r == dil - 1)
    def _():
        for p in range(A_HEADS // 2):
            o_ref[0, :, p * LANES:(p + 1) * LANES] = o_scr[p].astype(bf16)
        st_ref[0] = st_scr[...]


def _attn_mix_body(q_ref, k_ref, kp_ref, kn_ref, v_ref, vp_ref, vn_ref, bias_ref,
                   o1_ref, st1_ref, o2_ref, st2_ref, expand_ref, y_ref, *, sub_len, step_rows):
    i = pl.program_id(1)
    qb = ATT_QBLOCK
    kwin = jnp.concatenate([kp_ref[0], k_ref[0], kn_ref[0]], axis=0)
    vwin = jnp.concatenate([vp_ref[0], v_ref[0], vn_ref[0]], axis=0)
    head_lane = lax.broadcasted_iota(jnp.int32, (qb, LANES), 1) < A_HEADS
    first_keys = [i * step_rows + blk * qb - A_HALF for blk in range(step_rows // qb)]
    for blk, (outs, st0) in enumerate(_attn_blocks(q_ref, kwin, vwin, bias_ref, first_keys, sub_len)):
        r0 = blk * qb
        st1 = st1_ref[0, r0:r0 + qb, :]
        st2 = st2_ref[0, r0:r0 + qb, :]
        dn0, dn1, dn2 = [pltpu.roll(st, LANES - A_HEADS, axis=1) for st in (st0, st1, st2)]
        top = jnp.maximum(jnp.maximum(st0, st1), st2)
        w0, w1, w2 = jnp.exp2(st0 - top), jnp.exp2(st1 - top), jnp.exp2(st2 - top)
        dsum = w0 * dn0 + w1 * dn1 + w2 * dn2
        o_groups = (jnp.concatenate(outs, axis=1),
                    o1_ref[0, r0:r0 + qb, :].astype(f32), o2_ref[0, r0:r0 + qb, :].astype(f32))
        coef = []
        for w in (w0, w1, w2):
            c = jnp.where(head_lane, w / dsum, 0.0)
            c_hi = c.astype(bf16)
            coef.append(jnp.concatenate([c_hi, (c - c_hi.astype(f32)).astype(bf16)], axis=1))
        wide = _dot(jnp.concatenate(coef, axis=0), expand_ref[...])
        y = (wide[:qb] * o_groups[0] + wide[qb:2 * qb] * o_groups[1] + wide[2 * qb:] * o_groups[2])
        y_ref[0, r0:r0 + qb, :] = y.astype(bf16)


def _attn_halo_specs(step_rows, sub_len, width, index):
    hpb = step_rows // A_HALF
    last = sub_len // A_HALF - 1
    before = pl.BlockSpec((1, A_HALF, width), lambda *g: index(g, jnp.maximum(g[1] * hpb - 1, 0)))
    after = pl.BlockSpec((1, A_HALF, width), lambda *g: index(g, jnp.minimum((g[1] + 1) * hpb, last)))
    return before, after


def _attn_dilated(qkv, bias, dil):
    b_, sub_len, _ = qkv.shape
    s_ = sub_len * dil
    step_rows = min(ATT_STEP_ROWS[dil], sub_len)
    tokens = step_rows * dil
    in_specs = []
    for comp in range(3):
        in_specs.append(pl.BlockSpec((1, step_rows, A_WIDTH), lambda b, i, r, comp=comp: (b, i, 3 * r + comp)))
        if comp > 0:
            in_specs += _attn_halo_specs(step_rows, sub_len, A_WIDTH,
                                         lambda g, row, comp=comp: (g[0], row, 3 * g[2] + comp))
    return pl.pallas_call(
        functools.partial(_attn_dilated_body, sub_len=sub_len, step_rows=step_rows, dil=dil),
        grid=(b_, sub_len // step_rows, dil),
        in_specs=in_specs + [_const_spec(bias.shape)],
        out_specs=[pl.BlockSpec((1, tokens, A_WIDTH), lambda b, i, r: (b, i, 0)),
                   pl.BlockSpec((1, tokens, LANES), lambda b, i, r: (b, i, 0))],
        out_shape=[jax.ShapeDtypeStruct((b_, s_, A_WIDTH), bf16), jax.ShapeDtypeStruct((b_, s_, LANES), f32)],
        scratch_shapes=[pltpu.VMEM((A_WIDTH // LANES, tokens, LANES), f32), pltpu.VMEM((tokens, LANES), f32)],
        compiler_params=pltpu.CompilerParams(
            dimension_semantics=("parallel", "parallel", "arbitrary"), vmem_limit_bytes=VMEM_LIMIT_BYTES),
        name=f"attn_d{dil}",
    )(*([qkv] * 7), bias)


def _attn_mix(qkv, bias, o1, st1, o2, st2, expand):
    b_, s_, _ = qkv.shape
    step_rows = min(ATT_STEP_ROWS[1], s_)
    main = lambda width: pl.BlockSpec((1, step_rows, width), lambda b, i: (b, i, 0))
    in_specs = []
    for comp in range(3):
        in_specs.append(pl.BlockSpec((1, step_rows, A_WIDTH), lambda b, i, comp=comp: (b, i, comp)))
        if comp > 0:
            in_specs += _attn_halo_specs(step_rows, s_, A_WIDTH, lambda g, row, comp=comp: (g[0], row, comp))
    return pl.pallas_call(
        functools.partial(_attn_mix_body, sub_len=s_, step_rows=step_rows),
        grid=(b_, s_ // step_rows),
        in_specs=in_specs + [_const_spec(bias.shape),
                             main(A_WIDTH), main(LANES), main(A_WIDTH), main(LANES), _const_spec(expand.shape)],
        out_specs=main(A_WIDTH),
        out_shape=jax.ShapeDtypeStruct((b_, s_, A_WIDTH), bf16),
        compiler_params=pltpu.CompilerParams(
            dimension_semantics=("parallel", "parallel"), vmem_limit_bytes=VMEM_LIMIT_BYTES),
        name="attn_mix",
    )(*([qkv] * 7), bias, o1, st1, o2, st2, expand)


def _final_body(x_ref, hf_ref, hb_ref, ya_ref, prew_ref, w4_ref, hnw_ref, wpm_ref, wpa_ref, wout_ref, postw_ref,
                y_ref):
    x = x_ref[0]
    h = _rms(x, prew_ref[...]).astype(bf16)
    hsum = hf_ref[0].astype(f32) + hb_ref[0].astype(f32)
    off_z, off_az, off_ga = M_WIDTH, 2 * M_WIDTH, 2 * M_WIDTH + A_WIDTH
    off_gb = off_ga + D_MODEL
    parts = []
    for hh in range(M_HEADS):
        hs = slice(hh * M_HEAD_DIM, (hh + 1) * M_HEAD_DIM)
        o = _dot(h, w4_ref[:, hs])
        z = _dot(h, w4_ref[:, off_z + hh * M_HEAD_DIM:off_z + (hh + 1) * M_HEAD_DIM])
        hx = hsum[:, hs]
        hn = hx * lax.rsqrt(jnp.mean(hx * hx, axis=-1, keepdims=True) + EPS) * hnw_ref[:, hs]
        parts.append((hn * _sigmoid(o) * _silu(z)).astype(bf16))
    ym = jnp.concatenate(parts, axis=1)
    az = _dot(h, w4_ref[:, off_az:off_az + A_WIDTH])
    ya = (ya_ref[0].astype(f32) * _silu(az)).astype(bf16)
    pm = _dot(ym, wpm_ref[...])
    pa = _dot(ya, wpa_ref[...])
    ga = _sigmoid(_dot(h, w4_ref[:, off_ga:off_ga + D_MODEL]))
    gb = _sigmoid(_dot(h, w4_ref[:, off_gb:off_gb + D_MODEL]))
    merged = (ga * pm + gb * pa).astype(bf16)
    out = _dot(merged, wout_ref[...])
    y_ref[0] = x + _rms(out, postw_ref[...])


def _final(x, hf, hb, ya, pre_w, w4, hnw, wpm, wpa, wout, post_w):
    b_, s_, _ = x.shape
    tm = FINAL_ROWS
    row_spec = lambda width: pl.BlockSpec((1, tm, width), lambda b, i: (b, i, 0))
    consts = [pre_w, w4, hnw, wpm, wpa, wout, post_w]
    return pl.pallas_call(
        _final_body,
        grid=(b_, s_ // tm),
        in_specs=[row_spec(D_MODEL), row_spec(M_WIDTH), row_spec(M_WIDTH), row_spec(A_WIDTH)]
                 + [_const_spec(c.shape) for c in consts],
        out_specs=row_spec(D_MODEL),
        out_shape=jax.ShapeDtypeStruct(x.shape, x.dtype),
        compiler_params=pltpu.CompilerParams(
            dimension_semantics=("parallel", "parallel"), vmem_limit_bytes=VMEM_LIMIT_BYTES),
        name="final",
    )(x, hf, hb, ya, *consts)


def _t5_bucket(rel):
    nb = N_BUCKETS // 2
    exact = nb // 2
    n = np.abs(rel)
    large = exact + (np.log(np.maximum(n, 1) / exact) / math.log(MAX_DISTANCE / exact) * (nb - exact)).astype(np.int32)
    large = np.minimum(large, nb - 1)
    return (rel > 0).astype(np.int32) * nb + np.where(n < exact, n, large)


def _attn_bias(rel_table, g, dil):
    win = ATT_QBLOCK + 2 * A_HALF
    off = np.arange(win)[None, :] - A_HALF - np.arange(ATT_QBLOCK)[:, None]
    band = np.abs(off) <= A_HALF
    bucket = np.where(band, _t5_bucket(off * dil), -1)
    onehot = jnp.asarray(bucket[None] == np.arange(N_BUCKETS)[:, None, None])
    table = rel_table.astype(f32)[:, g, :]
    bias = jnp.sum(jnp.where(onehot[:, None], table[:, :, None, None], 0.0), axis=0)
    bias = jnp.where(band[None], bias * LOG2E, NEG)
    return bias.reshape(A_HEADS // 2, 2 * ATT_QBLOCK, win)


def _head_expand_matrix():
    e = np.zeros((2 * LANES, A_WIDTH), np.float32)
    for h in range(A_HEADS):
        e[h, h * A_HEAD_DIM:(h + 1) * A_HEAD_DIM] = 1.0
        e[LANES + h, h * A_HEAD_DIM:(h + 1) * A_HEAD_DIM] = 1.0
    return jnp.asarray(e, bf16)


def _layer(x, p):
    q, kt, v, grow, *a = _proj(x, p["pre_w"], p["conv_w"], p["conv_b"], p["wqk"], p["wv"], p["wift"], p["bias_r"],
                               p["wa"])
    hf, hb = _mlstm(q, kt, v, grow)
    dilated = []
    for g, (_, dil) in enumerate(A_PATTERNS):
        if dil > 1:
            dilated += _attn_dilated(a[g], p["attn_bias"][g], dil)
    ya = _attn_mix(a[0], p["attn_bias"][0], *dilated, _head_expand_matrix())
    return _final(x, hf, hb, ya, p["pre_w"], p["w4"], p["hnw"], p["wpm"], p["wpa"], p["wout"], p["post_w"])


def kernel(x_prompt, x_sample, pre_norm_w, w_in, m_conv_w, m_conv_b, m_igate_b, m_fgate_b, m_head_norm_w,
           w_proj_m, w_proj_a, w_out, post_norm_w, rel_bias_table):
    depth = pre_norm_w.shape[0]
    params = []
    for l in range(depth):
        w = w_in[l]
        n_query = N_GROUPS * A_WIDTH
        wa = jnp.concatenate([w[:, OFF_AQKV:OFF_AQKV + n_query] * (A_HEAD_DIM ** -0.5 * LOG2E),
                              w[:, OFF_AQKV + n_query:OFF_AZ]], axis=1)
        gate_b = jnp.concatenate([m_igate_b[l].reshape(-1), m_fgate_b[l].reshape(-1)]).astype(f32)
        params.append(dict(
            pre_w=pre_norm_w[l].reshape(1, D_MODEL), post_w=post_norm_w[l].reshape(1, D_MODEL),
            conv_w=m_conv_w[l], conv_b=m_conv_b[l].reshape(1, 2 * M_WIDTH),
            wqk=w[:, OFF_QK:OFF_V].astype(bf16), wv=w[:, OFF_V:OFF_O].astype(bf16),
            wift=w[:, OFF_I:OFF_AQKV].T.astype(bf16), wa=wa.astype(bf16),
            bias_r=gate_b.reshape(4 * M_HEADS, 1),
            w4=jnp.concatenate([w[:, OFF_O:OFF_I], w[:, OFF_AZ:]], axis=1).astype(bf16),
            hnw=m_head_norm_w[l].reshape(1, M_WIDTH),
            wpm=w_proj_m[l].astype(bf16), wpa=w_proj_a[l].astype(bf16), wout=w_out[l].astype(bf16),
            attn_bias=[_attn_bias(rel_bias_table, g, dil) for g, (_, dil) in enumerate(A_PATTERNS)],
        ))

    def trunk(x):
        for p in params:
            x = _layer(x, p)
        return x

    return (trunk(x_prompt), trunk(x_sample))
```

```python
import functools
import math

import numpy as np
import jax
import jax.numpy as jnp
from jax import lax
from jax.experimental import pallas as pl
from jax.experimental.pallas import tpu as pltpu

D_MODEL = 1024
M_HEADS = 4
M_HEAD_DIM = 256
M_WIDTH = M_HEADS * M_HEAD_DIM
M_CHUNK = 128
A_PATTERNS = ((128, 1), (512, 4), (2048, 16))
N_GROUPS = 3
A_HEADS = 8
A_HEAD_DIM = 64
A_WIDTH = A_HEADS * A_HEAD_DIM
A_HALF = 64
N_BUCKETS = 32
MAX_DISTANCE = 1024
EPS = 1e-6
NEG = -1e30
LOG2E = math.log2(math.e)

OFF_QK = 0
OFF_V = 2 * M_WIDTH
OFF_O = OFF_V + M_WIDTH
OFF_Z = OFF_O + M_WIDTH
OFF_I = OFF_Z + M_WIDTH
OFF_F = OFF_I + 2 * M_HEADS
OFF_AQKV = OFF_F + 2 * M_HEADS
OFF_AZ = OFF_AQKV + 3 * N_GROUPS * A_WIDTH
OFF_GATE = OFF_AZ + A_WIDTH

LANES = 128
F32_SUBLANES = 8
STRIDE_ONE_OP = 4
VMEM_LIMIT_BYTES = 60 * 1024 * 1024

GATE_ROWS = 32
PROJ_ROWS = 512
PROJ_HALO = 8
PROJ_COLS = 256
MLSTM_STEP_CHUNKS = 4
FINAL_ROWS = 512
ATT_QBLOCK = 128
ATT_STEP_ROWS = {1: 512, 4: 512, 16: 256}

f32 = jnp.float32
bf16 = jnp.bfloat16


def _const_spec(shape):
    nd = len(shape)
    return pl.BlockSpec(shape, lambda *_: (0,) * nd, pipeline_mode=pl.Buffered(1))


def _dot(a, b):
    return jnp.dot(a, b, preferred_element_type=f32)


def _dot_nt(a, b):
    return lax.dot_general(a, b, (((1,), (1,)), ((), ())), preferred_element_type=f32)


def _dot_tn(a, b):
    return lax.dot_general(a, b, (((0,), (0,)), ((), ())), preferred_element_type=f32)


def _dot_exact(a, b):
    return jnp.dot(a, b, preferred_element_type=f32, precision=lax.Precision.HIGHEST)


def _rms(x, w):
    return x * lax.rsqrt(jnp.mean(x * x, axis=-1, keepdims=True) + EPS) * w


def _sigmoid(x):
    return 0.5 + 0.5 * jnp.tanh(0.5 * x)


def _silu(x):
    half = 0.5 * x
    return half + half * jnp.tanh(half)


PROJ_N_IN, PROJ_N_OUT, PROJ_N_SCRATCH = 11, 4 + N_GROUPS, 4


def _run(steps):
    for _ in steps:
        pass


def _interleave(first, second, n_first, n_second):
    done = [0, 0]
    gens = [first, second]
    total = [n_first, n_second]
    live = [True, True]
    while live[0] or live[1]:
        pick = 0 if (live[0] and (not live[1] or done[0] * total[1] <= done[1] * total[0])) else 1
        try:
            next(gens[pick])
            done[pick] += 1
        except StopIteration:
            live[pick] = False


def _proj_body(*refs):
    _run(_proj_steps(pl.program_id(1), pl.num_programs(1), *refs))


def _proj_steps(i, ni, x_ref, xp_ref, xn_ref, prew_ref, cw_ref, cb_ref, wqk_ref, wv_ref, wift_ref, gb_ref, wa_ref,
                q_ref, kt_ref, v_ref, gr_ref, *a_refs):
    tm = x_ref.shape[1]
    keep_prev = (i > 0).astype(f32)
    keep_next = (i < ni - 1).astype(f32)
    xa = jnp.concatenate([xp_ref[0] * keep_prev, x_ref[0], xn_ref[0] * keep_next], axis=0)
    hf = _rms(xa, prew_ref[...])
    h_ext = hf.astype(bf16)
    hm = hf[PROJ_HALO:PROJ_HALO + tm].astype(bf16)
    a_refs, (slab_ref, slab2_ref, conv_ref, act_ref) = a_refs[:-4], a_refs[-4:]
    nslab = A_WIDTH // LANES

    def qk_item(c):
        cs = slice(c * PROJ_COLS, (c + 1) * PROJ_COLS)

        def epilogue(r):
            half = tm // 2
            for sl in range(PROJ_COLS // LANES):
                conv_ref[sl] = r[:, sl * LANES:(sl + 1) * LANES]
            for sl in range(PROJ_COLS // LANES):
                col = slice(c * PROJ_COLS + sl * LANES, c * PROJ_COLS + (sl + 1) * LANES)
                w = cw_ref[:, col]
                even = conv_ref[sl, pl.ds(PROJ_HALO, half, stride=2), :]
                odd = conv_ref[sl, pl.ds(PROJ_HALO + 1, half, stride=2), :]
                odd_prev = conv_ref[sl, pl.ds(PROJ_HALO - 1, half, stride=2), :]
                even_next = conv_ref[sl, pl.ds(PROJ_HALO + 2, half, stride=2), :]
                y_even = cb_ref[:, col] + odd_prev * w[0:1] + even * w[1:2] + odd * w[2:3]
                y_odd = cb_ref[:, col] + even * w[0:1] + odd * w[1:2] + even_next * w[2:3]
                act_ref[sl, pl.ds(0, half, stride=2), :] = _silu(y_even)
                act_ref[sl, pl.ds(1, half, stride=2), :] = _silu(y_odd)
                if c * PROJ_COLS < M_WIDTH:
                    q_ref[0, :, col] = act_ref[sl].astype(bf16)
                else:
                    rows = slice(col.start - M_WIDTH, col.stop - M_WIDTH)
                    kt = (act_ref[sl] * (M_HEAD_DIM ** -0.5)).T.astype(bf16)
                    for ch in range(tm // M_CHUNK):
                        kt_ref[0, ch, rows, :] = kt[:, ch * M_CHUNK:(ch + 1) * M_CHUNK]

        return (lambda: _dot(h_ext, wqk_ref[:, cs])), epilogue

    def v_item(c):
        cs = slice(c * PROJ_COLS, (c + 1) * PROJ_COLS)

        def epilogue(res):
            v_ref[0, :, cs] = res.astype(bf16)

        return (lambda: _dot(hm, wv_ref[:, cs])), epilogue

    def attn_item(n):
        g, comp = divmod(n, 3)
        a_ref = a_refs[g]
        dil = A_PATTERNS[g][1]
        w_cols = slice((comp * N_GROUPS + g) * A_WIDTH, (comp * N_GROUPS + g + 1) * A_WIDTH)

        def epilogue(res):
            if dil == 1:
                a_ref[0, :, comp * A_WIDTH:(comp + 1) * A_WIDTH] = res.astype(bf16)
                return
            for sl in range(nslab):
                slab_ref[sl] = res[:, sl * LANES:(sl + 1) * LANES]
            src_ref, groups = slab_ref, [(0, 0)]
            stride = dil
            if dil > STRIDE_ONE_OP:
                stride = dil // STRIDE_ONE_OP
                part = tm // STRIDE_ONE_OP
                for r0 in range(STRIDE_ONE_OP):
                    for sl in range(nslab):
                        slab2_ref[sl, r0 * part:(r0 + 1) * part, :] = (
                            slab_ref[sl, pl.ds(r0, part, stride=STRIDE_ONE_OP), :])
                src_ref = slab2_ref
                groups = [(r0 * part, r0) for r0 in range(STRIDE_ONE_OP)]
            for base, r0 in groups:
                for r1 in range(stride):
                    r = r1 * (dil // stride) + r0
                    for sl in range(nslab):
                        c0 = (r * 3 + comp) * A_WIDTH + sl * LANES
                        a_ref[0, :, c0:c0 + LANES] = (
                            src_ref[sl, pl.ds(base + r1, tm // dil, stride=stride), :].astype(bf16))

        return (lambda: _dot(hm, wa_ref[:, w_cols])), epilogue

    def gate_epilogue(gpre):
        ns = 2 * M_HEADS
        lc = M_CHUNK
        sub = lax.broadcasted_iota(jnp.int32, (2 * ns, tm), 0)
        lrow = jnp.where(sub >= ns, jax.nn.log_sigmoid(gpre + gb_ref[...]), gpre + gb_ref[...])
        tri_r = lax.broadcasted_iota(jnp.int32, (lc, lc), 0)
        tri_c = lax.broadcasted_iota(jnp.int32, (lc, lc), 1)
        upper = (tri_c >= tri_r).astype(f32)
        lower = (tri_c <= tri_r).astype(f32)
        fwd_rows = lax.broadcasted_iota(jnp.int32, (ns, lc), 0) < M_HEADS
        for c in range(tm // lc):
            ls = slice(c * lc, (c + 1) * lc)
            blk = lrow[:, ls]
            lf = blk[ns:]
            f_cum = jnp.where(fwd_rows, _dot_exact(lf, upper), _dot_exact(lf, lower))
            f_all = jnp.sum(lf, axis=-1, keepdims=True)
            gr_ref[0, c, 0:ns, :] = (blk[:ns] - f_cum) * LOG2E
            gr_ref[0, c, ns:2 * ns, :] = lf * LOG2E
            gr_ref[0, c, 2 * ns:3 * ns, :] = jnp.broadcast_to(f_all * LOG2E, (ns, lc))
            gr_ref[0, c, 3 * ns:, :] = jnp.zeros((GATE_ROWS - 3 * ns, lc), f32)

    n_qk = 2 * M_WIDTH // PROJ_COLS
    n_v = M_WIDTH // PROJ_COLS
    light = [attn_item(n) for n in range(3 * N_GROUPS)] + [v_item(c) for c in range(n_v)]
    items = [((lambda: _dot_nt(wift_ref[...], hm)), gate_epilogue)]
    for c in range(n_qk):
        items.append(qk_item(c))
        items.append(light.pop())
    items += light

    yield
    pending = None
    for matmul, epilogue in items:
        val = matmul()
        if pending is not None:
            pending()
        pending = functools.partial(epilogue, val)
        yield
    pending()
    yield


PROJ_N_STEPS = 3 + 2 * M_WIDTH // PROJ_COLS + M_WIDTH // PROJ_COLS + 3 * N_GROUPS


def _proj_parts(x, pre_w, conv_w, conv_b, wqk, wv, wift, gate_b, wa):
    b_, s_, _ = x.shape
    tm = PROJ_ROWS
    hb = tm // PROJ_HALO
    n_halo_blocks = s_ // PROJ_HALO
    row_spec = lambda width: pl.BlockSpec((1, tm, width), lambda b, i: (b, i, 0))
    in_specs = [
        row_spec(D_MODEL),
        pl.BlockSpec((1, PROJ_HALO, D_MODEL), lambda b, i: (b, jnp.maximum(i * hb - 1, 0), 0)),
        pl.BlockSpec((1, PROJ_HALO, D_MODEL), lambda b, i: (b, jnp.minimum((i + 1) * hb, n_halo_blocks - 1), 0)),
        _const_spec(pre_w.shape), _const_spec(conv_w.shape), _const_spec(conv_b.shape),
        _const_spec(wqk.shape), _const_spec(wv.shape), _const_spec(wift.shape), _const_spec(gate_b.shape),
        _const_spec(wa.shape),
    ]
    act = lambda width: jax.ShapeDtypeStruct((b_, s_, width), bf16)
    chunk_spec = lambda height: pl.BlockSpec((1, tm // M_CHUNK, height, M_CHUNK), lambda b, i: (b, i, 0, 0))
    out_shape = [act(M_WIDTH), jax.ShapeDtypeStruct((b_, s_ // M_CHUNK, M_WIDTH, M_CHUNK), bf16), act(M_WIDTH),
                 jax.ShapeDtypeStruct((b_, s_ // M_CHUNK, GATE_ROWS, M_CHUNK), f32)]
    out_specs = [row_spec(M_WIDTH), chunk_spec(M_WIDTH), row_spec(M_WIDTH), chunk_spec(GATE_ROWS)]
    for _, dil in A_PATTERNS:
        out_shape.append(jax.ShapeDtypeStruct((b_, s_ // dil, dil * 3 * A_WIDTH), bf16))
        out_specs.append(pl.BlockSpec((1, tm // dil, dil * 3 * A_WIDTH), lambda b, i: (b, i, 0)))
    scratch = ([pltpu.VMEM((A_WIDTH // LANES, tm, LANES), f32)] * 2
               + [pltpu.VMEM((PROJ_COLS // LANES, tm + 2 * PROJ_HALO, LANES), f32),
                  pltpu.VMEM((PROJ_COLS // LANES, tm, LANES), f32)])
    return dict(grid=(b_, s_ // tm), in_specs=in_specs, out_specs=out_specs, out_shape=out_shape, scratch=scratch,
                args=(x, x, x, pre_w, conv_w, conv_b, wqk, wv, wift, gate_b, wa))


def _proj(*operands):
    parts = _proj_parts(*operands)
    return pl.pallas_call(
        _proj_body,
        grid=parts["grid"],
        in_specs=parts["in_specs"],
        out_specs=parts["out_specs"],
        out_shape=parts["out_shape"],
        scratch_shapes=parts["scratch"],
        compiler_params=pltpu.CompilerParams(
            dimension_semantics=("parallel", "parallel"), vmem_limit_bytes=VMEM_LIMIT_BYTES),
        name="proj",
    )(*parts["args"])


def _mlstm_body(qf_ref, ktf_ref, vf_ref, grf_ref, qb_ref, ktb_ref, vb_ref, grb_ref,
                hf_ref, hb_ref, c_ref, n_ref, m_ref):
    refs = (qf_ref, ktf_ref, vf_ref, grf_ref, qb_ref, ktb_ref, vb_ref, grb_ref, hf_ref, hb_ref, c_ref, n_ref, m_ref)
    _mlstm_reset(pl.program_id(1), c_ref, n_ref, m_ref)

    def step_chunk(step, carry):
        _run(_mlstm_chunk_steps(step, MLSTM_STEP_CHUNKS, *refs))
        return carry

    lax.fori_loop(0, MLSTM_STEP_CHUNKS, step_chunk, 0)


def _mlstm_reset(j, c_ref, n_ref, m_ref):
    @pl.when(j == 0)
    def _():
        c_ref[...] = jnp.zeros_like(c_ref)
        n_ref[...] = jnp.zeros_like(n_ref)
        m_ref[...] = jnp.zeros_like(m_ref)


MLSTM_N_STEPS = 3 * 2 * M_HEADS


def _mlstm_chunk_steps(step, step_chunks, qf_ref, ktf_ref, vf_ref, grf_ref, qb_ref, ktb_ref, vb_ref, grb_ref,
                       hf_ref, hb_ref, c_ref, n_ref, m_ref):
    lc = M_CHUNK
    nh = M_HEADS
    ns = 2 * M_HEADS
    e = M_HEAD_DIM
    row = lax.broadcasted_iota(jnp.int32, (lc, lc), 0)
    col = lax.broadcasted_iota(jnp.int32, (lc, lc), 1)

    def head_cols(hh):
        return slice(hh * e, (hh + 1) * e)

    if True:
        streams = []
        for d in range(2):
            q_ref, kt_ref, v_ref, gr_ref, out_ref = (
                (qf_ref, ktf_ref, vf_ref, grf_ref, hf_ref) if d == 0 else (qb_ref, ktb_ref, vb_ref, grb_ref, hb_ref))
            mask = (col <= row) if d == 0 else (col >= row)
            chunk = step if d == 0 else step_chunks - 1 - step
            tok = (slice(chunk * lc, (chunk + 1) * lc) if isinstance(chunk, int)
                   else pl.ds(pl.multiple_of(chunk * lc, lc), lc))
            for hh in range(nh):
                streams.append((d * nh + hh, hh, q_ref, kt_ref, v_ref, gr_ref, out_ref, mask, chunk, tok))

        gated = []
        for ci, hh, q_ref, kt_ref, v_ref, gr_ref, out_ref, mask, chunk, tok in streams:
            r_row = gr_ref[0, chunk, ci:ci + 1, :]
            lf_row = gr_ref[0, chunk, ns + ci:ns + ci + 1, :]
            m_sc = m_ref[ci][:, 0:1]
            s_aug = _dot(q_ref[0, tok, head_cols(hh)],
                         jnp.concatenate([kt_ref[0, chunk, head_cols(hh), :], n_ref[ci].astype(bf16)], axis=1))
            r_mat = jnp.where(mask, r_row, NEG)
            u = jnp.maximum(m_sc, jnp.max(r_mat, axis=-1, keepdims=True))
            f_col = jnp.sum(jnp.where(mask, lf_row, 0.0), axis=-1, keepdims=True)
            p = s_aug[:, :lc] * jnp.exp2(r_mat - u)
            wa = jnp.exp2(m_sc - u)
            den = jnp.sum(p, axis=-1, keepdims=True) + wa * s_aug[:, lc:]
            inv = 1.0 / jnp.maximum(jnp.abs(den), jnp.exp2(-(f_col + u)))
            gated.append((p.astype(bf16), wa, inv))
            yield

        for (ci, hh, q_ref, kt_ref, v_ref, gr_ref, out_ref, mask, chunk, tok), (p, wa, inv) in zip(streams, gated):
            acc = (_dot(p, v_ref[0, tok, head_cols(hh)])
                   + wa * _dot(q_ref[0, tok, head_cols(hh)], c_ref[ci].astype(bf16)))
            for part in range(e // LANES):
                ps = slice(part * LANES, (part + 1) * LANES)
                out_ref[0, tok, hh * e + part * LANES:hh * e + (part + 1) * LANES] = (acc[:, ps] * inv).astype(bf16)
            yield

        for ci, hh, q_ref, kt_ref, v_ref, gr_ref, out_ref, mask, chunk, tok in streams:
            r_row = gr_ref[0, chunk, ci:ci + 1, :]
            f_all = gr_ref[0, chunk, 2 * ns + ci:2 * ns + ci + 1, :]
            m_old = m_ref[ci]
            g = f_all + r_row
            m_new = jnp.maximum(f_all + m_old, jnp.max(g, axis=-1, keepdims=True))
            kw = kt_ref[0, chunk, head_cols(hh), :].astype(f32) * jnp.exp2(g - m_new)
            decay = jnp.exp2(f_all + m_old - m_new)[:, 0:1]
            c_ref[ci] = decay * c_ref[ci] + _dot(kw.astype(bf16), v_ref[0, tok, head_cols(hh)])
            n_ref[ci] = decay * n_ref[ci] + jnp.sum(kw, axis=-1, keepdims=True)
            m_ref[ci] = m_new
            yield


def _mlstm_parts(q, kt, v, grow, step_chunks):
    b_, s_, _ = q.shape
    sc = step_chunks
    lc = M_CHUNK * sc
    nc = s_ // lc
    fwd = pl.BlockSpec((1, lc, M_WIDTH), lambda b, j: (b, j, 0))
    bwd = pl.BlockSpec((1, lc, M_WIDTH), lambda b, j: (b, nc - 1 - j, 0))
    fwd_t = lambda height: pl.BlockSpec((1, sc, height, M_CHUNK), lambda b, j: (b, j, 0, 0))
    bwd_t = lambda height: pl.BlockSpec((1, sc, height, M_CHUNK), lambda b, j: (b, nc - 1 - j, 0, 0))
    out = jax.ShapeDtypeStruct((b_, s_, M_WIDTH), bf16)
    scratch = [pltpu.VMEM((2 * M_HEADS, M_HEAD_DIM, M_HEAD_DIM), f32),
               pltpu.VMEM((2 * M_HEADS, M_HEAD_DIM, LANES), f32),
               pltpu.VMEM((2 * M_HEADS, 1, LANES), f32)]
    return dict(grid=(b_, nc), out_specs=[fwd, bwd], out_shape=[out, out], scratch=scratch,
                in_specs=[fwd, fwd_t(M_WIDTH), fwd, fwd_t(GATE_ROWS), bwd, bwd_t(M_WIDTH), bwd, bwd_t(GATE_ROWS)],
                args=(q, kt, v, grow, q, kt, v, grow))


def _mlstm(q, kt, v, grow):
    parts = _mlstm_parts(q, kt, v, grow, MLSTM_STEP_CHUNKS)
    return pl.pallas_call(
        _mlstm_body,
        grid=parts["grid"],
        in_specs=parts["in_specs"],
        out_specs=parts["out_specs"],
        out_shape=parts["out_shape"],
        scratch_shapes=parts["scratch"],
        compiler_params=pltpu.CompilerParams(
            dimension_semantics=("parallel", "arbitrary"), vmem_limit_bytes=VMEM_LIMIT_BYTES),
        name="mlstm",
    )(*parts["args"])


def _respec(specs, adapt):
    return [pl.BlockSpec(s.block_shape, (lambda step, f=s.index_map: f(*adapt(step))), pipeline_mode=s.pipeline_mode)
            for s in specs]


def _proj_mlstm_body(*refs, proj_tiles, scan_steps, step_chunks):
    step = pl.program_id(0)
    n_pi, n_mi, n_po, n_mo, n_ps = PROJ_N_IN, 8, PROJ_N_OUT, 2, PROJ_N_SCRATCH
    bounds = np.cumsum([0, n_pi, n_mi, n_po, n_mo, n_ps])
    p_in, m_in, p_out, m_out, p_scr = [refs[bounds[k]:bounds[k + 1]] for k in range(5)]
    m_scr = refs[bounds[5]:]
    scan_refs = (*m_in, *m_out, *m_scr)
    _mlstm_reset(step % scan_steps, *m_scr)

    def scan():
        for st in range(step_chunks):
            yield from _mlstm_chunk_steps(st, step_chunks, *scan_refs)

    _interleave(_proj_steps(step % proj_tiles, proj_tiles, *p_in, *p_out, *p_scr), scan(),
                PROJ_N_STEPS, step_chunks * MLSTM_N_STEPS)


def _proj_mlstm(proj_operands, scan_operands):
    x = proj_operands[0]
    q = scan_operands[0]
    n_steps = x.shape[0] * (x.shape[1] // PROJ_ROWS)
    total_chunks = q.shape[0] * (q.shape[1] // M_CHUNK)
    step_chunks = total_chunks // n_steps
    assert step_chunks * n_steps == total_chunks and (q.shape[1] // M_CHUNK) % step_chunks == 0
    pp = _proj_parts(*proj_operands)
    mp = _mlstm_parts(*scan_operands, step_chunks)
    proj_tiles = pp["grid"][1]
    scan_steps = mp["grid"][1]
    p_adapt = lambda step: (step // proj_tiles, step % proj_tiles)
    m_adapt = lambda step: (step // scan_steps, step % scan_steps)
    outs = pl.pallas_call(
        functools.partial(_proj_mlstm_body, proj_tiles=proj_tiles, scan_steps=scan_steps, step_chunks=step_chunks),
        grid=(n_steps,),
        in_specs=_respec(pp["in_specs"], p_adapt) + _respec(mp["in_specs"], m_adapt),
        out_specs=_respec(pp["out_specs"], p_adapt) + _respec(mp["out_specs"], m_adapt),
        out_shape=pp["out_shape"] + mp["out_shape"],
        scratch_shapes=pp["scratch"] + mp["scratch"],
        compiler_params=pltpu.CompilerParams(
            dimension_semantics=("arbitrary",), vmem_limit_bytes=VMEM_LIMIT_BYTES),
        name="proj_mlstm",
    )(*pp["args"], *mp["args"])
    return outs[:PROJ_N_OUT], outs[PROJ_N_OUT:]


def _attn_blocks(q_ref, kwin, vwin, bias_ref, first_keys, sub_len):
    qb = ATT_QBLOCK
    win = qb + 2 * A_HALF
    lane_q = lax.broadcasted_iota(jnp.int32, (qb, LANES), 1)
    lo_q = lane_q < A_HEAD_DIM
    lo_w = lax.broadcasted_iota(jnp.int32, (win, LANES), 1) < A_HEAD_DIM
    zq = jnp.zeros((qb, LANES), bf16)
    zw = jnp.zeros((win, LANES), bf16)
    npair = A_HEADS // 2
    units = [(b, p, b * qb, slice(p * LANES, (p + 1) * LANES)) for b in range(len(first_keys)) for p in range(npair)]

    raws = []
    for b, p, r0, ps in units:
        qp = q_ref[0, r0:r0 + qb, ps]
        q2 = jnp.concatenate([jnp.where(lo_q, qp, zq), jnp.where(lo_q, zq, qp)], axis=0)
        raws.append(_dot_nt(q2, kwin[r0:r0 + win, ps]))

    edges = []
    for first_key in first_keys:
        kpos = first_key + lax.broadcasted_iota(jnp.int32, (1, win), 1)
        edges.append(jnp.where((kpos >= 0) & (kpos < sub_len), 0.0, NEG).astype(f32))

    probs = []
    stats = [jnp.zeros((qb, LANES), f32) for _ in first_keys]
    for (b, p, r0, ps), raw in zip(units, raws):
        s = raw + bias_ref[p] + edges[b]
        mx = jnp.max(s, axis=-1, keepdims=True)
        pe = jnp.exp2(s - mx)
        den = jnp.sum(pe, axis=-1, keepdims=True)
        probs.append(pe.astype(bf16))
        st = stats[b]
        st = jnp.where(lane_q == 2 * p, mx[:qb], st)
        st = jnp.where(lane_q == 2 * p + 1, mx[qb:], st)
        st = jnp.where(lane_q == A_HEADS + 2 * p, den[:qb], st)
        stats[b] = jnp.where(lane_q == A_HEADS + 2 * p + 1, den[qb:], st)

    outs = [[] for _ in first_keys]
    for (b, p, r0, ps), pb in zip(units, probs):
        vp = vwin[r0:r0 + win, ps]
        outs[b].append(_dot(pb[:qb], jnp.where(lo_w, vp, zw)) + _dot(pb[qb:], jnp.where(lo_w, zw, vp)))
    return list(zip(outs, stats))


def _attn_dilated_body(q_ref, k_ref, kp_ref, kn_ref, v_ref, vp_ref, vn_ref, bias_ref, o_ref, st_ref,
                       o_scr, st_scr, *, sub_len, step_rows, dil):
    i = pl.program_id(1)
    r = pl.program_id(2)
    qb = ATT_QBLOCK
    kwin = jnp.concatenate([kp_ref[0], k_ref[0], kn_ref[0]], axis=0)
    vwin = jnp.concatenate([vp_ref[0], v_ref[0], vn_ref[0]], axis=0)
    first_keys = [i * step_rows + blk * qb - A_HALF for blk in range(step_rows // qb)]
    for blk, (outs, stats) in enumerate(_attn_blocks(q_ref, kwin, vwin, bias_ref, first_keys, sub_len)):
        r0 = blk * qb
        rows = pl.ds(r0 * dil + r, qb, stride=dil)
        for p, o in enumerate(outs):
            o_scr[p, rows, :] = o
        st_scr[rows, :] = stats

    @pl.when(r == dil - 1)
    def _():
        for p in range(A_HEADS // 2):
            o_ref[0, :, p * LANES:(p + 1) * LANES] = o_scr[p].astype(bf16)
        st_ref[0] = st_scr[...]


def _attn_mix_body(q_ref, k_ref, kp_ref, kn_ref, v_ref, vp_ref, vn_ref, bias_ref,
                   o1_ref, st1_ref, o2_ref, st2_ref, expand_ref, y_ref, *, sub_len, step_rows):
    i = pl.program_id(1)
    qb = ATT_QBLOCK
    kwin = jnp.concatenate([kp_ref[0], k_ref[0], kn_ref[0]], axis=0)
    vwin = jnp.concatenate([vp_ref[0], v_ref[0], vn_ref[0]], axis=0)
    head_lane = lax.broadcasted_iota(jnp.int32, (qb, LANES), 1) < A_HEADS
    first_keys = [i * step_rows + blk * qb - A_HALF for blk in range(step_rows // qb)]
    for blk, (outs, st0) in enumerate(_attn_blocks(q_ref, kwin, vwin, bias_ref, first_keys, sub_len)):
        r0 = blk * qb
        st1 = st1_ref[0, r0:r0 + qb, :]
        st2 = st2_ref[0, r0:r0 + qb, :]
        dn0, dn1, dn2 = [pltpu.roll(st, LANES - A_HEADS, axis=1) for st in (st0, st1, st2)]
        top = jnp.maximum(jnp.maximum(st0, st1), st2)
        w0, w1, w2 = jnp.exp2(st0 - top), jnp.exp2(st1 - top), jnp.exp2(st2 - top)
        dsum = w0 * dn0 + w1 * dn1 + w2 * dn2
        o_groups = (jnp.concatenate(outs, axis=1),
                    o1_ref[0, r0:r0 + qb, :].astype(f32), o2_ref[0, r0:r0 + qb, :].astype(f32))
        coef = []
        for w in (w0, w1, w2):
            c = jnp.where(head_lane, w / dsum, 0.0)
            c_hi = c.astype(bf16)
            coef.append(jnp.concatenate([c_hi, (c - c_hi.astype(f32)).astype(bf16)], axis=1))
        wide = _dot(jnp.concatenate(coef, axis=0), expand_ref[...])
        y = (wide[:qb] * o_groups[0] + wide[qb:2 * qb] * o_groups[1] + wide[2 * qb:] * o_groups[2])
        y_ref[0, r0:r0 + qb, :] = y.astype(bf16)


def _attn_halo_specs(step_rows, sub_len, width, index):
    hpb = step_rows // A_HALF
    last = sub_len // A_HALF - 1
    before = pl.BlockSpec((1, A_HALF, width), lambda *g: index(g, jnp.maximum(g[1] * hpb - 1, 0)))
    after = pl.BlockSpec((1, A_HALF, width), lambda *g: index(g, jnp.minimum((g[1] + 1) * hpb, last)))
    return before, after


def _attn_dilated(qkv, bias, dil):
    b_, sub_len, _ = qkv.shape
    s_ = sub_len * dil
    step_rows = min(ATT_STEP_ROWS[dil], sub_len)
    tokens = step_rows * dil
    in_specs = []
    for comp in range(3):
        in_specs.append(pl.BlockSpec((1, step_rows, A_WIDTH), lambda b, i, r, comp=comp: (b, i, 3 * r + comp)))
        if comp > 0:
            in_specs += _attn_halo_specs(step_rows, sub_len, A_WIDTH,
                                         lambda g, row, comp=comp: (g[0], row, 3 * g[2] + comp))
    return pl.pallas_call(
        functools.partial(_attn_dilated_body, sub_len=sub_len, step_rows=step_rows, dil=dil),
        grid=(b_, sub_len // step_rows, dil),
        in_specs=in_specs + [_const_spec(bias.shape)],
        out_specs=[pl.BlockSpec((1, tokens, A_WIDTH), lambda b, i, r: (b, i, 0)),
                   pl.BlockSpec((1, tokens, LANES), lambda b, i, r: (b, i, 0))],
        out_shape=[jax.ShapeDtypeStruct((b_, s_, A_WIDTH), bf16), jax.ShapeDtypeStruct((b_, s_, LANES), f32)],
        scratch_shapes=[pltpu.VMEM((A_WIDTH // LANES, tokens, LANES), f32), pltpu.VMEM((tokens, LANES), f32)],
        compiler_params=pltpu.CompilerParams(
            dimension_semantics=("parallel", "parallel", "arbitrary"), vmem_limit_bytes=VMEM_LIMIT_BYTES),
        name=f"attn_d{dil}",
    )(*([qkv] * 7), bias)


def _attn_mix(qkv, bias, o1, st1, o2, st2, expand):
    b_, s_, _ = qkv.shape
    step_rows = min(ATT_STEP_ROWS[1], s_)
    main = lambda width: pl.BlockSpec((1, step_rows, width), lambda b, i: (b, i, 0))
    in_specs = []
    for comp in range(3):
        in_specs.append(pl.BlockSpec((1, step_rows, A_WIDTH), lambda b, i, comp=comp: (b, i, comp)))
        if comp > 0:
            in_specs += _attn_halo_specs(step_rows, s_, A_WIDTH, lambda g, row, comp=comp: (g[0], row, comp))
    return pl.pallas_call(
        functools.partial(_attn_mix_body, sub_len=s_, step_rows=step_rows),
        grid=(b_, s_ // step_rows),
        in_specs=in_specs + [_const_spec(bias.shape),
                             main(A_WIDTH), main(LANES), main(A_WIDTH), main(LANES), _const_spec(expand.shape)],
        out_specs=main(A_WIDTH),
        out_shape=jax.ShapeDtypeStruct((b_, s_, A_WIDTH), bf16),
        compiler_params=pltpu.CompilerParams(
            dimension_semantics=("parallel", "parallel"), vmem_limit_bytes=VMEM_LIMIT_BYTES),
        name="attn_mix",
    )(*([qkv] * 7), bias, o1, st1, o2, st2, expand)


def _final_body(x_ref, hf_ref, hb_ref, ya_ref, prew_ref, w4_ref, hnw_ref, wpm_ref, wpa_ref, wout_ref, postw_ref,
                y_ref):
    x = x_ref[0]
    h = _rms(x, prew_ref[...]).astype(bf16)
    hsum = hf_ref[0].astype(f32) + hb_ref[0].astype(f32)
    off_z, off_az, off_ga = M_WIDTH, 2 * M_WIDTH, 2 * M_WIDTH + A_WIDTH
    off_gb = off_ga + D_MODEL
    parts = []
    for hh in range(M_HEADS):
        hs = slice(hh * M_HEAD_DIM, (hh + 1) * M_HEAD_DIM)
        o = _dot(h, w4_ref[:, hs])
        z = _dot(h, w4_ref[:, off_z + hh * M_HEAD_DIM:off_z + (hh + 1) * M_HEAD_DIM])
        hx = hsum[:, hs]
        hn = hx * lax.rsqrt(jnp.mean(hx * hx, axis=-1, keepdims=True) + EPS) * hnw_ref[:, hs]
        parts.append((hn * _sigmoid(o) * _silu(z)).astype(bf16))
    ym = jnp.concatenate(parts, axis=1)
    az = _dot(h, w4_ref[:, off_az:off_az + A_WIDTH])
    ya = (ya_ref[0].astype(f32) * _silu(az)).astype(bf16)
    pm = _dot(ym, wpm_ref[...])
    pa = _dot(ya, wpa_ref[...])
    ga = _sigmoid(_dot(h, w4_ref[:, off_ga:off_ga + D_MODEL]))
    gb = _sigmoid(_dot(h, w4_ref[:, off_gb:off_gb + D_MODEL]))
    merged = (ga * pm + gb * pa).astype(bf16)
    out = _dot(merged, wout_ref[...])
    y_ref[0] = x + _rms(out, postw_ref[...])


def _final(x, hf, hb, ya, pre_w, w4, hnw, wpm, wpa, wout, post_w):
    b_, s_, _ = x.shape
    tm = FINAL_ROWS
    row_spec = lambda width: pl.BlockSpec((1, tm, width), lambda b, i: (b, i, 0))
    consts = [pre_w, w4, hnw, wpm, wpa, wout, post_w]
    return pl.pallas_call(
        _final_body,
        grid=(b_, s_ // tm),
        in_specs=[row_spec(D_MODEL), row_spec(M_WIDTH), row_spec(M_WIDTH), row_spec(A_WIDTH)]
                 + [_const_spec(c.shape) for c in consts],
        out_specs=row_spec(D_MODEL),
        out_shape=jax.ShapeDtypeStruct(x.shape, x.dtype),
        compiler_params=pltpu.CompilerParams(
            dimension_semantics=("parallel", "parallel"), vmem_limit_bytes=VMEM_LIMIT_BYTES),
        name="final",
    )(x, hf, hb, ya, *consts)


def _t5_bucket(rel):
    nb = N_BUCKETS // 2
    exact = nb // 2
    n = np.abs(rel)
    large = exact + (np.log(np.maximum(n, 1) / exact) / math.log(MAX_DISTANCE / exact) * (nb - exact)).astype(np.int32)
    large = np.minimum(large, nb - 1)
    return (rel > 0).astype(np.int32) * nb + np.where(n < exact, n, large)


def _attn_bias(rel_table, g, dil):
    win = ATT_QBLOCK + 2 * A_HALF
    off = np.arange(win)[None, :] - A_HALF - np.arange(ATT_QBLOCK)[:, None]
    band = np.abs(off) <= A_HALF
    bucket = np.where(band, _t5_bucket(off * dil), -1)
    onehot = jnp.asarray(bucket[None] == np.arange(N_BUCKETS)[:, None, None])
    table = rel_table.astype(f32)[:, g, :]
    bias = jnp.sum(jnp.where(onehot[:, None], table[:, :, None, None], 0.0), axis=0)
    bias = jnp.where(band[None], bias * LOG2E, NEG)
    return bias.reshape(A_HEADS // 2, 2 * ATT_QBLOCK, win)


def _head_expand_matrix():
    e = np.zeros((2 * LANES, A_WIDTH), np.float32)
    for h in range(A_HEADS):
        e[h, h * A_HEAD_DIM:(h + 1) * A_HEAD_DIM] = 1.0
        e[LANES + h, h * A_HEAD_DIM:(h + 1) * A_HEAD_DIM] = 1.0
    return jnp.asarray(e, bf16)


def _proj_operands(x, p):
    return (x, p["pre_w"], p["conv_w"], p["conv_b"], p["wqk"], p["wv"], p["wift"], p["bias_r"], p["wa"])


def _attention(a, p):
    dilated = []
    for g, (_, dil) in enumerate(A_PATTERNS):
        if dil > 1:
            dilated += _attn_dilated(a[g], p["attn_bias"][g], dil)
    return _attn_mix(a[0], p["attn_bias"][0], *dilated, _head_expand_matrix())


def _output(x, hf, hb, ya, p):
    return _final(x, hf, hb, ya, p["pre_w"], p["w4"], p["hnw"], p["wpm"], p["wpa"], p["wout"], p["post_w"])


def _layer_pair(x_first, x_second, p):
    *scan_in, a_first = _split_proj(_proj(*_proj_operands(x_first, p)))
    proj_second, (hf, hb) = _proj_mlstm(_proj_operands(x_second, p), tuple(scan_in))
    y_first = _output(x_first, hf, hb, _attention(a_first, p), p)
    *scan_in, a_second = _split_proj(proj_second)
    hf, hb = _mlstm(*scan_in)
    return y_first, _output(x_second, hf, hb, _attention(a_second, p), p)


def _split_proj(outs):
    q, kt, v, grow, *a = outs
    return q, kt, v, grow, a


def kernel(x_prompt, x_sample, pre_norm_w, w_in, m_conv_w, m_conv_b, m_igate_b, m_fgate_b, m_head_norm_w,
           w_proj_m, w_proj_a, w_out, post_norm_w, rel_bias_table):
    depth = pre_norm_w.shape[0]
    params = []
    for l in range(depth):
        w = w_in[l]
        n_query = N_GROUPS * A_WIDTH
        wa = jnp.concatenate([w[:, OFF_AQKV:OFF_AQKV + n_query] * (A_HEAD_DIM ** -0.5 * LOG2E),
                              w[:, OFF_AQKV + n_query:OFF_AZ]], axis=1)
        gate_b = jnp.concatenate([m_igate_b[l].reshape(-1), m_fgate_b[l].reshape(-1)]).astype(f32)
        params.append(dict(
            pre_w=pre_norm_w[l].reshape(1, D_MODEL), post_w=post_norm_w[l].reshape(1, D_MODEL),
            conv_w=m_conv_w[l], conv_b=m_conv_b[l].reshape(1, 2 * M_WIDTH),
            wqk=w[:, OFF_QK:OFF_V].astype(bf16), wv=w[:, OFF_V:OFF_O].astype(bf16),
            wift=w[:, OFF_I:OFF_AQKV].T.astype(bf16), wa=wa.astype(bf16),
            bias_r=gate_b.reshape(4 * M_HEADS, 1),
            w4=jnp.concatenate([w[:, OFF_O:OFF_I], w[:, OFF_AZ:]], axis=1).astype(bf16),
            hnw=m_head_norm_w[l].reshape(1, M_WIDTH),
            wpm=w_proj_m[l].astype(bf16), wpa=w_proj_a[l].astype(bf16), wout=w_out[l].astype(bf16),
            attn_bias=[_attn_bias(rel_bias_table, g, dil) for g, (_, dil) in enumerate(A_PATTERNS)],
        ))

    for p in params:
        x_prompt, x_sample = _layer_pair(x_prompt, x_sample, p)
    return (x_prompt, x_sample)
```

```python
import functools
import math

import numpy as np
import jax
import jax.numpy as jnp
from jax import lax
from jax.experimental import pallas as pl
from jax.experimental.pallas import tpu as pltpu

D_MODEL = 1024
M_HEADS = 4
M_HEAD_DIM = 256
M_WIDTH = M_HEADS * M_HEAD_DIM
M_CHUNK = 128
A_PATTERNS = ((128, 1), (512, 4), (2048, 16))
N_GROUPS = 3
A_HEADS = 8
A_HEAD_DIM = 64
A_WIDTH = A_HEADS * A_HEAD_DIM
A_HALF = 64
N_BUCKETS = 32
MAX_DISTANCE = 1024
EPS = 1e-6
NEG = -1e30
LOG2E = math.log2(math.e)

OFF_QK = 0
OFF_V = 2 * M_WIDTH
OFF_O = OFF_V + M_WIDTH
OFF_Z = OFF_O + M_WIDTH
OFF_I = OFF_Z + M_WIDTH
OFF_F = OFF_I + 2 * M_HEADS
OFF_AQKV = OFF_F + 2 * M_HEADS
OFF_AZ = OFF_AQKV + 3 * N_GROUPS * A_WIDTH
OFF_GATE = OFF_AZ + A_WIDTH

LANES = 128
F32_SUBLANES = 8
STRIDE_ONE_OP = 4
VMEM_LIMIT_BYTES = 60 * 1024 * 1024

GATE_ROWS = 32
PROJ_ROWS = 512
PROJ_HALO = 8
PROJ_COLS = 256
MLSTM_STEP_CHUNKS = 4
FINAL_ROWS = 512
FUSED_FINAL_ROWS = 256
ATT_QBLOCK = 128
ATT_STEP_ROWS = {1: 512, 4: 512, 16: 256}

f32 = jnp.float32
bf16 = jnp.bfloat16


def _const_spec(shape):
    nd = len(shape)
    return pl.BlockSpec(shape, lambda *_: (0,) * nd, pipeline_mode=pl.Buffered(1))


def _dot(a, b):
    return jnp.dot(a, b, preferred_element_type=f32)


def _dot_nt(a, b):
    return lax.dot_general(a, b, (((1,), (1,)), ((), ())), preferred_element_type=f32)


def _dot_tn(a, b):
    return lax.dot_general(a, b, (((0,), (0,)), ((), ())), preferred_element_type=f32)


def _dot_exact(a, b):
    return jnp.dot(a, b, preferred_element_type=f32, precision=lax.Precision.HIGHEST)


def _rms(x, w):
    return x * lax.rsqrt(jnp.mean(x * x, axis=-1, keepdims=True) + EPS) * w


def _sigmoid(x):
    return 0.5 + 0.5 * jnp.tanh(0.5 * x)


def _silu(x):
    half = 0.5 * x
    return half + half * jnp.tanh(half)


PROJ_N_IN, PROJ_N_OUT, PROJ_N_SCRATCH = 11, 4 + N_GROUPS, 4


def _run(steps):
    for _ in steps:
        pass


def _interleave(first, second, n_first, n_second):
    done = [0, 0]
    gens = [first, second]
    total = [n_first, n_second]
    live = [True, True]
    while live[0] or live[1]:
        pick = 0 if (live[0] and (not live[1] or done[0] * total[1] <= done[1] * total[0])) else 1
        try:
            next(gens[pick])
            done[pick] += 1
        except StopIteration:
            live[pick] = False


def _proj_body(*refs):
    _run(_proj_steps(pl.program_id(1), pl.num_programs(1), *refs))


def _proj_steps(i, ni, x_ref, xp_ref, xn_ref, prew_ref, cw_ref, cb_ref, wqk_ref, wv_ref, wift_ref, gb_ref, wa_ref,
                q_ref, kt_ref, v_ref, gr_ref, *a_refs):
    tm = x_ref.shape[1]
    keep_prev = (i > 0).astype(f32)
    keep_next = (i < ni - 1).astype(f32)
    xa = jnp.concatenate([xp_ref[0] * keep_prev, x_ref[0], xn_ref[0] * keep_next], axis=0)
    hf = _rms(xa, prew_ref[...])
    h_ext = hf.astype(bf16)
    hm = hf[PROJ_HALO:PROJ_HALO + tm].astype(bf16)
    a_refs, (slab_ref, slab2_ref, conv_ref, act_ref) = a_refs[:-4], a_refs[-4:]
    nslab = A_WIDTH // LANES

    def qk_item(c):
        cs = slice(c * PROJ_COLS, (c + 1) * PROJ_COLS)

        def epilogue(r):
            half = tm // 2
            for sl in range(PROJ_COLS // LANES):
                conv_ref[sl] = r[:, sl * LANES:(sl + 1) * LANES]
            for sl in range(PROJ_COLS // LANES):
                col = slice(c * PROJ_COLS + sl * LANES, c * PROJ_COLS + (sl + 1) * LANES)
                w = cw_ref[:, col]
                even = conv_ref[sl, pl.ds(PROJ_HALO, half, stride=2), :]
                odd = conv_ref[sl, pl.ds(PROJ_HALO + 1, half, stride=2), :]
                odd_prev = conv_ref[sl, pl.ds(PROJ_HALO - 1, half, stride=2), :]
                even_next = conv_ref[sl, pl.ds(PROJ_HALO + 2, half, stride=2), :]
                y_even = cb_ref[:, col] + odd_prev * w[0:1] + even * w[1:2] + odd * w[2:3]
                y_odd = cb_ref[:, col] + even * w[0:1] + odd * w[1:2] + even_next * w[2:3]
                act_ref[sl, pl.ds(0, half, stride=2), :] = _silu(y_even)
                act_ref[sl, pl.ds(1, half, stride=2), :] = _silu(y_odd)
                if c * PROJ_COLS < M_WIDTH:
                    q_ref[0, :, col] = act_ref[sl].astype(bf16)
                else:
                    rows = slice(col.start - M_WIDTH, col.stop - M_WIDTH)
                    kt = (act_ref[sl] * (M_HEAD_DIM ** -0.5)).T.astype(bf16)
                    for ch in range(tm // M_CHUNK):
                        kt_ref[0, ch, rows, :] = kt[:, ch * M_CHUNK:(ch + 1) * M_CHUNK]

        return (lambda: _dot(h_ext, wqk_ref[:, cs])), epilogue

    def v_item(c):
        cs = slice(c * PROJ_COLS, (c + 1) * PROJ_COLS)

        def epilogue(res):
            v_ref[0, :, cs] = res.astype(bf16)

        return (lambda: _dot(hm, wv_ref[:, cs])), epilogue

    def attn_item(n):
        g, comp = divmod(n, 3)
        a_ref = a_refs[g]
        dil = A_PATTERNS[g][1]
        w_cols = slice((comp * N_GROUPS + g) * A_WIDTH, (comp * N_GROUPS + g + 1) * A_WIDTH)

        def epilogue(res):
            if dil == 1:
                a_ref[0, :, comp * A_WIDTH:(comp + 1) * A_WIDTH] = res.astype(bf16)
                return
            for sl in range(nslab):
                slab_ref[sl] = res[:, sl * LANES:(sl + 1) * LANES]
            src_ref, groups = slab_ref, [(0, 0)]
            stride = dil
            if dil > STRIDE_ONE_OP:
                stride = dil // STRIDE_ONE_OP
                part = tm // STRIDE_ONE_OP
                for r0 in range(STRIDE_ONE_OP):
                    for sl in range(nslab):
                        slab2_ref[sl, r0 * part:(r0 + 1) * part, :] = (
                            slab_ref[sl, pl.ds(r0, part, stride=STRIDE_ONE_OP), :])
                src_ref = slab2_ref
                groups = [(r0 * part, r0) for r0 in range(STRIDE_ONE_OP)]
            for base, r0 in groups:
                for r1 in range(stride):
                    r = r1 * (dil // stride) + r0
                    for sl in range(nslab):
                        c0 = (r * 3 + comp) * A_WIDTH + sl * LANES
                        a_ref[0, :, c0:c0 + LANES] = (
                            src_ref[sl, pl.ds(base + r1, tm // dil, stride=stride), :].astype(bf16))

        return (lambda: _dot(hm, wa_ref[:, w_cols])), epilogue

    def gate_epilogue(gpre):
        ns = 2 * M_HEADS
        lc = M_CHUNK
        sub = lax.broadcasted_iota(jnp.int32, (2 * ns, tm), 0)
        lrow = jnp.where(sub >= ns, jax.nn.log_sigmoid(gpre + gb_ref[...]), gpre + gb_ref[...])
        tri_r = lax.broadcasted_iota(jnp.int32, (lc, lc), 0)
        tri_c = lax.broadcasted_iota(jnp.int32, (lc, lc), 1)
        upper = (tri_c >= tri_r).astype(f32)
        lower = (tri_c <= tri_r).astype(f32)
        fwd_rows = lax.broadcasted_iota(jnp.int32, (ns, lc), 0) < M_HEADS
        for c in range(tm // lc):
            ls = slice(c * lc, (c + 1) * lc)
            blk = lrow[:, ls]
            lf = blk[ns:]
            f_cum = jnp.where(fwd_rows, _dot_exact(lf, upper), _dot_exact(lf, lower))
            f_all = jnp.sum(lf, axis=-1, keepdims=True)
            gr_ref[0, c, 0:ns, :] = (blk[:ns] - f_cum) * LOG2E
            gr_ref[0, c, ns:2 * ns, :] = lf * LOG2E
            gr_ref[0, c, 2 * ns:3 * ns, :] = jnp.broadcast_to(f_all * LOG2E, (ns, lc))
            gr_ref[0, c, 3 * ns:, :] = jnp.zeros((GATE_ROWS - 3 * ns, lc), f32)

    n_qk = 2 * M_WIDTH // PROJ_COLS
    n_v = M_WIDTH // PROJ_COLS
    light = [attn_item(n) for n in range(3 * N_GROUPS)] + [v_item(c) for c in range(n_v)]
    items = [((lambda: _dot_nt(wift_ref[...], hm)), gate_epilogue)]
    for c in range(n_qk):
        items.append(qk_item(c))
        items.append(light.pop())
    items += light

    yield
    pending = None
    for matmul, epilogue in items:
        val = matmul()
        if pending is not None:
            pending()
        pending = functools.partial(epilogue, val)
        yield
    pending()
    yield


PROJ_N_STEPS = 3 + 2 * M_WIDTH // PROJ_COLS + M_WIDTH // PROJ_COLS + 3 * N_GROUPS


def _proj_parts(x, pre_w, conv_w, conv_b, wqk, wv, wift, gate_b, wa):
    b_, s_, _ = x.shape
    tm = PROJ_ROWS
    hb = tm // PROJ_HALO
    n_halo_blocks = s_ // PROJ_HALO
    row_spec = lambda width: pl.BlockSpec((1, tm, width), lambda b, i: (b, i, 0))
    in_specs = [
        row_spec(D_MODEL),
        pl.BlockSpec((1, PROJ_HALO, D_MODEL), lambda b, i: (b, jnp.maximum(i * hb - 1, 0), 0)),
        pl.BlockSpec((1, PROJ_HALO, D_MODEL), lambda b, i: (b, jnp.minimum((i + 1) * hb, n_halo_blocks - 1), 0)),
        _const_spec(pre_w.shape), _const_spec(conv_w.shape), _const_spec(conv_b.shape),
        _const_spec(wqk.shape), _const_spec(wv.shape), _const_spec(wift.shape), _const_spec(gate_b.shape),
        _const_spec(wa.shape),
    ]
    act = lambda width: jax.ShapeDtypeStruct((b_, s_, width), bf16)
    chunk_spec = lambda height: pl.BlockSpec((1, tm // M_CHUNK, height, M_CHUNK), lambda b, i: (b, i, 0, 0))
    out_shape = [act(M_WIDTH), jax.ShapeDtypeStruct((b_, s_ // M_CHUNK, M_WIDTH, M_CHUNK), bf16), act(M_WIDTH),
                 jax.ShapeDtypeStruct((b_, s_ // M_CHUNK, GATE_ROWS, M_CHUNK), f32)]
    out_specs = [row_spec(M_WIDTH), chunk_spec(M_WIDTH), row_spec(M_WIDTH), chunk_spec(GATE_ROWS)]
    for _, dil in A_PATTERNS:
        out_shape.append(jax.ShapeDtypeStruct((b_, s_ // dil, dil * 3 * A_WIDTH), bf16))
        out_specs.append(pl.BlockSpec((1, tm // dil, dil * 3 * A_WIDTH), lambda b, i: (b, i, 0)))
    scratch = ([pltpu.VMEM((A_WIDTH // LANES, tm, LANES), f32)] * 2
               + [pltpu.VMEM((PROJ_COLS // LANES, tm + 2 * PROJ_HALO, LANES), f32),
                  pltpu.VMEM((PROJ_COLS // LANES, tm, LANES), f32)])
    return dict(grid=(b_, s_ // tm), in_specs=in_specs, out_specs=out_specs, out_shape=out_shape, scratch=scratch,
                args=(x, x, x, pre_w, conv_w, conv_b, wqk, wv, wift, gate_b, wa))


def _proj(*operands):
    parts = _proj_parts(*operands)
    return pl.pallas_call(
        _proj_body,
        grid=parts["grid"],
        in_specs=parts["in_specs"],
        out_specs=parts["out_specs"],
        out_shape=parts["out_shape"],
        scratch_shapes=parts["scratch"],
        compiler_params=pltpu.CompilerParams(
            dimension_semantics=("parallel", "parallel"), vmem_limit_bytes=VMEM_LIMIT_BYTES),
        name="proj",
    )(*parts["args"])


def _mlstm_body(qf_ref, ktf_ref, vf_ref, grf_ref, qb_ref, ktb_ref, vb_ref, grb_ref,
                hf_ref, hb_ref, c_ref, n_ref, m_ref):
    refs = (qf_ref, ktf_ref, vf_ref, grf_ref, qb_ref, ktb_ref, vb_ref, grb_ref, hf_ref, hb_ref, c_ref, n_ref, m_ref)
    _mlstm_reset(pl.program_id(1), c_ref, n_ref, m_ref)

    def step_chunk(step, carry):
        _run(_mlstm_chunk_steps(step, MLSTM_STEP_CHUNKS, *refs))
        return carry

    lax.fori_loop(0, MLSTM_STEP_CHUNKS, step_chunk, 0)


def _mlstm_reset(j, c_ref, n_ref, m_ref):
    @pl.when(j == 0)
    def _():
        c_ref[...] = jnp.zeros_like(c_ref)
        n_ref[...] = jnp.zeros_like(n_ref)
        m_ref[...] = jnp.zeros_like(m_ref)


MLSTM_N_STEPS = 3 * 2 * M_HEADS


def _mlstm_chunk_steps(step, step_chunks, qf_ref, ktf_ref, vf_ref, grf_ref, qb_ref, ktb_ref, vb_ref, grb_ref,
                       hf_ref, hb_ref, c_ref, n_ref, m_ref):
    lc = M_CHUNK
    nh = M_HEADS
    ns = 2 * M_HEADS
    e = M_HEAD_DIM
    row = lax.broadcasted_iota(jnp.int32, (lc, lc), 0)
    col = lax.broadcasted_iota(jnp.int32, (lc, lc), 1)

    def head_cols(hh):
        return slice(hh * e, (hh + 1) * e)

    if True:
        streams = []
        for d in range(2):
            q_ref, kt_ref, v_ref, gr_ref, out_ref = (
                (qf_ref, ktf_ref, vf_ref, grf_ref, hf_ref) if d == 0 else (qb_ref, ktb_ref, vb_ref, grb_ref, hb_ref))
            mask = (col <= row) if d == 0 else (col >= row)
            chunk = step if d == 0 else step_chunks - 1 - step
            tok = (slice(chunk * lc, (chunk + 1) * lc) if isinstance(chunk, int)
                   else pl.ds(pl.multiple_of(chunk * lc, lc), lc))
            for hh in range(nh):
                streams.append((d * nh + hh, hh, q_ref, kt_ref, v_ref, gr_ref, out_ref, mask, chunk, tok))

        gated = []
        for ci, hh, q_ref, kt_ref, v_ref, gr_ref, out_ref, mask, chunk, tok in streams:
            r_row = gr_ref[0, chunk, ci:ci + 1, :]
            lf_row = gr_ref[0, chunk, ns + ci:ns + ci + 1, :]
            m_sc = m_ref[ci][:, 0:1]
            s_aug = _dot(q_ref[0, tok, head_cols(hh)],
                         jnp.concatenate([kt_ref[0, chunk, head_cols(hh), :], n_ref[ci].astype(bf16)], axis=1))
            r_mat = jnp.where(mask, r_row, NEG)
            u = jnp.maximum(m_sc, jnp.max(r_mat, axis=-1, keepdims=True))
            f_col = jnp.sum(jnp.where(mask, lf_row, 0.0), axis=-1, keepdims=True)
            p = s_aug[:, :lc] * jnp.exp2(r_mat - u)
            wa = jnp.exp2(m_sc - u)
            den = jnp.sum(p, axis=-1, keepdims=True) + wa * s_aug[:, lc:]
            inv = 1.0 / jnp.maximum(jnp.abs(den), jnp.exp2(-(f_col + u)))
            gated.append((p.astype(bf16), wa, inv))
            yield

        for (ci, hh, q_ref, kt_ref, v_ref, gr_ref, out_ref, mask, chunk, tok), (p, wa, inv) in zip(streams, gated):
            acc = (_dot(p, v_ref[0, tok, head_cols(hh)])
                   + wa * _dot(q_ref[0, tok, head_cols(hh)], c_ref[ci].astype(bf16)))
            for part in range(e // LANES):
                ps = slice(part * LANES, (part + 1) * LANES)
                out_ref[0, tok, hh * e + part * LANES:hh * e + (part + 1) * LANES] = (acc[:, ps] * inv).astype(bf16)
            yield

        for ci, hh, q_ref, kt_ref, v_ref, gr_ref, out_ref, mask, chunk, tok in streams:
            r_row = gr_ref[0, chunk, ci:ci + 1, :]
            f_all = gr_ref[0, chunk, 2 * ns + ci:2 * ns + ci + 1, :]
            m_old = m_ref[ci]
            g = f_all + r_row
            m_new = jnp.maximum(f_all + m_old, jnp.max(g, axis=-1, keepdims=True))
            kw = kt_ref[0, chunk, head_cols(hh), :].astype(f32) * jnp.exp2(g - m_new)
            decay = jnp.exp2(f_all + m_old - m_new)[:, 0:1]
            c_ref[ci] = decay * c_ref[ci] + _dot(kw.astype(bf16), v_ref[0, tok, head_cols(hh)])
            n_ref[ci] = decay * n_ref[ci] + jnp.sum(kw, axis=-1, keepdims=True)
            m_ref[ci] = m_new
            yield


def _mlstm_parts(q, kt, v, grow, step_chunks):
    b_, s_, _ = q.shape
    sc = step_chunks
    lc = M_CHUNK * sc
    nc = s_ // lc
    fwd = pl.BlockSpec((1, lc, M_WIDTH), lambda b, j: (b, j, 0))
    bwd = pl.BlockSpec((1, lc, M_WIDTH), lambda b, j: (b, nc - 1 - j, 0))
    fwd_t = lambda height: pl.BlockSpec((1, sc, height, M_CHUNK), lambda b, j: (b, j, 0, 0))
    bwd_t = lambda height: pl.BlockSpec((1, sc, height, M_CHUNK), lambda b, j: (b, nc - 1 - j, 0, 0))
    out = jax.ShapeDtypeStruct((b_, s_, M_WIDTH), bf16)
    scratch = [pltpu.VMEM((2 * M_HEADS, M_HEAD_DIM, M_HEAD_DIM), f32),
               pltpu.VMEM((2 * M_HEADS, M_HEAD_DIM, LANES), f32),
               pltpu.VMEM((2 * M_HEADS, 1, LANES), f32)]
    return dict(grid=(b_, nc), out_specs=[fwd, bwd], out_shape=[out, out], scratch=scratch,
                in_specs=[fwd, fwd_t(M_WIDTH), fwd, fwd_t(GATE_ROWS), bwd, bwd_t(M_WIDTH), bwd, bwd_t(GATE_ROWS)],
                args=(q, kt, v, grow, q, kt, v, grow))


def _mlstm(q, kt, v, grow):
    parts = _mlstm_parts(q, kt, v, grow, MLSTM_STEP_CHUNKS)
    return pl.pallas_call(
        _mlstm_body,
        grid=parts["grid"],
        in_specs=parts["in_specs"],
        out_specs=parts["out_specs"],
        out_shape=parts["out_shape"],
        scratch_shapes=parts["scratch"],
        compiler_params=pltpu.CompilerParams(
            dimension_semantics=("parallel", "arbitrary"), vmem_limit_bytes=VMEM_LIMIT_BYTES),
        name="mlstm",
    )(*parts["args"])


def _respec(specs, adapt):
    return [pl.BlockSpec(s.block_shape, (lambda step, f=s.index_map: f(*adapt(step))), pipeline_mode=s.pipeline_mode)
            for s in specs]


def _with_scan_body(*refs, tile_steps, n_tile_steps, counts, tiles, scan_steps, step_chunks):
    step = pl.program_id(0)
    n_ti, n_to, n_ts = counts
    bounds = np.cumsum([0, n_ti, 8, n_to, 2, n_ts])
    t_in, m_in, t_out, m_out, t_scr = [refs[bounds[k]:bounds[k + 1]] for k in range(5)]
    m_scr = refs[bounds[5]:]
    scan_refs = (*m_in, *m_out, *m_scr)
    _mlstm_reset(step % scan_steps, *m_scr)

    def scan():
        for st in range(step_chunks):
            yield from _mlstm_chunk_steps(st, step_chunks, *scan_refs)

    _interleave(tile_steps(step % tiles, tiles, *t_in, *t_out, *t_scr), scan(),
                n_tile_steps, step_chunks * MLSTM_N_STEPS)


def _with_scan(name, parts, tile_steps, n_tile_steps, scan_operands):
    q = scan_operands[0]
    tiles = parts["grid"][1]
    n_steps = parts["grid"][0] * tiles
    seq_chunks = q.shape[1] // M_CHUNK
    step_chunks = q.shape[0] * seq_chunks // n_steps
    assert step_chunks * n_steps == q.shape[0] * seq_chunks and seq_chunks % step_chunks == 0
    mp = _mlstm_parts(*scan_operands, step_chunks)
    scan_steps = mp["grid"][1]
    t_adapt = lambda step: (step // tiles, step % tiles)
    m_adapt = lambda step: (step // scan_steps, step % scan_steps)
    n_out = len(parts["out_shape"])
    outs = pl.pallas_call(
        functools.partial(_with_scan_body, tile_steps=tile_steps, n_tile_steps=n_tile_steps,
                          counts=(len(parts["in_specs"]), n_out, len(parts["scratch"])),
                          tiles=tiles, scan_steps=scan_steps, step_chunks=step_chunks),
        grid=(n_steps,),
        in_specs=_respec(parts["in_specs"], t_adapt) + _respec(mp["in_specs"], m_adapt),
        out_specs=_respec(parts["out_specs"], t_adapt) + _respec(mp["out_specs"], m_adapt),
        out_shape=parts["out_shape"] + mp["out_shape"],
        scratch_shapes=parts["scratch"] + mp["scratch"],
        compiler_params=pltpu.CompilerParams(
            dimension_semantics=("arbitrary",), vmem_limit_bytes=VMEM_LIMIT_BYTES),
        name=name,
    )(*parts["args"], *mp["args"])
    return outs[:n_out], outs[n_out:]


def _attn_blocks(q_ref, kwin, vwin, bias_ref, first_keys, sub_len):
    qb = ATT_QBLOCK
    win = qb + 2 * A_HALF
    lane_q = lax.broadcasted_iota(jnp.int32, (qb, LANES), 1)
    lo_q = lane_q < A_HEAD_DIM
    lo_w = lax.broadcasted_iota(jnp.int32, (win, LANES), 1) < A_HEAD_DIM
    zq = jnp.zeros((qb, LANES), bf16)
    zw = jnp.zeros((win, LANES), bf16)
    npair = A_HEADS // 2
    units = [(b, p, b * qb, slice(p * LANES, (p + 1) * LANES)) for b in range(len(first_keys)) for p in range(npair)]

    raws = []
    for b, p, r0, ps in units:
        qp = q_ref[0, r0:r0 + qb, ps]
        q2 = jnp.concatenate([jnp.where(lo_q, qp, zq), jnp.where(lo_q, zq, qp)], axis=0)
        raws.append(_dot_nt(q2, kwin[r0:r0 + win, ps]))

    edges = []
    for first_key in first_keys:
        kpos = first_key + lax.broadcasted_iota(jnp.int32, (1, win), 1)
        edges.append(jnp.where((kpos >= 0) & (kpos < sub_len), 0.0, NEG).astype(f32))

    probs = []
    stats = [jnp.zeros((qb, LANES), f32) for _ in first_keys]
    for (b, p, r0, ps), raw in zip(units, raws):
        s = raw + bias_ref[p] + edges[b]
        mx = jnp.max(s, axis=-1, keepdims=True)
        pe = jnp.exp2(s - mx)
        den = jnp.sum(pe, axis=-1, keepdims=True)
        probs.append(pe.astype(bf16))
        st = stats[b]
        st = jnp.where(lane_q == 2 * p, mx[:qb], st)
        st = jnp.where(lane_q == 2 * p + 1, mx[qb:], st)
        st = jnp.where(lane_q == A_HEADS + 2 * p, den[:qb], st)
        stats[b] = jnp.where(lane_q == A_HEADS + 2 * p + 1, den[qb:], st)

    outs = [[] for _ in first_keys]
    for (b, p, r0, ps), pb in zip(units, probs):
        vp = vwin[r0:r0 + win, ps]
        outs[b].append(_dot(pb[:qb], jnp.where(lo_w, vp, zw)) + _dot(pb[qb:], jnp.where(lo_w, zw, vp)))
    return list(zip(outs, stats))


def _attn_dilated_body(q_ref, k_ref, kp_ref, kn_ref, v_ref, vp_ref, vn_ref, bias_ref, o_ref, st_ref,
                       o_scr, st_scr, *, sub_len, step_rows, dil):
    i = pl.program_id(1)
    r = pl.program_id(2)
    qb = ATT_QBLOCK
    kwin = jnp.concatenate([kp_ref[0], k_ref[0], kn_ref[0]], axis=0)
    vwin = jnp.concatenate([vp_ref[0], v_ref[0], vn_ref[0]], axis=0)
    first_keys = [i * step_rows + blk * qb - A_HALF for blk in range(step_rows // qb)]
    for blk, (outs, stats) in enumerate(_attn_blocks(q_ref, kwin, vwin, bias_ref, first_keys, sub_len)):
        r0 = blk * qb
        rows = pl.ds(r0 * dil + r, qb, stride=dil)
        for p, o in enumerate(outs):
            o_scr[p, rows, :] = o
        st_scr[rows, :] = stats

    @pl.when(r == dil - 1)
    def _():
        for p in range(A_HEADS // 2):
            o_ref[0, :, p * LANES:(p + 1) * LANES] = o_scr[p].astype(bf16)
        st_ref[0] = st_scr[...]


def _attn_mix_body(q_ref, k_ref, kp_ref, kn_ref, v_ref, vp_ref, vn_ref, bias_ref,
                   o1_ref, st1_ref, o2_ref, st2_ref, expand_ref, y_ref, *, sub_len, step_rows):
    i = pl.program_id(1)
    qb = ATT_QBLOCK
    kwin = jnp.concatenate([kp_ref[0], k_ref[0], kn_ref[0]], axis=0)
    vwin = jnp.concatenate([vp_ref[0], v_ref[0], vn_ref[0]], axis=0)
    head_lane = lax.broadcasted_iota(jnp.int32, (qb, LANES), 1) < A_HEADS
    first_keys = [i * step_rows + blk * qb - A_HALF for blk in range(step_rows // qb)]
    for blk, (outs, st0) in enumerate(_attn_blocks(q_ref, kwin, vwin, bias_ref, first_keys, sub_len)):
        r0 = blk * qb
        st1 = st1_ref[0, r0:r0 + qb, :]
        st2 = st2_ref[0, r0:r0 + qb, :]
        dn0, dn1, dn2 = [pltpu.roll(st, LANES - A_HEADS, axis=1) for st in (st0, st1, st2)]
        top = jnp.maximum(jnp.maximum(st0, st1), st2)
        w0, w1, w2 = jnp.exp2(st0 - top), jnp.exp2(st1 - top), jnp.exp2(st2 - top)
        dsum = w0 * dn0 + w1 * dn1 + w2 * dn2
        o_groups = (jnp.concatenate(outs, axis=1),
                    o1_ref[0, r0:r0 + qb, :].astype(f32), o2_ref[0, r0:r0 + qb, :].astype(f32))
        coef = []
        for w in (w0, w1, w2):
            c = jnp.where(head_lane, w / dsum, 0.0)
            c_hi = c.astype(bf16)
            coef.append(jnp.concatenate([c_hi, (c - c_hi.astype(f32)).astype(bf16)], axis=1))
        wide = _dot(jnp.concatenate(coef, axis=0), expand_ref[...])
        y = (wide[:qb] * o_groups[0] + wide[qb:2 * qb] * o_groups[1] + wide[2 * qb:] * o_groups[2])
        y_ref[0, r0:r0 + qb, :] = y.astype(bf16)


def _attn_halo_specs(step_rows, sub_len, width, index):
    hpb = step_rows // A_HALF
    last = sub_len // A_HALF - 1
    before = pl.BlockSpec((1, A_HALF, width), lambda *g: index(g, jnp.maximum(g[1] * hpb - 1, 0)))
    after = pl.BlockSpec((1, A_HALF, width), lambda *g: index(g, jnp.minimum((g[1] + 1) * hpb, last)))
    return before, after


def _attn_dilated(qkv, bias, dil):
    b_, sub_len, _ = qkv.shape
    s_ = sub_len * dil
    step_rows = min(ATT_STEP_ROWS[dil], sub_len)
    tokens = step_rows * dil
    in_specs = []
    for comp in range(3):
        in_specs.append(pl.BlockSpec((1, step_rows, A_WIDTH), lambda b, i, r, comp=comp: (b, i, 3 * r + comp)))
        if comp > 0:
            in_specs += _attn_halo_specs(step_rows, sub_len, A_WIDTH,
                                         lambda g, row, comp=comp: (g[0], row, 3 * g[2] + comp))
    return pl.pallas_call(
        functools.partial(_attn_dilated_body, sub_len=sub_len, step_rows=step_rows, dil=dil),
        grid=(b_, sub_len // step_rows, dil),
        in_specs=in_specs + [_const_spec(bias.shape)],
        out_specs=[pl.BlockSpec((1, tokens, A_WIDTH), lambda b, i, r: (b, i, 0)),
                   pl.BlockSpec((1, tokens, LANES), lambda b, i, r: (b, i, 0))],
        out_shape=[jax.ShapeDtypeStruct((b_, s_, A_WIDTH), bf16), jax.ShapeDtypeStruct((b_, s_, LANES), f32)],
        scratch_shapes=[pltpu.VMEM((A_WIDTH // LANES, tokens, LANES), f32), pltpu.VMEM((tokens, LANES), f32)],
        compiler_params=pltpu.CompilerParams(
            dimension_semantics=("parallel", "parallel", "arbitrary"), vmem_limit_bytes=VMEM_LIMIT_BYTES),
        name=f"attn_d{dil}",
    )(*([qkv] * 7), bias)


def _attn_mix(qkv, bias, o1, st1, o2, st2, expand):
    b_, s_, _ = qkv.shape
    step_rows = min(ATT_STEP_ROWS[1], s_)
    main = lambda width: pl.BlockSpec((1, step_rows, width), lambda b, i: (b, i, 0))
    in_specs = []
    for comp in range(3):
        in_specs.append(pl.BlockSpec((1, step_rows, A_WIDTH), lambda b, i, comp=comp: (b, i, comp)))
        if comp > 0:
            in_specs += _attn_halo_specs(step_rows, s_, A_WIDTH, lambda g, row, comp=comp: (g[0], row, comp))
    return pl.pallas_call(
        functools.partial(_attn_mix_body, sub_len=s_, step_rows=step_rows),
        grid=(b_, s_ // step_rows),
        in_specs=in_specs + [_const_spec(bias.shape),
                             main(A_WIDTH), main(LANES), main(A_WIDTH), main(LANES), _const_spec(expand.shape)],
        out_specs=main(A_WIDTH),
        out_shape=jax.ShapeDtypeStruct((b_, s_, A_WIDTH), bf16),
        compiler_params=pltpu.CompilerParams(
            dimension_semantics=("parallel", "parallel"), vmem_limit_bytes=VMEM_LIMIT_BYTES),
        name="attn_mix",
    )(*([qkv] * 7), bias, o1, st1, o2, st2, expand)


def _final_body(*refs):
    _run(_final_steps(None, None, *refs))


FINAL_N_STEPS = M_HEADS + 6


def _final_steps(i, ni, x_ref, hf_ref, hb_ref, ya_ref, prew_ref, w4_ref, hnw_ref, wpm_ref, wpa_ref, wout_ref,
                 postw_ref, y_ref):
    x = x_ref[0]
    h = _rms(x, prew_ref[...]).astype(bf16)
    hsum = hf_ref[0].astype(f32) + hb_ref[0].astype(f32)
    off_z, off_az, off_ga = M_WIDTH, 2 * M_WIDTH, 2 * M_WIDTH + A_WIDTH
    off_gb = off_ga + D_MODEL
    yield
    parts = []
    for hh in range(M_HEADS):
        hs = slice(hh * M_HEAD_DIM, (hh + 1) * M_HEAD_DIM)
        o = _dot(h, w4_ref[:, hs])
        z = _dot(h, w4_ref[:, off_z + hh * M_HEAD_DIM:off_z + (hh + 1) * M_HEAD_DIM])
        hx = hsum[:, hs]
        hn = hx * lax.rsqrt(jnp.mean(hx * hx, axis=-1, keepdims=True) + EPS) * hnw_ref[:, hs]
        parts.append((hn * _sigmoid(o) * _silu(z)).astype(bf16))
        yield
    ym = jnp.concatenate(parts, axis=1)
    az = _dot(h, w4_ref[:, off_az:off_az + A_WIDTH])
    ya = (ya_ref[0].astype(f32) * _silu(az)).astype(bf16)
    yield
    pm = _dot(ym, wpm_ref[...])
    pa = _dot(ya, wpa_ref[...])
    yield
    ga = _sigmoid(_dot(h, w4_ref[:, off_ga:off_ga + D_MODEL]))
    yield
    gb = _sigmoid(_dot(h, w4_ref[:, off_gb:off_gb + D_MODEL]))
    merged = (ga * pm + gb * pa).astype(bf16)
    yield
    out = _dot(merged, wout_ref[...])
    y_ref[0] = x + _rms(out, postw_ref[...])
    yield


def _final_parts(x, hf, hb, ya, pre_w, w4, hnw, wpm, wpa, wout, post_w, rows):
    b_, s_, _ = x.shape
    row_spec = lambda width: pl.BlockSpec((1, rows, width), lambda b, i: (b, i, 0))
    consts = [pre_w, w4, hnw, wpm, wpa, wout, post_w]
    return dict(grid=(b_, s_ // rows), out_specs=[row_spec(D_MODEL)],
                in_specs=[row_spec(D_MODEL), row_spec(M_WIDTH), row_spec(M_WIDTH), row_spec(A_WIDTH)]
                         + [_const_spec(c.shape) for c in consts],
                out_shape=[jax.ShapeDtypeStruct(x.shape, x.dtype)], scratch=[], args=(x, hf, hb, ya, *consts))


def _final(*operands):
    parts = _final_parts(*operands, FINAL_ROWS)
    return pl.pallas_call(
        _final_body,
        grid=parts["grid"],
        in_specs=parts["in_specs"],
        out_specs=parts["out_specs"],
        out_shape=parts["out_shape"],
        compiler_params=pltpu.CompilerParams(
            dimension_semantics=("parallel", "parallel"), vmem_limit_bytes=VMEM_LIMIT_BYTES),
        name="final",
    )(*parts["args"])[0]


def _t5_bucket(rel):
    nb = N_BUCKETS // 2
    exact = nb // 2
    n = np.abs(rel)
    large = exact + (np.log(np.maximum(n, 1) / exact) / math.log(MAX_DISTANCE / exact) * (nb - exact)).astype(np.int32)
    large = np.minimum(large, nb - 1)
    return (rel > 0).astype(np.int32) * nb + np.where(n < exact, n, large)


def _attn_bias(rel_table, g, dil):
    win = ATT_QBLOCK + 2 * A_HALF
    off = np.arange(win)[None, :] - A_HALF - np.arange(ATT_QBLOCK)[:, None]
    band = np.abs(off) <= A_HALF
    bucket = np.where(band, _t5_bucket(off * dil), -1)
    onehot = jnp.asarray(bucket[None] == np.arange(N_BUCKETS)[:, None, None])
    table = rel_table.astype(f32)[:, g, :]
    bias = jnp.sum(jnp.where(onehot[:, None], table[:, :, None, None], 0.0), axis=0)
    bias = jnp.where(band[None], bias * LOG2E, NEG)
    return bias.reshape(A_HEADS // 2, 2 * ATT_QBLOCK, win)


def _head_expand_matrix():
    e = np.zeros((2 * LANES, A_WIDTH), np.float32)
    for h in range(A_HEADS):
        e[h, h * A_HEAD_DIM:(h + 1) * A_HEAD_DIM] = 1.0
        e[LANES + h, h * A_HEAD_DIM:(h + 1) * A_HEAD_DIM] = 1.0
    return jnp.asarray(e, bf16)


def _proj_operands(x, p):
    return (x, p["pre_w"], p["conv_w"], p["conv_b"], p["wqk"], p["wv"], p["wift"], p["bias_r"], p["wa"])


def _attention(a, p):
    dilated = []
    for g, (_, dil) in enumerate(A_PATTERNS):
        if dil > 1:
            dilated += _attn_dilated(a[g], p["attn_bias"][g], dil)
    return _attn_mix(a[0], p["attn_bias"][0], *dilated, _head_expand_matrix())


def _final_operands(x, hf, hb, ya, p):
    return (x, hf, hb, ya, p["pre_w"], p["w4"], p["hnw"], p["wpm"], p["wpa"], p["wout"], p["post_w"])


def _layer_pair(x_first, x_second, p):
    *scan_in, a_first = _split_proj(_proj(*_proj_operands(x_first, p)))
    proj_second, (hf, hb) = _with_scan("proj_mlstm", _proj_parts(*_proj_operands(x_second, p)), _proj_steps,
                                       PROJ_N_STEPS, tuple(scan_in))
    *scan_in, a_second = _split_proj(proj_second)
    final_first = _final_parts(*_final_operands(x_first, hf, hb, _attention(a_first, p), p), FUSED_FINAL_ROWS)
    (y_first,), (hf, hb) = _with_scan("final_mlstm", final_first, _final_steps, FINAL_N_STEPS, tuple(scan_in))
    return y_first, _final(*_final_operands(x_second, hf, hb, _attention(a_second, p), p))


def _split_proj(outs):
    q, kt, v, grow, *a = outs
    return q, kt, v, grow, a


def kernel(x_prompt, x_sample, pre_norm_w, w_in, m_conv_w, m_conv_b, m_igate_b, m_fgate_b, m_head_norm_w,
           w_proj_m, w_proj_a, w_out, post_norm_w, rel_bias_table):
    depth = pre_norm_w.shape[0]
    params = []
    for l in range(depth):
        w = w_in[l]
        n_query = N_GROUPS * A_WIDTH
        wa = jnp.concatenate([w[:, OFF_AQKV:OFF_AQKV + n_query] * (A_HEAD_DIM ** -0.5 * LOG2E),
                              w[:, OFF_AQKV + n_query:OFF_AZ]], axis=1)
        gate_b = jnp.concatenate([m_igate_b[l].reshape(-1), m_fgate_b[l].reshape(-1)]).astype(f32)
        params.append(dict(
            pre_w=pre_norm_w[l].reshape(1, D_MODEL), post_w=post_norm_w[l].reshape(1, D_MODEL),
            conv_w=m_conv_w[l], conv_b=m_conv_b[l].reshape(1, 2 * M_WIDTH),
            wqk=w[:, OFF_QK:OFF_V].astype(bf16), wv=w[:, OFF_V:OFF_O].astype(bf16),
            wift=w[:, OFF_I:OFF_AQKV].T.astype(bf16), wa=wa.astype(bf16),
            bias_r=gate_b.reshape(4 * M_HEADS, 1),
            w4=jnp.concatenate([w[:, OFF_O:OFF_I], w[:, OFF_AZ:]], axis=1).astype(bf16),
            hnw=m_head_norm_w[l].reshape(1, M_WIDTH),
            wpm=w_proj_m[l].astype(bf16), wpa=w_proj_a[l].astype(bf16), wout=w_out[l].astype(bf16),
            attn_bias=[_attn_bias(rel_bias_table, g, dil) for g, (_, dil) in enumerate(A_PATTERNS)],
        ))

    for p in params:
        x_prompt, x_sample = _layer_pair(x_prompt, x_sample, p)
    return (x_prompt, x_sample)
```

```python
import functools
import math

import numpy as np
import jax
import jax.numpy as jnp
from jax import lax
from jax.experimental import pallas as pl
from jax.experimental.pallas import tpu as pltpu

D_MODEL = 1024
M_HEADS = 4
M_HEAD_DIM = 256
M_WIDTH = M_HEADS * M_HEAD_DIM
M_CHUNK = 128
A_PATTERNS = ((128, 1), (512, 4), (2048, 16))
N_GROUPS = 3
A_HEADS = 8
A_HEAD_DIM = 64
A_WIDTH = A_HEADS * A_HEAD_DIM
A_HALF = 64
N_BUCKETS = 32
MAX_DISTANCE = 1024
EPS = 1e-6
NEG = -1e30
LOG2E = math.log2(math.e)

OFF_QK = 0
OFF_V = 2 * M_WIDTH
OFF_O = OFF_V + M_WIDTH
OFF_Z = OFF_O + M_WIDTH
OFF_I = OFF_Z + M_WIDTH
OFF_F = OFF_I + 2 * M_HEADS
OFF_AQKV = OFF_F + 2 * M_HEADS
OFF_AZ = OFF_AQKV + 3 * N_GROUPS * A_WIDTH
OFF_GATE = OFF_AZ + A_WIDTH

LANES = 128
STRIDE_ONE_OP = 4
VMEM_LIMIT_BYTES = 60 * 1024 * 1024

GATE_ROWS = 32
PROJ_ROWS = 512
PROJ_HALO = 8
PROJ_COLS = 256
FINAL_ROWS = 512
FUSED_FINAL_ROWS = 256
ATT_QBLOCK = 128
ATT_STEP_ROWS = {1: 512, 4: 512, 16: 256}

f32 = jnp.float32
bf16 = jnp.bfloat16


def _const_spec(shape):
    nd = len(shape)
    return pl.BlockSpec(shape, lambda *_: (0,) * nd, pipeline_mode=pl.Buffered(1))


def _dot(a, b):
    return jnp.dot(a, b, preferred_element_type=f32)


def _dot_nt(a, b):
    return lax.dot_general(a, b, (((1,), (1,)), ((), ())), preferred_element_type=f32)


def _dot_exact(a, b):
    return jnp.dot(a, b, preferred_element_type=f32, precision=lax.Precision.HIGHEST)


def _rms(x, w):
    return x * lax.rsqrt(jnp.mean(x * x, axis=-1, keepdims=True) + EPS) * w


def _sigmoid(x):
    return 0.5 + 0.5 * jnp.tanh(0.5 * x)


def _silu(x):
    half = 0.5 * x
    return half + half * jnp.tanh(half)


def _run(steps):
    for _ in steps:
        pass


def _interleave(first, second, total_first, total_second):
    done = [0.0, 0.0]
    gens = [first, second]
    total = [total_first, total_second]
    live = [True, True]
    while live[0] or live[1]:
        pick = 0 if (live[0] and (not live[1] or done[0] * total[1] <= done[1] * total[0])) else 1
        try:
            done[pick] += next(gens[pick])
        except StopIteration:
            live[pick] = False


def _proj_body(*refs):
    _run(_proj_steps(pl.program_id(1), pl.num_programs(1), *refs))


def _proj_steps(i, ni, x_ref, xp_ref, xn_ref, prew_ref, cw_ref, cb_ref, wqk_ref, wv_ref, wift_ref, gb_ref, wa_ref,
                q_ref, kt_ref, v_ref, gr_ref, *a_refs):
    tm = x_ref.shape[1]
    keep_prev = (i > 0).astype(f32)
    keep_next = (i < ni - 1).astype(f32)
    xa = jnp.concatenate([xp_ref[0] * keep_prev, x_ref[0], xn_ref[0] * keep_next], axis=0)
    hf = _rms(xa, prew_ref[...])
    h_ext = hf.astype(bf16)
    hm = hf[PROJ_HALO:PROJ_HALO + tm].astype(bf16)
    a_refs, (slab_ref, slab2_ref, conv_ref, act_ref) = a_refs[:-4], a_refs[-4:]
    nslab = A_WIDTH // LANES

    def qk_item(c):
        cs = slice(c * PROJ_COLS, (c + 1) * PROJ_COLS)

        def epilogue(r):
            half = tm // 2
            for sl in range(PROJ_COLS // LANES):
                conv_ref[sl] = r[:, sl * LANES:(sl + 1) * LANES]
            for sl in range(PROJ_COLS // LANES):
                col = slice(c * PROJ_COLS + sl * LANES, c * PROJ_COLS + (sl + 1) * LANES)
                w = cw_ref[:, col]
                even = conv_ref[sl, pl.ds(PROJ_HALO, half, stride=2), :]
                odd = conv_ref[sl, pl.ds(PROJ_HALO + 1, half, stride=2), :]
                odd_prev = conv_ref[sl, pl.ds(PROJ_HALO - 1, half, stride=2), :]
                even_next = conv_ref[sl, pl.ds(PROJ_HALO + 2, half, stride=2), :]
                y_even = cb_ref[:, col] + odd_prev * w[0:1] + even * w[1:2] + odd * w[2:3]
                y_odd = cb_ref[:, col] + even * w[0:1] + odd * w[1:2] + even_next * w[2:3]
                act_ref[sl, pl.ds(0, half, stride=2), :] = _silu(y_even)
                act_ref[sl, pl.ds(1, half, stride=2), :] = _silu(y_odd)
                if c * PROJ_COLS < M_WIDTH:
                    q_ref[0, :, col] = act_ref[sl].astype(bf16)
                else:
                    rows = slice(col.start - M_WIDTH, col.stop - M_WIDTH)
                    kt = (act_ref[sl] * (M_HEAD_DIM ** -0.5)).T.astype(bf16)
                    for ch in range(tm // M_CHUNK):
                        kt_ref[0, ch, rows, :] = kt[:, ch * M_CHUNK:(ch + 1) * M_CHUNK]

        return (lambda: _dot(h_ext, wqk_ref[:, cs])), epilogue, PROJ_COLS

    def v_item(c):
        cs = slice(c * PROJ_COLS, (c + 1) * PROJ_COLS)

        def epilogue(res):
            v_ref[0, :, cs] = res.astype(bf16)

        return (lambda: _dot(hm, wv_ref[:, cs])), epilogue, PROJ_COLS

    def attn_item(n):
        g, comp = divmod(n, 3)
        a_ref = a_refs[g]
        dil = A_PATTERNS[g][1]
        w_cols = slice((comp * N_GROUPS + g) * A_WIDTH, (comp * N_GROUPS + g + 1) * A_WIDTH)

        def epilogue(res):
            if dil == 1:
                a_ref[0, :, comp * A_WIDTH:(comp + 1) * A_WIDTH] = res.astype(bf16)
                return
            for sl in range(nslab):
                slab_ref[sl] = res[:, sl * LANES:(sl + 1) * LANES]
            src_ref, groups = slab_ref, [(0, 0)]
            stride = dil
            if dil > STRIDE_ONE_OP:
                stride = dil // STRIDE_ONE_OP
                part = tm // STRIDE_ONE_OP
                for r0 in range(STRIDE_ONE_OP):
                    for sl in range(nslab):
                        slab2_ref[sl, r0 * part:(r0 + 1) * part, :] = (
                            slab_ref[sl, pl.ds(r0, part, stride=STRIDE_ONE_OP), :])
                src_ref = slab2_ref
                groups = [(r0 * part, r0) for r0 in range(STRIDE_ONE_OP)]
            for base, r0 in groups:
                for r1 in range(stride):
                    r = r1 * (dil // stride) + r0
                    for sl in range(nslab):
                        c0 = (r * 3 + comp) * A_WIDTH + sl * LANES
                        a_ref[0, :, c0:c0 + LANES] = (
                            src_ref[sl, pl.ds(base + r1, tm // dil, stride=stride), :].astype(bf16))

        return (lambda: _dot(hm, wa_ref[:, w_cols])), epilogue, A_WIDTH

    def gate_epilogue(gpre):
        ns = 2 * M_HEADS
        lc = M_CHUNK
        sub = lax.broadcasted_iota(jnp.int32, (2 * ns, tm), 0)
        lrow = jnp.where(sub >= ns, jax.nn.log_sigmoid(gpre + gb_ref[...]), gpre + gb_ref[...])
        tri_r = lax.broadcasted_iota(jnp.int32, (lc, lc), 0)
        tri_c = lax.broadcasted_iota(jnp.int32, (lc, lc), 1)
        upper = (tri_c >= tri_r).astype(f32)
        lower = (tri_c <= tri_r).astype(f32)
        fwd_rows = lax.broadcasted_iota(jnp.int32, (ns, lc), 0) < M_HEADS
        for c in range(tm // lc):
            ls = slice(c * lc, (c + 1) * lc)
            blk = lrow[:, ls]
            lf = blk[ns:]
            f_cum = jnp.where(fwd_rows, _dot_exact(lf, upper), _dot_exact(lf, lower))
            f_all = jnp.sum(lf, axis=-1, keepdims=True)
            gr_ref[0, c, 0:ns, :] = (blk[:ns] - f_cum) * LOG2E
            gr_ref[0, c, ns:2 * ns, :] = lf * LOG2E
            gr_ref[0, c, 2 * ns:3 * ns, :] = jnp.broadcast_to(f_all * LOG2E, (ns, lc))
            gr_ref[0, c, 3 * ns:, :] = jnp.zeros((GATE_ROWS - 3 * ns, lc), f32)

    n_qk = 2 * M_WIDTH // PROJ_COLS
    n_v = M_WIDTH // PROJ_COLS
    light = [attn_item(n) for n in range(3 * N_GROUPS)] + [v_item(c) for c in range(n_v)]
    items = [((lambda: _dot_nt(wift_ref[...], hm)), gate_epilogue, 4 * M_HEADS)]
    for c in range(n_qk):
        items.append(qk_item(c))
        items.append(light.pop())
    items += light

    yield 0
    pending = None
    for matmul, epilogue, cols in items:
        val = matmul()
        if pending is not None:
            pending()
        pending = functools.partial(epilogue, val)
        yield cols
    pending()
    yield 0


PROJ_WEIGHT = 4 * M_HEADS + 3 * M_WIDTH + 3 * N_GROUPS * A_WIDTH


def _proj_parts(x, pre_w, conv_w, conv_b, wqk, wv, wift, gate_b, wa):
    b_, s_, _ = x.shape
    tm = PROJ_ROWS
    hb = tm // PROJ_HALO
    n_halo_blocks = s_ // PROJ_HALO
    row_spec = lambda width: pl.BlockSpec((1, tm, width), lambda b, i: (b, i, 0))
    in_specs = [
        row_spec(D_MODEL),
        pl.BlockSpec((1, PROJ_HALO, D_MODEL), lambda b, i: (b, jnp.maximum(i * hb - 1, 0), 0)),
        pl.BlockSpec((1, PROJ_HALO, D_MODEL), lambda b, i: (b, jnp.minimum((i + 1) * hb, n_halo_blocks - 1), 0)),
        _const_spec(pre_w.shape), _const_spec(conv_w.shape), _const_spec(conv_b.shape),
        _const_spec(wqk.shape), _const_spec(wv.shape), _const_spec(wift.shape), _const_spec(gate_b.shape),
        _const_spec(wa.shape),
    ]
    act = lambda width: jax.ShapeDtypeStruct((b_, s_, width), bf16)
    chunk_spec = lambda height: pl.BlockSpec((1, tm // M_CHUNK, height, M_CHUNK), lambda b, i: (b, i, 0, 0))
    out_shape = [act(M_WIDTH), jax.ShapeDtypeStruct((b_, s_ // M_CHUNK, M_WIDTH, M_CHUNK), bf16), act(M_WIDTH),
                 jax.ShapeDtypeStruct((b_, s_ // M_CHUNK, GATE_ROWS, M_CHUNK), f32)]
    out_specs = [row_spec(M_WIDTH), chunk_spec(M_WIDTH), row_spec(M_WIDTH), chunk_spec(GATE_ROWS)]
    for _, dil in A_PATTERNS:
        out_shape.append(jax.ShapeDtypeStruct((b_, s_ // dil, dil * 3 * A_WIDTH), bf16))
        out_specs.append(pl.BlockSpec((1, tm // dil, dil * 3 * A_WIDTH), lambda b, i: (b, i, 0)))
    scratch = ([pltpu.VMEM((A_WIDTH // LANES, tm, LANES), f32)] * 2
               + [pltpu.VMEM((PROJ_COLS // LANES, tm + 2 * PROJ_HALO, LANES), f32),
                  pltpu.VMEM((PROJ_COLS // LANES, tm, LANES), f32)])
    return dict(grid=(b_, s_ // tm), in_specs=in_specs, out_specs=out_specs, out_shape=out_shape, scratch=scratch,
                args=(x, x, x, pre_w, conv_w, conv_b, wqk, wv, wift, gate_b, wa))


def _proj(*operands):
    parts = _proj_parts(*operands)
    return pl.pallas_call(
        _proj_body,
        grid=parts["grid"],
        in_specs=parts["in_specs"],
        out_specs=parts["out_specs"],
        out_shape=parts["out_shape"],
        scratch_shapes=parts["scratch"],
        compiler_params=pltpu.CompilerParams(
            dimension_semantics=("parallel", "parallel"), vmem_limit_bytes=VMEM_LIMIT_BYTES),
        name="proj",
    )(*parts["args"])


def _mlstm_reset(j, c_ref, n_ref, m_ref):
    @pl.when(j == 0)
    def _():
        c_ref[...] = jnp.zeros_like(c_ref)
        n_ref[...] = jnp.zeros_like(n_ref)
        m_ref[...] = jnp.zeros_like(m_ref)


MLSTM_N_STEPS = 3 * 2 * M_HEADS


def _mlstm_chunk_steps(step, step_chunks, qf_ref, ktf_ref, vf_ref, grf_ref, qb_ref, ktb_ref, vb_ref, grb_ref,
                       hf_ref, hb_ref, c_ref, n_ref, m_ref):
    lc = M_CHUNK
    nh = M_HEADS
    ns = 2 * M_HEADS
    e = M_HEAD_DIM
    row = lax.broadcasted_iota(jnp.int32, (lc, lc), 0)
    col = lax.broadcasted_iota(jnp.int32, (lc, lc), 1)

    def head_cols(hh):
        return slice(hh * e, (hh + 1) * e)

    streams = []
    for d in range(2):
        q_ref, kt_ref, v_ref, gr_ref, out_ref = (
            (qf_ref, ktf_ref, vf_ref, grf_ref, hf_ref) if d == 0 else (qb_ref, ktb_ref, vb_ref, grb_ref, hb_ref))
        mask = (col <= row) if d == 0 else (col >= row)
        chunk = step if d == 0 else step_chunks - 1 - step
        tok = slice(chunk * lc, (chunk + 1) * lc)
        for hh in range(nh):
            streams.append((d * nh + hh, hh, q_ref, kt_ref, v_ref, gr_ref, out_ref, mask, chunk, tok))

    gated = []
    for ci, hh, q_ref, kt_ref, v_ref, gr_ref, out_ref, mask, chunk, tok in streams:
        r_row = gr_ref[0, chunk, ci:ci + 1, :]
        lf_row = gr_ref[0, chunk, ns + ci:ns + ci + 1, :]
        m_sc = m_ref[ci][:, 0:1]
        s_aug = _dot(q_ref[0, tok, head_cols(hh)],
                     jnp.concatenate([kt_ref[0, chunk, head_cols(hh), :], n_ref[ci].astype(bf16)], axis=1))
        r_mat = jnp.where(mask, r_row, NEG)
        u = jnp.maximum(m_sc, jnp.max(r_mat, axis=-1, keepdims=True))
        f_col = jnp.sum(jnp.where(mask, lf_row, 0.0), axis=-1, keepdims=True)
        p = s_aug[:, :lc] * jnp.exp2(r_mat - u)
        wa = jnp.exp2(m_sc - u)
        den = jnp.sum(p, axis=-1, keepdims=True) + wa * s_aug[:, lc:]
        inv = 1.0 / jnp.maximum(jnp.abs(den), jnp.exp2(-(f_col + u)))
        gated.append((p.astype(bf16), wa, inv))
        yield 1

    for (ci, hh, q_ref, kt_ref, v_ref, gr_ref, out_ref, mask, chunk, tok), (p, wa, inv) in zip(streams, gated):
        acc = (_dot(p, v_ref[0, tok, head_cols(hh)])
               + wa * _dot(q_ref[0, tok, head_cols(hh)], c_ref[ci].astype(bf16)))
        for part in range(e // LANES):
            ps = slice(part * LANES, (part + 1) * LANES)
            out_ref[0, tok, hh * e + part * LANES:hh * e + (part + 1) * LANES] = (acc[:, ps] * inv).astype(bf16)
        yield 1

    for ci, hh, q_ref, kt_ref, v_ref, gr_ref, out_ref, mask, chunk, tok in streams:
        r_row = gr_ref[0, chunk, ci:ci + 1, :]
        f_all = gr_ref[0, chunk, 2 * ns + ci:2 * ns + ci + 1, :]
        m_old = m_ref[ci]
        g = f_all + r_row
        m_new = jnp.maximum(f_all + m_old, jnp.max(g, axis=-1, keepdims=True))
        kw = kt_ref[0, chunk, head_cols(hh), :].astype(f32) * jnp.exp2(g - m_new)
        decay = jnp.exp2(f_all + m_old - m_new)[:, 0:1]
        c_ref[ci] = decay * c_ref[ci] + _dot(kw.astype(bf16), v_ref[0, tok, head_cols(hh)])
        n_ref[ci] = decay * n_ref[ci] + jnp.sum(kw, axis=-1, keepdims=True)
        m_ref[ci] = m_new
        yield 1


def _mlstm_parts(q, kt, v, grow, step_chunks):
    b_, s_, _ = q.shape
    sc = step_chunks
    lc = M_CHUNK * sc
    nc = s_ // lc
    fwd = pl.BlockSpec((1, lc, M_WIDTH), lambda b, j: (b, j, 0))
    bwd = pl.BlockSpec((1, lc, M_WIDTH), lambda b, j: (b, nc - 1 - j, 0))
    fwd_t = lambda height: pl.BlockSpec((1, sc, height, M_CHUNK), lambda b, j: (b, j, 0, 0))
    bwd_t = lambda height: pl.BlockSpec((1, sc, height, M_CHUNK), lambda b, j: (b, nc - 1 - j, 0, 0))
    out = jax.ShapeDtypeStruct((b_, s_, M_WIDTH), bf16)
    scratch = [pltpu.VMEM((2 * M_HEADS, M_HEAD_DIM, M_HEAD_DIM), f32),
               pltpu.VMEM((2 * M_HEADS, M_HEAD_DIM, LANES), f32),
               pltpu.VMEM((2 * M_HEADS, 1, LANES), f32)]
    return dict(grid=(b_, nc), out_specs=[fwd, bwd], out_shape=[out, out], scratch=scratch,
                in_specs=[fwd, fwd_t(M_WIDTH), fwd, fwd_t(GATE_ROWS), bwd, bwd_t(M_WIDTH), bwd, bwd_t(GATE_ROWS)],
                args=(q, kt, v, grow, q, kt, v, grow))


def _respec(specs, adapt):
    return [pl.BlockSpec(s.block_shape, (lambda step, f=s.index_map: f(*adapt(step))), pipeline_mode=s.pipeline_mode)
            for s in specs]


def _with_scan_body(*refs, tile_steps, tile_weight, counts, tiles, scan_steps, step_chunks):
    step = pl.program_id(0)
    n_ti, n_to, n_ts = counts
    bounds = np.cumsum([0, n_ti, 8, n_to, 2, n_ts])
    t_in, m_in, t_out, m_out, t_scr = [refs[bounds[k]:bounds[k + 1]] for k in range(5)]
    m_scr = refs[bounds[5]:]
    scan_refs = (*m_in, *m_out, *m_scr)
    _mlstm_reset(step % scan_steps, *m_scr)

    def scan():
        for st in range(step_chunks):
            yield from _mlstm_chunk_steps(st, step_chunks, *scan_refs)

    _interleave(tile_steps(step % tiles, tiles, *t_in, *t_out, *t_scr), scan(),
                tile_weight, step_chunks * MLSTM_N_STEPS)


def _with_scan(name, parts, tile_steps, tile_weight, scan_operands):
    q = scan_operands[0]
    tiles = parts["grid"][1]
    n_steps = parts["grid"][0] * tiles
    seq_chunks = q.shape[1] // M_CHUNK
    step_chunks = q.shape[0] * seq_chunks // n_steps
    assert step_chunks * n_steps == q.shape[0] * seq_chunks and seq_chunks % step_chunks == 0
    mp = _mlstm_parts(*scan_operands, step_chunks)
    scan_steps = mp["grid"][1]
    t_adapt = lambda step: (step // tiles, step % tiles)
    m_adapt = lambda step: (step // scan_steps, step % scan_steps)
    n_out = len(parts["out_shape"])
    outs = pl.pallas_call(
        functools.partial(_with_scan_body, tile_steps=tile_steps, tile_weight=tile_weight,
                          counts=(len(parts["in_specs"]), n_out, len(parts["scratch"])),
                          tiles=tiles, scan_steps=scan_steps, step_chunks=step_chunks),
        grid=(n_steps,),
        in_specs=_respec(parts["in_specs"], t_adapt) + _respec(mp["in_specs"], m_adapt),
        out_specs=_respec(parts["out_specs"], t_adapt) + _respec(mp["out_specs"], m_adapt),
        out_shape=parts["out_shape"] + mp["out_shape"],
        scratch_shapes=parts["scratch"] + mp["scratch"],
        compiler_params=pltpu.CompilerParams(
            dimension_semantics=("arbitrary",), vmem_limit_bytes=VMEM_LIMIT_BYTES),
        name=name,
    )(*parts["args"], *mp["args"])
    return outs[:n_out], outs[n_out:]


def _attn_blocks(q_ref, kwin, vwin, bias_ref, first_keys, sub_len):
    qb = ATT_QBLOCK
    win = qb + 2 * A_HALF
    lane_q = lax.broadcasted_iota(jnp.int32, (qb, LANES), 1)
    lo_q = lane_q < A_HEAD_DIM
    zq = jnp.zeros((qb, LANES), bf16)
    npair = A_HEADS // 2
    last = len(first_keys) - 1
    units = [(b, p, b * qb, slice(p * LANES, (p + 1) * LANES)) for b in range(len(first_keys)) for p in range(npair)]

    raws = []
    for b, p, r0, ps in units:
        qp = q_ref[0, r0:r0 + qb, ps]
        q2 = jnp.concatenate([jnp.where(lo_q, qp, zq), jnp.where(lo_q, zq, qp)], axis=0)
        raws.append(_dot_nt(q2, kwin[r0:r0 + win, ps]))

    edges = {}
    for b in {0, last}:
        kpos = first_keys[b] + lax.broadcasted_iota(jnp.int32, (1, win), 1)
        edges[b] = jnp.where((kpos >= 0) & (kpos < sub_len), 0.0, NEG).astype(f32)

    probs = []
    stats = [jnp.zeros((qb, LANES), f32) for _ in first_keys]
    for (b, p, r0, ps), raw in zip(units, raws):
        s = raw + bias_ref[p]
        if b in edges:
            s = s + edges[b]
        mx = jnp.max(s, axis=-1, keepdims=True)
        pe = jnp.exp2(s - mx)
        den = jnp.sum(pe, axis=-1, keepdims=True)
        probs.append(pe.astype(bf16))
        st = stats[b]
        st = jnp.where(lane_q == 2 * p, mx[:qb], st)
        st = jnp.where(lane_q == 2 * p + 1, mx[qb:], st)
        st = jnp.where(lane_q == A_HEADS + 2 * p, den[:qb], st)
        stats[b] = jnp.where(lane_q == A_HEADS + 2 * p + 1, den[qb:], st)

    outs = [[] for _ in first_keys]
    for (b, p, r0, ps), pb in zip(units, probs):
        vp = vwin[r0:r0 + win, ps]
        outs[b].append(jnp.where(lo_q, _dot(pb[:qb], vp), _dot(pb[qb:], vp)))
    return list(zip(outs, stats))


def _attn_dilated_body(q_ref, k_ref, kp_ref, kn_ref, v_ref, vp_ref, vn_ref, bias_ref, o_ref, st_ref,
                       o_scr, st_scr, *, sub_len, step_rows, dil):
    i = pl.program_id(1)
    r = pl.program_id(2)
    qb = ATT_QBLOCK
    kwin = jnp.concatenate([kp_ref[0], k_ref[0], kn_ref[0]], axis=0)
    vwin = jnp.concatenate([vp_ref[0], v_ref[0], vn_ref[0]], axis=0)
    first_keys = [i * step_rows + blk * qb - A_HALF for blk in range(step_rows // qb)]
    for blk, (outs, stats) in enumerate(_attn_blocks(q_ref, kwin, vwin, bias_ref, first_keys, sub_len)):
        r0 = blk * qb
        rows = pl.ds(r0 * dil + r, qb, stride=dil)
        for p, o in enumerate(outs):
            o_scr[p, rows, :] = o
        st_scr[rows, :] = stats

    @pl.when(r == dil - 1)
    def _():
        for p in range(A_HEADS // 2):
            o_ref[0, :, p * LANES:(p + 1) * LANES] = o_scr[p].astype(bf16)
        st_ref[0] = st_scr[...]


def _attn_mix_body(q_ref, k_ref, kp_ref, kn_ref, v_ref, vp_ref, vn_ref, bias_ref,
                   o1_ref, st1_ref, o2_ref, st2_ref, expand_ref, y_ref, *, sub_len, step_rows):
    i = pl.program_id(1)
    qb = ATT_QBLOCK
    kwin = jnp.concatenate([kp_ref[0], k_ref[0], kn_ref[0]], axis=0)
    vwin = jnp.concatenate([vp_ref[0], v_ref[0], vn_ref[0]], axis=0)
    head_lane = lax.broadcasted_iota(jnp.int32, (qb, LANES), 1) < A_HEADS
    first_keys = [i * step_rows + blk * qb - A_HALF for blk in range(step_rows // qb)]
    for blk, (outs, st0) in enumerate(_attn_blocks(q_ref, kwin, vwin, bias_ref, first_keys, sub_len)):
        r0 = blk * qb
        st1 = st1_ref[0, r0:r0 + qb, :]
        st2 = st2_ref[0, r0:r0 + qb, :]
        dn0, dn1, dn2 = [pltpu.roll(st, LANES - A_HEADS, axis=1) for st in (st0, st1, st2)]
        top = jnp.maximum(jnp.maximum(st0, st1), st2)
        w0, w1, w2 = jnp.exp2(st0 - top), jnp.exp2(st1 - top), jnp.exp2(st2 - top)
        dsum = w0 * dn0 + w1 * dn1 + w2 * dn2
        o_groups = (jnp.concatenate(outs, axis=1),
                    o1_ref[0, r0:r0 + qb, :].astype(f32), o2_ref[0, r0:r0 + qb, :].astype(f32))
        coef = []
        for w in (w0, w1, w2):
            c = jnp.where(head_lane, w / dsum, 0.0)
            c_hi = c.astype(bf16)
            coef.append(jnp.concatenate([c_hi, (c - c_hi.astype(f32)).astype(bf16)], axis=1))
        wide = _dot(jnp.concatenate(coef, axis=0), expand_ref[...])
        y = (wide[:qb] * o_groups[0] + wide[qb:2 * qb] * o_groups[1] + wide[2 * qb:] * o_groups[2])
        y_ref[0, r0:r0 + qb, :] = y.astype(bf16)


def _attn_halo_specs(step_rows, sub_len, width, index):
    hpb = step_rows // A_HALF
    last = sub_len // A_HALF - 1
    before = pl.BlockSpec((1, A_HALF, width), lambda *g: index(g, jnp.maximum(g[1] * hpb - 1, 0)))
    after = pl.BlockSpec((1, A_HALF, width), lambda *g: index(g, jnp.minimum((g[1] + 1) * hpb, last)))
    return before, after


def _attn_dilated(qkv, bias, dil):
    b_, sub_len, _ = qkv.shape
    s_ = sub_len * dil
    step_rows = min(ATT_STEP_ROWS[dil], sub_len)
    tokens = step_rows * dil
    in_specs = []
    for comp in range(3):
        in_specs.append(pl.BlockSpec((1, step_rows, A_WIDTH), lambda b, i, r, comp=comp: (b, i, 3 * r + comp)))
        if comp > 0:
            in_specs += _attn_halo_specs(step_rows, sub_len, A_WIDTH,
                                         lambda g, row, comp=comp: (g[0], row, 3 * g[2] + comp))
    return pl.pallas_call(
        functools.partial(_attn_dilated_body, sub_len=sub_len, step_rows=step_rows, dil=dil),
        grid=(b_, sub_len // step_rows, dil),
        in_specs=in_specs + [_const_spec(bias.shape)],
        out_specs=[pl.BlockSpec((1, tokens, A_WIDTH), lambda b, i, r: (b, i, 0)),
                   pl.BlockSpec((1, tokens, LANES), lambda b, i, r: (b, i, 0))],
        out_shape=[jax.ShapeDtypeStruct((b_, s_, A_WIDTH), bf16), jax.ShapeDtypeStruct((b_, s_, LANES), f32)],
        scratch_shapes=[pltpu.VMEM((A_WIDTH // LANES, tokens, LANES), f32), pltpu.VMEM((tokens, LANES), f32)],
        compiler_params=pltpu.CompilerParams(
            dimension_semantics=("parallel", "parallel", "arbitrary"), vmem_limit_bytes=VMEM_LIMIT_BYTES),
        name=f"attn_d{dil}",
    )(*([qkv] * 7), bias)


def _attn_mix(qkv, bias, o1, st1, o2, st2, expand):
    b_, s_, _ = qkv.shape
    step_rows = min(ATT_STEP_ROWS[1], s_)
    main = lambda width: pl.BlockSpec((1, step_rows, width), lambda b, i: (b, i, 0))
    in_specs = []
    for comp in range(3):
        in_specs.append(pl.BlockSpec((1, step_rows, A_WIDTH), lambda b, i, comp=comp: (b, i, comp)))
        if comp > 0:
            in_specs += _attn_halo_specs(step_rows, s_, A_WIDTH, lambda g, row, comp=comp: (g[0], row, comp))
    return pl.pallas_call(
        functools.partial(_attn_mix_body, sub_len=s_, step_rows=step_rows),
        grid=(b_, s_ // step_rows),
        in_specs=in_specs + [_const_spec(bias.shape),
                             main(A_WIDTH), main(LANES), main(A_WIDTH), main(LANES), _const_spec(expand.shape)],
        out_specs=main(A_WIDTH),
        out_shape=jax.ShapeDtypeStruct((b_, s_, A_WIDTH), bf16),
        compiler_params=pltpu.CompilerParams(
            dimension_semantics=("parallel", "parallel"), vmem_limit_bytes=VMEM_LIMIT_BYTES),
        name="attn_mix",
    )(*([qkv] * 7), bias, o1, st1, o2, st2, expand)


def _final_body(*refs):
    _run(_final_steps(None, None, *refs))


FINAL_WEIGHT = 2 * M_WIDTH + A_WIDTH + 4 * D_MODEL + D_MODEL * A_WIDTH // M_WIDTH


def _final_steps(i, ni, x_ref, hf_ref, hb_ref, ya_ref, prew_ref, w4_ref, hnw_ref, wpm_ref, wpa_ref, wout_ref,
                 postw_ref, y_ref):
    x = x_ref[0]
    h = _rms(x, prew_ref[...]).astype(bf16)
    hsum = hf_ref[0].astype(f32) + hb_ref[0].astype(f32)
    off_z, off_az, off_ga = M_WIDTH, 2 * M_WIDTH, 2 * M_WIDTH + A_WIDTH
    off_gb = off_ga + D_MODEL
    yield 0
    parts = []
    for hh in range(M_HEADS):
        hs = slice(hh * M_HEAD_DIM, (hh + 1) * M_HEAD_DIM)
        o = _dot(h, w4_ref[:, hs])
        z = _dot(h, w4_ref[:, off_z + hh * M_HEAD_DIM:off_z + (hh + 1) * M_HEAD_DIM])
        hx = hsum[:, hs]
        hn = hx * lax.rsqrt(jnp.mean(hx * hx, axis=-1, keepdims=True) + EPS) * hnw_ref[:, hs]
        parts.append((hn * _sigmoid(o) * _silu(z)).astype(bf16))
        yield 2 * M_HEAD_DIM
    ym = jnp.concatenate(parts, axis=1)
    az = _dot(h, w4_ref[:, off_az:off_az + A_WIDTH])
    ya = (ya_ref[0].astype(f32) * _silu(az)).astype(bf16)
    yield A_WIDTH
    pm = _dot(ym, wpm_ref[...])
    yield D_MODEL
    pa = _dot(ya, wpa_ref[...])
    yield D_MODEL * A_WIDTH // M_WIDTH
    ga = _sigmoid(_dot(h, w4_ref[:, off_ga:off_ga + D_MODEL]))
    yield D_MODEL
    gb = _sigmoid(_dot(h, w4_ref[:, off_gb:off_gb + D_MODEL]))
    merged = (ga * pm + gb * pa).astype(bf16)
    yield D_MODEL
    out = _dot(merged, wout_ref[...])
    y_ref[0] = x + _rms(out, postw_ref[...])
    yield D_MODEL


def _final_parts(x, hf, hb, ya, pre_w, w4, hnw, wpm, wpa, wout, post_w, rows):
    b_, s_, _ = x.shape
    row_spec = lambda width: pl.BlockSpec((1, rows, width), lambda b, i: (b, i, 0))
    consts = [pre_w, w4, hnw, wpm, wpa, wout, post_w]
    return dict(grid=(b_, s_ // rows), out_specs=[row_spec(D_MODEL)],
                in_specs=[row_spec(D_MODEL), row_spec(M_WIDTH), row_spec(M_WIDTH), row_spec(A_WIDTH)]
                         + [_const_spec(c.shape) for c in consts],
                out_shape=[jax.ShapeDtypeStruct(x.shape, x.dtype)], scratch=[], args=(x, hf, hb, ya, *consts))


def _final(*operands):
    parts = _final_parts(*operands, FINAL_ROWS)
    return pl.pallas_call(
        _final_body,
        grid=parts["grid"],
        in_specs=parts["in_specs"],
        out_specs=parts["out_specs"],
        out_shape=parts["out_shape"],
        compiler_params=pltpu.CompilerParams(
            dimension_semantics=("parallel", "parallel"), vmem_limit_bytes=VMEM_LIMIT_BYTES),
        name="final",
    )(*parts["args"])[0]


def _t5_bucket(rel):
    nb = N_BUCKETS // 2
    exact = nb // 2
    n = np.abs(rel)
    large = exact + (np.log(np.maximum(n, 1) / exact) / math.log(MAX_DISTANCE / exact) * (nb - exact)).astype(np.int32)
    large = np.minimum(large, nb - 1)
    return (rel > 0).astype(np.int32) * nb + np.where(n < exact, n, large)


def _attn_bias(rel_table, g, dil):
    win = ATT_QBLOCK + 2 * A_HALF
    off = np.arange(win)[None, :] - A_HALF - np.arange(ATT_QBLOCK)[:, None]
    band = np.abs(off) <= A_HALF
    bucket = np.where(band, _t5_bucket(off * dil), -1)
    onehot = jnp.asarray(bucket[None] == np.arange(N_BUCKETS)[:, None, None])
    table = rel_table.astype(f32)[:, g, :]
    bias = jnp.sum(jnp.where(onehot[:, None], table[:, :, None, None], 0.0), axis=0)
    bias = jnp.where(band[None], bias * LOG2E, NEG)
    return bias.reshape(A_HEADS // 2, 2 * ATT_QBLOCK, win)


def _head_expand_matrix():
    e = np.zeros((2 * LANES, A_WIDTH), np.float32)
    for h in range(A_HEADS):
        e[h, h * A_HEAD_DIM:(h + 1) * A_HEAD_DIM] = 1.0
        e[LANES + h, h * A_HEAD_DIM:(h + 1) * A_HEAD_DIM] = 1.0
    return jnp.asarray(e, bf16)


def _proj_operands(x, p):
    return (x, p["pre_w"], p["conv_w"], p["conv_b"], p["wqk"], p["wv"], p["wift"], p["bias_r"], p["wa"])


def _attention(a, p):
    dilated = []
    for g, (_, dil) in enumerate(A_PATTERNS):
        if dil > 1:
            dilated += _attn_dilated(a[g], p["attn_bias"][g], dil)
    return _attn_mix(a[0], p["attn_bias"][0], *dilated, _head_expand_matrix())


def _final_operands(x, hf, hb, ya, p):
    return (x, hf, hb, ya, p["pre_w"], p["w4"], p["hnw"], p["wpm"], p["wpa"], p["wout"], p["post_w"])


def _layer_pair(x_first, x_second, p):
    *scan_in, a_first = _split_proj(_proj(*_proj_operands(x_first, p)))
    proj_second, (hf, hb) = _with_scan("proj_mlstm", _proj_parts(*_proj_operands(x_second, p)), _proj_steps,
                                       PROJ_WEIGHT, tuple(scan_in))
    *scan_in, a_second = _split_proj(proj_second)
    final_first = _final_parts(*_final_operands(x_first, hf, hb, _attention(a_first, p), p), FUSED_FINAL_ROWS)
    (y_first,), (hf, hb) = _with_scan("final_mlstm", final_first, _final_steps, FINAL_WEIGHT, tuple(scan_in))
    return y_first, _final(*_final_operands(x_second, hf, hb, _attention(a_second, p), p))


def _split_proj(outs):
    q, kt, v, grow, *a = outs
    return q, kt, v, grow, a


def kernel(x_prompt, x_sample, pre_norm_w, w_in, m_conv_w, m_conv_b, m_igate_b, m_fgate_b, m_head_norm_w,
           w_proj_m, w_proj_a, w_out, post_norm_w, rel_bias_table):
    depth = pre_norm_w.shape[0]
    params = []
    for l in range(depth):
        w = w_in[l]
        n_query = N_GROUPS * A_WIDTH
        wa = jnp.concatenate([w[:, OFF_AQKV:OFF_AQKV + n_query] * (A_HEAD_DIM ** -0.5 * LOG2E),
                              w[:, OFF_AQKV + n_query:OFF_AZ]], axis=1)
        gate_b = jnp.concatenate([m_igate_b[l].reshape(-1), m_fgate_b[l].reshape(-1)]).astype(f32)
        params.append(dict(
            pre_w=pre_norm_w[l].reshape(1, D_MODEL), post_w=post_norm_w[l].reshape(1, D_MODEL),
            conv_w=m_conv_w[l], conv_b=m_conv_b[l].reshape(1, 2 * M_WIDTH),
            wqk=w[:, OFF_QK:OFF_V].astype(bf16), wv=w[:, OFF_V:OFF_O].astype(bf16),
            wift=w[:, OFF_I:OFF_AQKV].T.astype(bf16), wa=wa.astype(bf16),
            bias_r=gate_b.reshape(4 * M_HEADS, 1),
            w4=jnp.concatenate([w[:, OFF_O:OFF_I], w[:, OFF_AZ:]], axis=1).astype(bf16),
            hnw=m_head_norm_w[l].reshape(1, M_WIDTH),
            wpm=w_proj_m[l].astype(bf16), wpa=w_proj_a[l].astype(bf16), wout=w_out[l].astype(bf16),
            attn_bias=[_attn_bias(rel_bias_table, g, dil) for g, (_, dil) in enumerate(A_PATTERNS)],
        ))

    for p in params:
        x_prompt, x_sample = _layer_pair(x_prompt, x_sample, p)
    return (x_prompt, x_sample)
```

```python
import functools
import math

import numpy as np
import jax
import jax.numpy as jnp
from jax import lax
from jax.experimental import pallas as pl
from jax.experimental.pallas import tpu as pltpu

D_MODEL = 1024
M_HEADS = 4
M_HEAD_DIM = 256
M_WIDTH = M_HEADS * M_HEAD_DIM
M_CHUNK = 128
A_PATTERNS = ((128, 1), (512, 4), (2048, 16))
N_GROUPS = 3
A_HEADS = 8
A_HEAD_DIM = 64
A_WIDTH = A_HEADS * A_HEAD_DIM
A_HALF = 64
N_BUCKETS = 32
MAX_DISTANCE = 1024
EPS = 1e-6
NEG = -1e30
LOG2E = math.log2(math.e)

OFF_QK = 0
OFF_V = 2 * M_WIDTH
OFF_O = OFF_V + M_WIDTH
OFF_Z = OFF_O + M_WIDTH
OFF_I = OFF_Z + M_WIDTH
OFF_F = OFF_I + 2 * M_HEADS
OFF_AQKV = OFF_F + 2 * M_HEADS
OFF_AZ = OFF_AQKV + 3 * N_GROUPS * A_WIDTH
OFF_GATE = OFF_AZ + A_WIDTH

LANES = 128
STRIDE_ONE_OP = 4
VMEM_LIMIT_BYTES = 60 * 1024 * 1024

GATE_ROWS = 32
PROJ_ROWS = 512
PROJ_HALO = 8
PROJ_COLS = 256
FINAL_ROWS = 512
FUSED_FINAL_ROWS = 256
ATT_QBLOCK = 128
ATT_STEP_ROWS = {1: 1024, 4: 1024, 16: 512}

f32 = jnp.float32
bf16 = jnp.bfloat16


def _const_spec(shape):
    nd = len(shape)
    return pl.BlockSpec(shape, lambda *_: (0,) * nd, pipeline_mode=pl.Buffered(1))


def _dot(a, b):
    return jnp.dot(a, b, preferred_element_type=f32)


def _dot_nt(a, b):
    return lax.dot_general(a, b, (((1,), (1,)), ((), ())), preferred_element_type=f32)


def _dot_exact(a, b):
    return jnp.dot(a, b, preferred_element_type=f32, precision=lax.Precision.HIGHEST)


def _rms(x, w):
    return x * lax.rsqrt(jnp.mean(x * x, axis=-1, keepdims=True) + EPS) * w


def _sigmoid(x):
    return 0.5 + 0.5 * jnp.tanh(0.5 * x)


def _silu(x):
    half = 0.5 * x
    return half + half * jnp.tanh(half)


def _run(steps):
    for _ in steps:
        pass


def _interleave(first, second, total_first, total_second):
    done = [0.0, 0.0]
    gens = [first, second]
    total = [total_first, total_second]
    live = [True, True]
    while live[0] or live[1]:
        pick = 0 if (live[0] and (not live[1] or done[0] * total[1] <= done[1] * total[0])) else 1
        try:
            done[pick] += next(gens[pick])
        except StopIteration:
            live[pick] = False


def _proj_body(*refs):
    _run(_proj_steps(pl.program_id(1), pl.num_programs(1), *refs))


def _proj_steps(i, ni, x_ref, xp_ref, xn_ref, prew_ref, cw_ref, cb_ref, wqk_ref, wv_ref, wift_ref, gb_ref, wa_ref,
                q_ref, kt_ref, v_ref, gr_ref, *a_refs):
    tm = x_ref.shape[1]
    keep_prev = (i > 0).astype(f32)
    keep_next = (i < ni - 1).astype(f32)
    xa = jnp.concatenate([xp_ref[0] * keep_prev, x_ref[0], xn_ref[0] * keep_next], axis=0)
    hf = _rms(xa, prew_ref[...])
    h_ext = hf.astype(bf16)
    hm = hf[PROJ_HALO:PROJ_HALO + tm].astype(bf16)
    a_refs, (slab_ref, slab2_ref, conv_ref, act_ref) = a_refs[:-4], a_refs[-4:]
    nslab = A_WIDTH // LANES

    def qk_item(c):
        cs = slice(c * PROJ_COLS, (c + 1) * PROJ_COLS)

        def epilogue(r):
            half = tm // 2
            for sl in range(PROJ_COLS // LANES):
                conv_ref[sl] = r[:, sl * LANES:(sl + 1) * LANES]
            for sl in range(PROJ_COLS // LANES):
                col = slice(c * PROJ_COLS + sl * LANES, c * PROJ_COLS + (sl + 1) * LANES)
                w = cw_ref[:, col]
                even = conv_ref[sl, pl.ds(PROJ_HALO, half, stride=2), :]
                odd = conv_ref[sl, pl.ds(PROJ_HALO + 1, half, stride=2), :]
                odd_prev = conv_ref[sl, pl.ds(PROJ_HALO - 1, half, stride=2), :]
                even_next = conv_ref[sl, pl.ds(PROJ_HALO + 2, half, stride=2), :]
                y_even = cb_ref[:, col] + odd_prev * w[0:1] + even * w[1:2] + odd * w[2:3]
                y_odd = cb_ref[:, col] + even * w[0:1] + odd * w[1:2] + even_next * w[2:3]
                act_ref[sl, pl.ds(0, half, stride=2), :] = _silu(y_even)
                act_ref[sl, pl.ds(1, half, stride=2), :] = _silu(y_odd)
                if c * PROJ_COLS < M_WIDTH:
                    q_ref[0, :, col] = act_ref[sl].astype(bf16)
                else:
                    rows = slice(col.start - M_WIDTH, col.stop - M_WIDTH)
                    kt = (act_ref[sl] * (M_HEAD_DIM ** -0.5)).T.astype(bf16)
                    for ch in range(tm // M_CHUNK):
                        kt_ref[0, ch, rows, :] = kt[:, ch * M_CHUNK:(ch + 1) * M_CHUNK]

        return (lambda: _dot(h_ext, wqk_ref[:, cs])), epilogue, PROJ_COLS

    def v_item(c):
        cs = slice(c * PROJ_COLS, (c + 1) * PROJ_COLS)

        def epilogue(res):
            v_ref[0, :, cs] = res.astype(bf16)

        return (lambda: _dot(hm, wv_ref[:, cs])), epilogue, PROJ_COLS

    def attn_item(n):
        g, comp = divmod(n, 3)
        a_ref = a_refs[g]
        dil = A_PATTERNS[g][1]
        w_cols = slice((comp * N_GROUPS + g) * A_WIDTH, (comp * N_GROUPS + g + 1) * A_WIDTH)

        def epilogue(res):
            if dil == 1:
                a_ref[0, :, comp * A_WIDTH:(comp + 1) * A_WIDTH] = res.astype(bf16)
                return
            for sl in range(nslab):
                slab_ref[sl] = res[:, sl * LANES:(sl + 1) * LANES]
            src_ref, groups = slab_ref, [(0, 0)]
            stride = dil
            if dil > STRIDE_ONE_OP:
                stride = dil // STRIDE_ONE_OP
                part = tm // STRIDE_ONE_OP
                for r0 in range(STRIDE_ONE_OP):
                    for sl in range(nslab):
                        slab2_ref[sl, r0 * part:(r0 + 1) * part, :] = (
                            slab_ref[sl, pl.ds(r0, part, stride=STRIDE_ONE_OP), :])
                src_ref = slab2_ref
                groups = [(r0 * part, r0) for r0 in range(STRIDE_ONE_OP)]
            for base, r0 in groups:
                for r1 in range(stride):
                    r = r1 * (dil // stride) + r0
                    for sl in range(nslab):
                        c0 = (r * 3 + comp) * A_WIDTH + sl * LANES
                        a_ref[0, :, c0:c0 + LANES] = (
                            src_ref[sl, pl.ds(base + r1, tm // dil, stride=stride), :].astype(bf16))

        return (lambda: _dot(hm, wa_ref[:, w_cols])), epilogue, A_WIDTH

    def gate_epilogue(gpre):
        ns = 2 * M_HEADS
        lc = M_CHUNK
        sub = lax.broadcasted_iota(jnp.int32, (2 * ns, tm), 0)
        lrow = jnp.where(sub >= ns, jax.nn.log_sigmoid(gpre + gb_ref[...]), gpre + gb_ref[...])
        tri_r = lax.broadcasted_iota(jnp.int32, (lc, lc), 0)
        tri_c = lax.broadcasted_iota(jnp.int32, (lc, lc), 1)
        upper = (tri_c >= tri_r).astype(f32)
        lower = (tri_c <= tri_r).astype(f32)
        fwd_rows = lax.broadcasted_iota(jnp.int32, (ns, lc), 0) < M_HEADS
        for c in range(tm // lc):
            ls = slice(c * lc, (c + 1) * lc)
            blk = lrow[:, ls]
            lf = blk[ns:]
            f_cum = jnp.where(fwd_rows, _dot_exact(lf, upper), _dot_exact(lf, lower))
            f_all = jnp.sum(lf, axis=-1, keepdims=True)
            gr_ref[0, c, 0:ns, :] = (blk[:ns] - f_cum) * LOG2E
            gr_ref[0, c, ns:2 * ns, :] = lf * LOG2E
            gr_ref[0, c, 2 * ns:3 * ns, :] = jnp.broadcast_to(f_all * LOG2E, (ns, lc))
            gr_ref[0, c, 3 * ns:, :] = jnp.zeros((GATE_ROWS - 3 * ns, lc), f32)

    n_qk = 2 * M_WIDTH // PROJ_COLS
    n_v = M_WIDTH // PROJ_COLS
    light = [attn_item(n) for n in range(3 * N_GROUPS)] + [v_item(c) for c in range(n_v)]
    items = [((lambda: _dot_nt(wift_ref[...], hm)), gate_epilogue, 4 * M_HEADS)]
    for c in range(n_qk):
        items.append(qk_item(c))
        items.append(light.pop())
    items += light

    yield 0
    pending = None
    for matmul, epilogue, cols in items:
        val = matmul()
        if pending is not None:
            pending()
        pending = functools.partial(epilogue, val)
        yield cols
    pending()
    yield 0


PROJ_WEIGHT = 4 * M_HEADS + 3 * M_WIDTH + 3 * N_GROUPS * A_WIDTH


def _proj_parts(x, pre_w, conv_w, conv_b, wqk, wv, wift, gate_b, wa):
    b_, s_, _ = x.shape
    tm = PROJ_ROWS
    hb = tm // PROJ_HALO
    n_halo_blocks = s_ // PROJ_HALO
    row_spec = lambda width: pl.BlockSpec((1, tm, width), lambda b, i: (b, i, 0))
    in_specs = [
        row_spec(D_MODEL),
        pl.BlockSpec((1, PROJ_HALO, D_MODEL), lambda b, i: (b, jnp.maximum(i * hb - 1, 0), 0)),
        pl.BlockSpec((1, PROJ_HALO, D_MODEL), lambda b, i: (b, jnp.minimum((i + 1) * hb, n_halo_blocks - 1), 0)),
        _const_spec(pre_w.shape), _const_spec(conv_w.shape), _const_spec(conv_b.shape),
        _const_spec(wqk.shape), _const_spec(wv.shape), _const_spec(wift.shape), _const_spec(gate_b.shape),
        _const_spec(wa.shape),
    ]
    act = lambda width: jax.ShapeDtypeStruct((b_, s_, width), bf16)
    chunk_spec = lambda height: pl.BlockSpec((1, tm // M_CHUNK, height, M_CHUNK), lambda b, i: (b, i, 0, 0))
    out_shape = [act(M_WIDTH), jax.ShapeDtypeStruct((b_, s_ // M_CHUNK, M_WIDTH, M_CHUNK), bf16), act(M_WIDTH),
                 jax.ShapeDtypeStruct((b_, s_ // M_CHUNK, GATE_ROWS, M_CHUNK), f32)]
    out_specs = [row_spec(M_WIDTH), chunk_spec(M_WIDTH), row_spec(M_WIDTH), chunk_spec(GATE_ROWS)]
    for _, dil in A_PATTERNS:
        out_shape.append(jax.ShapeDtypeStruct((b_, s_ // dil, dil * 3 * A_WIDTH), bf16))
        out_specs.append(pl.BlockSpec((1, tm // dil, dil * 3 * A_WIDTH), lambda b, i: (b, i, 0)))
    scratch = ([pltpu.VMEM((A_WIDTH // LANES, tm, LANES), f32)] * 2
               + [pltpu.VMEM((PROJ_COLS // LANES, tm + 2 * PROJ_HALO, LANES), f32),
                  pltpu.VMEM((PROJ_COLS // LANES, tm, LANES), f32)])
    return dict(grid=(b_, s_ // tm), in_specs=in_specs, out_specs=out_specs, out_shape=out_shape, scratch=scratch,
                args=(x, x, x, pre_w, conv_w, conv_b, wqk, wv, wift, gate_b, wa))


def _proj(*operands):
    parts = _proj_parts(*operands)
    return pl.pallas_call(
        _proj_body,
        grid=parts["grid"],
        in_specs=parts["in_specs"],
        out_specs=parts["out_specs"],
        out_shape=parts["out_shape"],
        scratch_shapes=parts["scratch"],
        compiler_params=pltpu.CompilerParams(
            dimension_semantics=("parallel", "parallel"), vmem_limit_bytes=VMEM_LIMIT_BYTES),
        name="proj",
    )(*parts["args"])


def _mlstm_reset(j, c_ref, n_ref, m_ref):
    @pl.when(j == 0)
    def _():
        c_ref[...] = jnp.zeros_like(c_ref)
        n_ref[...] = jnp.zeros_like(n_ref)
        m_ref[...] = jnp.zeros_like(m_ref)


MLSTM_N_STEPS = 3 * 2 * M_HEADS


def _mlstm_chunk_steps(step, step_chunks, qf_ref, ktf_ref, vf_ref, grf_ref, qb_ref, ktb_ref, vb_ref, grb_ref,
                       hf_ref, hb_ref, c_ref, n_ref, m_ref):
    lc = M_CHUNK
    nh = M_HEADS
    ns = 2 * M_HEADS
    e = M_HEAD_DIM
    row = lax.broadcasted_iota(jnp.int32, (lc, lc), 0)
    col = lax.broadcasted_iota(jnp.int32, (lc, lc), 1)

    def head_cols(hh):
        return slice(hh * e, (hh + 1) * e)

    streams = []
    for d in range(2):
        q_ref, kt_ref, v_ref, gr_ref, out_ref = (
            (qf_ref, ktf_ref, vf_ref, grf_ref, hf_ref) if d == 0 else (qb_ref, ktb_ref, vb_ref, grb_ref, hb_ref))
        mask = (col <= row) if d == 0 else (col >= row)
        chunk = step if d == 0 else step_chunks - 1 - step
        tok = slice(chunk * lc, (chunk + 1) * lc)
        for hh in range(nh):
            streams.append((d * nh + hh, hh, q_ref, kt_ref, v_ref, gr_ref, out_ref, mask, chunk, tok))

    gated = []
    for ci, hh, q_ref, kt_ref, v_ref, gr_ref, out_ref, mask, chunk, tok in streams:
        r_row = gr_ref[0, chunk, ci:ci + 1, :]
        lf_row = gr_ref[0, chunk, ns + ci:ns + ci + 1, :]
        m_sc = m_ref[ci][:, 0:1]
        s_aug = _dot(q_ref[0, tok, head_cols(hh)],
                     jnp.concatenate([kt_ref[0, chunk, head_cols(hh), :], n_ref[ci].astype(bf16)], axis=1))
        r_mat = jnp.where(mask, r_row, NEG)
        u = jnp.maximum(m_sc, jnp.max(r_mat, axis=-1, keepdims=True))
        f_col = jnp.sum(jnp.where(mask, lf_row, 0.0), axis=-1, keepdims=True)
        p = s_aug[:, :lc] * jnp.exp2(r_mat - u)
        wa = jnp.exp2(m_sc - u)
        den = jnp.sum(p, axis=-1, keepdims=True) + wa * s_aug[:, lc:]
        inv = 1.0 / jnp.maximum(jnp.abs(den), jnp.exp2(-(f_col + u)))
        gated.append((p.astype(bf16), wa, inv))
        yield 1

    for (ci, hh, q_ref, kt_ref, v_ref, gr_ref, out_ref, mask, chunk, tok), (p, wa, inv) in zip(streams, gated):
        acc = (_dot(p, v_ref[0, tok, head_cols(hh)])
               + wa * _dot(q_ref[0, tok, head_cols(hh)], c_ref[ci].astype(bf16)))
        for part in range(e // LANES):
            ps = slice(part * LANES, (part + 1) * LANES)
            out_ref[0, tok, hh * e + part * LANES:hh * e + (part + 1) * LANES] = (acc[:, ps] * inv).astype(bf16)
        yield 1

    for ci, hh, q_ref, kt_ref, v_ref, gr_ref, out_ref, mask, chunk, tok in streams:
        r_row = gr_ref[0, chunk, ci:ci + 1, :]
        f_all = gr_ref[0, chunk, 2 * ns + ci:2 * ns + ci + 1, :]
        m_old = m_ref[ci]
        g = f_all + r_row
        m_new = jnp.maximum(f_all + m_old, jnp.max(g, axis=-1, keepdims=True))
        kw = kt_ref[0, chunk, head_cols(hh), :].astype(f32) * jnp.exp2(g - m_new)
        decay = jnp.exp2(f_all + m_old - m_new)[:, 0:1]
        c_ref[ci] = decay * c_ref[ci] + _dot(kw.astype(bf16), v_ref[0, tok, head_cols(hh)])
        n_ref[ci] = decay * n_ref[ci] + jnp.sum(kw, axis=-1, keepdims=True)
        m_ref[ci] = m_new
        yield 1


def _mlstm_parts(q, kt, v, grow, step_chunks):
    b_, s_, _ = q.shape
    sc = step_chunks
    lc = M_CHUNK * sc
    nc = s_ // lc
    fwd = pl.BlockSpec((1, lc, M_WIDTH), lambda b, j: (b, j, 0))
    bwd = pl.BlockSpec((1, lc, M_WIDTH), lambda b, j: (b, nc - 1 - j, 0))
    fwd_t = lambda height: pl.BlockSpec((1, sc, height, M_CHUNK), lambda b, j: (b, j, 0, 0))
    bwd_t = lambda height: pl.BlockSpec((1, sc, height, M_CHUNK), lambda b, j: (b, nc - 1 - j, 0, 0))
    out = jax.ShapeDtypeStruct((b_, s_, M_WIDTH), bf16)
    scratch = [pltpu.VMEM((2 * M_HEADS, M_HEAD_DIM, M_HEAD_DIM), f32),
               pltpu.VMEM((2 * M_HEADS, M_HEAD_DIM, LANES), f32),
               pltpu.VMEM((2 * M_HEADS, 1, LANES), f32)]
    return dict(grid=(b_, nc), out_specs=[fwd, bwd], out_shape=[out, out], scratch=scratch,
                in_specs=[fwd, fwd_t(M_WIDTH), fwd, fwd_t(GATE_ROWS), bwd, bwd_t(M_WIDTH), bwd, bwd_t(GATE_ROWS)],
                args=(q, kt, v, grow, q, kt, v, grow))


def _respec(specs, adapt):
    return [pl.BlockSpec(s.block_shape, (lambda step, f=s.index_map: f(*adapt(step))), pipeline_mode=s.pipeline_mode)
            for s in specs]


def _with_scan_body(*refs, tile_steps, tile_weight, counts, tiles, scan_steps, step_chunks):
    step = pl.program_id(0)
    n_ti, n_to, n_ts = counts
    bounds = np.cumsum([0, n_ti, 8, n_to, 2, n_ts])
    t_in, m_in, t_out, m_out, t_scr = [refs[bounds[k]:bounds[k + 1]] for k in range(5)]
    m_scr = refs[bounds[5]:]
    scan_refs = (*m_in, *m_out, *m_scr)
    _mlstm_reset(step % scan_steps, *m_scr)

    def scan():
        for st in range(step_chunks):
            yield from _mlstm_chunk_steps(st, step_chunks, *scan_refs)

    _interleave(tile_steps(step % tiles, tiles, *t_in, *t_out, *t_scr), scan(),
                tile_weight, step_chunks * MLSTM_N_STEPS)


def _with_scan(name, parts, tile_steps, tile_weight, scan_operands):
    q = scan_operands[0]
    tiles = parts["grid"][1]
    n_steps = parts["grid"][0] * tiles
    seq_chunks = q.shape[1] // M_CHUNK
    step_chunks = q.shape[0] * seq_chunks // n_steps
    assert step_chunks * n_steps == q.shape[0] * seq_chunks and seq_chunks % step_chunks == 0
    mp = _mlstm_parts(*scan_operands, step_chunks)
    scan_steps = mp["grid"][1]
    t_adapt = lambda step: (step // tiles, step % tiles)
    m_adapt = lambda step: (step // scan_steps, step % scan_steps)
    n_out = len(parts["out_shape"])
    outs = pl.pallas_call(
        functools.partial(_with_scan_body, tile_steps=tile_steps, tile_weight=tile_weight,
                          counts=(len(parts["in_specs"]), n_out, len(parts["scratch"])),
                          tiles=tiles, scan_steps=scan_steps, step_chunks=step_chunks),
        grid=(n_steps,),
        in_specs=_respec(parts["in_specs"], t_adapt) + _respec(mp["in_specs"], m_adapt),
        out_specs=_respec(parts["out_specs"], t_adapt) + _respec(mp["out_specs"], m_adapt),
        out_shape=parts["out_shape"] + mp["out_shape"],
        scratch_shapes=parts["scratch"] + mp["scratch"],
        compiler_params=pltpu.CompilerParams(
            dimension_semantics=("arbitrary",), vmem_limit_bytes=VMEM_LIMIT_BYTES),
        name=name,
    )(*parts["args"], *mp["args"])
    return outs[:n_out], outs[n_out:]


def _attn_blocks(q_ref, kwin, vwin, bias_ref, first_keys, sub_len):
    qb = ATT_QBLOCK
    win = qb + 2 * A_HALF
    lane_q = lax.broadcasted_iota(jnp.int32, (qb, LANES), 1)
    lo_q = lane_q < A_HEAD_DIM
    zq = jnp.zeros((qb, LANES), bf16)
    npair = A_HEADS // 2
    last = len(first_keys) - 1
    units = [(b, p, b * qb, slice(p * LANES, (p + 1) * LANES)) for b in range(len(first_keys)) for p in range(npair)]

    raws = []
    for b, p, r0, ps in units:
        qp = q_ref[0, r0:r0 + qb, ps]
        q2 = jnp.concatenate([jnp.where(lo_q, qp, zq), jnp.where(lo_q, zq, qp)], axis=0)
        raws.append(_dot_nt(q2, kwin[r0:r0 + win, ps]))

    edges = {}
    for b in {0, last}:
        kpos = first_keys[b] + lax.broadcasted_iota(jnp.int32, (1, win), 1)
        edges[b] = jnp.where((kpos >= 0) & (kpos < sub_len), 0.0, NEG).astype(f32)

    probs = []
    stats = [jnp.zeros((qb, LANES), f32) for _ in first_keys]
    for (b, p, r0, ps), raw in zip(units, raws):
        s = raw + bias_ref[p]
        if b in edges:
            s = s + edges[b]
        mx = jnp.max(s, axis=-1, keepdims=True)
        pe = jnp.exp2(s - mx)
        den = jnp.sum(pe, axis=-1, keepdims=True)
        probs.append(pe.astype(bf16))
        st = stats[b]
        st = jnp.where(lane_q == 2 * p, mx[:qb], st)
        st = jnp.where(lane_q == 2 * p + 1, mx[qb:], st)
        st = jnp.where(lane_q == A_HEADS + 2 * p, den[:qb], st)
        stats[b] = jnp.where(lane_q == A_HEADS + 2 * p + 1, den[qb:], st)

    outs = [[] for _ in first_keys]
    for (b, p, r0, ps), pb in zip(units, probs):
        vp = vwin[r0:r0 + win, ps]
        outs[b].append(jnp.where(lo_q, _dot(pb[:qb], vp), _dot(pb[qb:], vp)))
    return list(zip(outs, stats))


def _attn_dilated_body(q_ref, k_ref, kp_ref, kn_ref, v_ref, vp_ref, vn_ref, bias_ref, o_ref, st_ref,
                       o_scr, st_scr, *, sub_len, step_rows, dil):
    i = pl.program_id(1)
    r = pl.program_id(2)
    qb = ATT_QBLOCK
    kwin = jnp.concatenate([kp_ref[0], k_ref[0], kn_ref[0]], axis=0)
    vwin = jnp.concatenate([vp_ref[0], v_ref[0], vn_ref[0]], axis=0)
    first_keys = [i * step_rows + blk * qb - A_HALF for blk in range(step_rows // qb)]
    for blk, (outs, stats) in enumerate(_attn_blocks(q_ref, kwin, vwin, bias_ref, first_keys, sub_len)):
        r0 = blk * qb
        rows = pl.ds(r0 * dil + r, qb, stride=dil)
        for p, o in enumerate(outs):
            o_scr[p, rows, :] = o
        st_scr[rows, :] = stats

    @pl.when(r == dil - 1)
    def _():
        for p in range(A_HEADS // 2):
            o_ref[0, :, p * LANES:(p + 1) * LANES] = o_scr[p].astype(bf16)
        st_ref[0] = st_scr[...]


def _attn_mix_body(q_ref, k_ref, kp_ref, kn_ref, v_ref, vp_ref, vn_ref, bias_ref,
                   o1_ref, st1_ref, o2_ref, st2_ref, expand_ref, y_ref, *, sub_len, step_rows):
    i = pl.program_id(1)
    qb = ATT_QBLOCK
    kwin = jnp.concatenate([kp_ref[0], k_ref[0], kn_ref[0]], axis=0)
    vwin = jnp.concatenate([vp_ref[0], v_ref[0], vn_ref[0]], axis=0)
    head_lane = lax.broadcasted_iota(jnp.int32, (qb, LANES), 1) < A_HEADS
    first_keys = [i * step_rows + blk * qb - A_HALF for blk in range(step_rows // qb)]
    for blk, (outs, st0) in enumerate(_attn_blocks(q_ref, kwin, vwin, bias_ref, first_keys, sub_len)):
        r0 = blk * qb
        st1 = st1_ref[0, r0:r0 + qb, :]
        st2 = st2_ref[0, r0:r0 + qb, :]
        dn0, dn1, dn2 = [pltpu.roll(st, LANES - A_HEADS, axis=1) for st in (st0, st1, st2)]
        top = jnp.maximum(jnp.maximum(st0, st1), st2)
        w0, w1, w2 = jnp.exp2(st0 - top), jnp.exp2(st1 - top), jnp.exp2(st2 - top)
        dsum = w0 * dn0 + w1 * dn1 + w2 * dn2
        o_groups = (jnp.concatenate(outs, axis=1),
                    o1_ref[0, r0:r0 + qb, :].astype(f32), o2_ref[0, r0:r0 + qb, :].astype(f32))
        coef = []
        for w in (w0, w1, w2):
            c = jnp.where(head_lane, w / dsum, 0.0)
            c_hi = c.astype(bf16)
            coef.append(jnp.concatenate([c_hi, (c - c_hi.astype(f32)).astype(bf16)], axis=1))
        wide = _dot(jnp.concatenate(coef, axis=0), expand_ref[...])
        y = (wide[:qb] * o_groups[0] + wide[qb:2 * qb] * o_groups[1] + wide[2 * qb:] * o_groups[2])
        y_ref[0, r0:r0 + qb, :] = y.astype(bf16)


def _attn_halo_specs(step_rows, sub_len, width, index):
    hpb = step_rows // A_HALF
    last = sub_len // A_HALF - 1
    before = pl.BlockSpec((1, A_HALF, width), lambda *g: index(g, jnp.maximum(g[1] * hpb - 1, 0)))
    after = pl.BlockSpec((1, A_HALF, width), lambda *g: index(g, jnp.minimum((g[1] + 1) * hpb, last)))
    return before, after


def _attn_dilated(qkv, bias, dil):
    b_, sub_len, _ = qkv.shape
    s_ = sub_len * dil
    step_rows = min(ATT_STEP_ROWS[dil], sub_len)
    tokens = step_rows * dil
    in_specs = []
    for comp in range(3):
        in_specs.append(pl.BlockSpec((1, step_rows, A_WIDTH), lambda b, i, r, comp=comp: (b, i, 3 * r + comp)))
        if comp > 0:
            in_specs += _attn_halo_specs(step_rows, sub_len, A_WIDTH,
                                         lambda g, row, comp=comp: (g[0], row, 3 * g[2] + comp))
    return pl.pallas_call(
        functools.partial(_attn_dilated_body, sub_len=sub_len, step_rows=step_rows, dil=dil),
        grid=(b_, sub_len // step_rows, dil),
        in_specs=in_specs + [_const_spec(bias.shape)],
        out_specs=[pl.BlockSpec((1, tokens, A_WIDTH), lambda b, i, r: (b, i, 0)),
                   pl.BlockSpec((1, tokens, LANES), lambda b, i, r: (b, i, 0))],
        out_shape=[jax.ShapeDtypeStruct((b_, s_, A_WIDTH), bf16), jax.ShapeDtypeStruct((b_, s_, LANES), f32)],
        scratch_shapes=[pltpu.VMEM((A_WIDTH // LANES, tokens, LANES), f32), pltpu.VMEM((tokens, LANES), f32)],
        compiler_params=pltpu.CompilerParams(
            dimension_semantics=("parallel", "parallel", "arbitrary"), vmem_limit_bytes=VMEM_LIMIT_BYTES),
        name=f"attn_d{dil}",
    )(*([qkv] * 7), bias)


def _attn_mix(qkv, bias, o1, st1, o2, st2, expand):
    b_, s_, _ = qkv.shape
    step_rows = min(ATT_STEP_ROWS[1], s_)
    main = lambda width: pl.BlockSpec((1, step_rows, width), lambda b, i: (b, i, 0))
    in_specs = []
    for comp in range(3):
        in_specs.append(pl.BlockSpec((1, step_rows, A_WIDTH), lambda b, i, comp=comp: (b, i, comp)))
        if comp > 0:
            in_specs += _attn_halo_specs(step_rows, s_, A_WIDTH, lambda g, row, comp=comp: (g[0], row, comp))
    return pl.pallas_call(
        functools.partial(_attn_mix_body, sub_len=s_, step_rows=step_rows),
        grid=(b_, s_ // step_rows),
        in_specs=in_specs + [_const_spec(bias.shape),
                             main(A_WIDTH), main(LANES), main(A_WIDTH), main(LANES), _const_spec(expand.shape)],
        out_specs=main(A_WIDTH),
        out_shape=jax.ShapeDtypeStruct((b_, s_, A_WIDTH), bf16),
        compiler_params=pltpu.CompilerParams(
            dimension_semantics=("parallel", "parallel"), vmem_limit_bytes=VMEM_LIMIT_BYTES),
        name="attn_mix",
    )(*([qkv] * 7), bias, o1, st1, o2, st2, expand)


def _final_body(*refs):
    _run(_final_steps(None, None, *refs))


FINAL_WEIGHT = 2 * M_WIDTH + A_WIDTH + 4 * D_MODEL + D_MODEL * A_WIDTH // M_WIDTH


def _final_steps(i, ni, x_ref, hf_ref, hb_ref, ya_ref, prew_ref, w4_ref, hnw_ref, wpm_ref, wpa_ref, wout_ref,
                 postw_ref, y_ref):
    x = x_ref[0]
    h = _rms(x, prew_ref[...]).astype(bf16)
    hsum = hf_ref[0].astype(f32) + hb_ref[0].astype(f32)
    off_z, off_az, off_ga = M_WIDTH, 2 * M_WIDTH, 2 * M_WIDTH + A_WIDTH
    off_gb = off_ga + D_MODEL
    yield 0
    parts = []
    for hh in range(M_HEADS):
        hs = slice(hh * M_HEAD_DIM, (hh + 1) * M_HEAD_DIM)
        o = _dot(h, w4_ref[:, hs])
        z = _dot(h, w4_ref[:, off_z + hh * M_HEAD_DIM:off_z + (hh + 1) * M_HEAD_DIM])
        hx = hsum[:, hs]
        hn = hx * lax.rsqrt(jnp.mean(hx * hx, axis=-1, keepdims=True) + EPS) * hnw_ref[:, hs]
        parts.append((hn * _sigmoid(o) * _silu(z)).astype(bf16))
        yield 2 * M_HEAD_DIM
    ym = jnp.concatenate(parts, axis=1)
    az = _dot(h, w4_ref[:, off_az:off_az + A_WIDTH])
    ya = (ya_ref[0].astype(f32) * _silu(az)).astype(bf16)
    yield A_WIDTH
    pm = _dot(ym, wpm_ref[...])
    yield D_MODEL
    pa = _dot(ya, wpa_ref[...])
    yield D_MODEL * A_WIDTH // M_WIDTH
    ga = _sigmoid(_dot(h, w4_ref[:, off_ga:off_ga + D_MODEL]))
    yield D_MODEL
    gb = _sigmoid(_dot(h, w4_ref[:, off_gb:off_gb + D_MODEL]))
    merged = (ga * pm + gb * pa).astype(bf16)
    yield D_MODEL
    out = _dot(merged, wout_ref[...])
    y_ref[0] = x + _rms(out, postw_ref[...])
    yield D_MODEL


def _final_parts(x, hf, hb, ya, pre_w, w4, hnw, wpm, wpa, wout, post_w, rows):
    b_, s_, _ = x.shape
    row_spec = lambda width: pl.BlockSpec((1, rows, width), lambda b, i: (b, i, 0))
    consts = [pre_w, w4, hnw, wpm, wpa, wout, post_w]
    return dict(grid=(b_, s_ // rows), out_specs=[row_spec(D_MODEL)],
                in_specs=[row_spec(D_MODEL), row_spec(M_WIDTH), row_spec(M_WIDTH), row_spec(A_WIDTH)]
                         + [_const_spec(c.shape) for c in consts],
                out_shape=[jax.ShapeDtypeStruct(x.shape, x.dtype)], scratch=[], args=(x, hf, hb, ya, *consts))


def _final(*operands):
    parts = _final_parts(*operands, FINAL_ROWS)
    return pl.pallas_call(
        _final_body,
        grid=parts["grid"],
        in_specs=parts["in_specs"],
        out_specs=parts["out_specs"],
        out_shape=parts["out_shape"],
        compiler_params=pltpu.CompilerParams(
            dimension_semantics=("parallel", "parallel"), vmem_limit_bytes=VMEM_LIMIT_BYTES),
        name="final",
    )(*parts["args"])[0]


def _t5_bucket(rel):
    nb = N_BUCKETS // 2
    exact = nb // 2
    n = np.abs(rel)
    large = exact + (np.log(np.maximum(n, 1) / exact) / math.log(MAX_DISTANCE / exact) * (nb - exact)).astype(np.int32)
    large = np.minimum(large, nb - 1)
    return (rel > 0).astype(np.int32) * nb + np.where(n < exact, n, large)


def _attn_bias(rel_table, g, dil):
    win = ATT_QBLOCK + 2 * A_HALF
    off = np.arange(win)[None, :] - A_HALF - np.arange(ATT_QBLOCK)[:, None]
    band = np.abs(off) <= A_HALF
    bucket = np.where(band, _t5_bucket(off * dil), -1)
    onehot = jnp.asarray(bucket[None] == np.arange(N_BUCKETS)[:, None, None])
    table = rel_table.astype(f32)[:, g, :]
    bias = jnp.sum(jnp.where(onehot[:, None], table[:, :, None, None], 0.0), axis=0)
    bias = jnp.where(band[None], bias * LOG2E, NEG)
    return bias.reshape(A_HEADS // 2, 2 * ATT_QBLOCK, win)


def _head_expand_matrix():
    e = np.zeros((2 * LANES, A_WIDTH), np.float32)
    for h in range(A_HEADS):
        e[h, h * A_HEAD_DIM:(h + 1) * A_HEAD_DIM] = 1.0
        e[LANES + h, h * A_HEAD_DIM:(h + 1) * A_HEAD_DIM] = 1.0
    return jnp.asarray(e, bf16)


def _proj_operands(x, p):
    return (x, p["pre_w"], p["conv_w"], p["conv_b"], p["wqk"], p["wv"], p["wift"], p["bias_r"], p["wa"])


def _attention(a, p):
    dilated = []
    for g, (_, dil) in enumerate(A_PATTERNS):
        if dil > 1:
            dilated += _attn_dilated(a[g], p["attn_bias"][g], dil)
    return _attn_mix(a[0], p["attn_bias"][0], *dilated, _head_expand_matrix())


def _final_operands(x, hf, hb, ya, p):
    return (x, hf, hb, ya, p["pre_w"], p["w4"], p["hnw"], p["wpm"], p["wpa"], p["wout"], p["post_w"])


def _layer_pair(x_first, x_second, p):
    *scan_in, a_first = _split_proj(_proj(*_proj_operands(x_first, p)))
    proj_second, (hf, hb) = _with_scan("proj_mlstm", _proj_parts(*_proj_operands(x_second, p)), _proj_steps,
                                       PROJ_WEIGHT, tuple(scan_in))
    *scan_in, a_second = _split_proj(proj_second)
    final_first = _final_parts(*_final_operands(x_first, hf, hb, _attention(a_first, p), p), FUSED_FINAL_ROWS)
    (y_first,), (hf, hb) = _with_scan("final_mlstm", final_first, _final_steps, FINAL_WEIGHT, tuple(scan_in))
    return y_first, _final(*_final_operands(x_second, hf, hb, _attention(a_second, p), p))


def _split_proj(outs):
    q, kt, v, grow, *a = outs
    return q, kt, v, grow, a


def kernel(x_prompt, x_sample, pre_norm_w, w_in, m_conv_w, m_conv_b, m_igate_b, m_fgate_b, m_head_norm_w,
           w_proj_m, w_proj_a, w_out, post_norm_w, rel_bias_table):
    depth = pre_norm_w.shape[0]
    params = []
    for l in range(depth):
        w = w_in[l]
        n_query = N_GROUPS * A_WIDTH
        wa = jnp.concatenate([w[:, OFF_AQKV:OFF_AQKV + n_query] * (A_HEAD_DIM ** -0.5 * LOG2E),
                              w[:, OFF_AQKV + n_query:OFF_AZ]], axis=1)
        gate_b = jnp.concatenate([m_igate_b[l].reshape(-1), m_fgate_b[l].reshape(-1)]).astype(f32)
        params.append(dict(
            pre_w=pre_norm_w[l].reshape(1, D_MODEL), post_w=post_norm_w[l].reshape(1, D_MODEL),
            conv_w=m_conv_w[l], conv_b=m_conv_b[l].reshape(1, 2 * M_WIDTH),
            wqk=w[:, OFF_QK:OFF_V].astype(bf16), wv=w[:, OFF_V:OFF_O].astype(bf16),
            wift=w[:, OFF_I:OFF_AQKV].T.astype(bf16), wa=wa.astype(bf16),
            bias_r=gate_b.reshape(4 * M_HEADS, 1),
            w4=jnp.concatenate([w[:, OFF_O:OFF_I], w[:, OFF_AZ:]], axis=1).astype(bf16),
            hnw=m_head_norm_w[l].reshape(1, M_WIDTH),
            wpm=w_proj_m[l].astype(bf16), wpa=w_proj_a[l].astype(bf16), wout=w_out[l].astype(bf16),
            attn_bias=[_attn_bias(rel_bias_table, g, dil) for g, (_, dil) in enumerate(A_PATTERNS)],
        ))

    for p in params:
        x_prompt, x_sample = _layer_pair(x_prompt, x_sample, p)
    return (x_prompt, x_sample)
```

```python
import functools
import math

import numpy as np
import jax
import jax.numpy as jnp
from jax import lax
from jax.experimental import pallas as pl
from jax.experimental.pallas import tpu as pltpu

D_MODEL = 1024
M_HEADS = 4
M_HEAD_DIM = 256
M_WIDTH = M_HEADS * M_HEAD_DIM
M_CHUNK = 128
A_PATTERNS = ((128, 1), (512, 4), (2048, 16))
N_GROUPS = 3
A_HEADS = 8
A_HEAD_DIM = 64
A_WIDTH = A_HEADS * A_HEAD_DIM
A_HALF = 64
N_BUCKETS = 32
MAX_DISTANCE = 1024
EPS = 1e-6
NEG = -1e30
LOG2E = math.log2(math.e)

OFF_QK = 0
OFF_V = 2 * M_WIDTH
OFF_O = OFF_V + M_WIDTH
OFF_Z = OFF_O + M_WIDTH
OFF_I = OFF_Z + M_WIDTH
OFF_F = OFF_I + 2 * M_HEADS
OFF_AQKV = OFF_F + 2 * M_HEADS
OFF_AZ = OFF_AQKV + 3 * N_GROUPS * A_WIDTH
OFF_GATE = OFF_AZ + A_WIDTH

LANES = 128
STRIDE_ONE_OP = 4
VMEM_LIMIT_BYTES = 60 * 1024 * 1024

GATE_ROWS = 32
PROJ_ROWS = 512
PROJ_HALO = 8
PROJ_COLS = 256
FINAL_ROWS = 512
FUSED_FINAL_ROWS = 256
ATT_QBLOCK = 128
ATT_STEP_ROWS = {1: 1024, 4: 1024, 16: 512}

f32 = jnp.float32
bf16 = jnp.bfloat16


def _const_spec(shape):
    nd = len(shape)
    return pl.BlockSpec(shape, lambda *_: (0,) * nd, pipeline_mode=pl.Buffered(1))


def _dot(a, b):
    return jnp.dot(a, b, preferred_element_type=f32)


def _dot_nt(a, b):
    return lax.dot_general(a, b, (((1,), (1,)), ((), ())), preferred_element_type=f32)


def _dot_exact(a, b):
    return jnp.dot(a, b, preferred_element_type=f32, precision=lax.Precision.HIGHEST)


def _rms(x, w):
    return x * lax.rsqrt(jnp.mean(x * x, axis=-1, keepdims=True) + EPS) * w


def _sigmoid(x):
    return 0.5 + 0.5 * jnp.tanh(0.5 * x)


def _silu(x):
    half = 0.5 * x
    return half + half * jnp.tanh(half)


def _run(steps):
    for _ in steps:
        pass


def _interleave(first, second, total_first, total_second):
    done = [0.0, 0.0]
    gens = [first, second]
    total = [total_first, total_second]
    live = [True, True]
    while live[0] or live[1]:
        pick = 0 if (live[0] and (not live[1] or done[0] * total[1] <= done[1] * total[0])) else 1
        try:
            done[pick] += next(gens[pick])
        except StopIteration:
            live[pick] = False


def _proj_body(*refs):
    _run(_proj_steps(pl.program_id(1), pl.num_programs(1), *refs))


def _proj_steps(i, ni, x_ref, xp_ref, xn_ref, prew_ref, cw_ref, cb_ref, wqk_ref, wv_ref, wift_ref, gb_ref, wa_ref,
                q_ref, kt_ref, v_ref, gr_ref, *a_refs):
    tm = x_ref.shape[1]
    keep_prev = (i > 0).astype(f32)
    keep_next = (i < ni - 1).astype(f32)
    xa = jnp.concatenate([xp_ref[0] * keep_prev, x_ref[0], xn_ref[0] * keep_next], axis=0)
    hf = _rms(xa, prew_ref[...])
    h_ext = hf.astype(bf16)
    hm = hf[PROJ_HALO:PROJ_HALO + tm].astype(bf16)
    a_refs, (slab_ref, slab2_ref, conv_ref, act_ref) = a_refs[:-4], a_refs[-4:]
    nslab = A_WIDTH // LANES

    def qk_item(c):
        cs = slice(c * PROJ_COLS, (c + 1) * PROJ_COLS)

        def epilogue(r):
            half = tm // 2
            for sl in range(PROJ_COLS // LANES):
                conv_ref[sl] = r[:, sl * LANES:(sl + 1) * LANES]
            for sl in range(PROJ_COLS // LANES):
                col = slice(c * PROJ_COLS + sl * LANES, c * PROJ_COLS + (sl + 1) * LANES)
                w = cw_ref[:, col]
                even = conv_ref[sl, pl.ds(PROJ_HALO, half, stride=2), :]
                odd = conv_ref[sl, pl.ds(PROJ_HALO + 1, half, stride=2), :]
                odd_prev = conv_ref[sl, pl.ds(PROJ_HALO - 1, half, stride=2), :]
                even_next = conv_ref[sl, pl.ds(PROJ_HALO + 2, half, stride=2), :]
                y_even = cb_ref[:, col] + odd_prev * w[0:1] + even * w[1:2] + odd * w[2:3]
                y_odd = cb_ref[:, col] + even * w[0:1] + odd * w[1:2] + even_next * w[2:3]
                act_ref[sl, pl.ds(0, half, stride=2), :] = _silu(y_even)
                act_ref[sl, pl.ds(1, half, stride=2), :] = _silu(y_odd)
                if c * PROJ_COLS < M_WIDTH:
                    q_ref[0, :, col] = act_ref[sl].astype(bf16)
                else:
                    rows = slice(col.start - M_WIDTH, col.stop - M_WIDTH)
                    kt = (act_ref[sl] * (M_HEAD_DIM ** -0.5)).T.astype(bf16)
                    for ch in range(tm // M_CHUNK):
                        kt_ref[0, ch, rows, :] = kt[:, ch * M_CHUNK:(ch + 1) * M_CHUNK]

        return (lambda: _dot(h_ext, wqk_ref[:, cs])), epilogue, PROJ_COLS

    def v_item(c):
        cs = slice(c * PROJ_COLS, (c + 1) * PROJ_COLS)

        def epilogue(res):
            v_ref[0, :, cs] = res.astype(bf16)

        return (lambda: _dot(hm, wv_ref[:, cs])), epilogue, PROJ_COLS

    def attn_item(n):
        g, comp = divmod(n, 3)
        a_ref = a_refs[g]
        dil = A_PATTERNS[g][1]
        w_cols = slice((comp * N_GROUPS + g) * A_WIDTH, (comp * N_GROUPS + g + 1) * A_WIDTH)

        def epilogue(res):
            if dil == 1:
                a_ref[0, :, comp * A_WIDTH:(comp + 1) * A_WIDTH] = res.astype(bf16)
                return
            for sl in range(nslab):
                slab_ref[sl] = res[:, sl * LANES:(sl + 1) * LANES]
            src_ref, groups = slab_ref, [(0, 0)]
            stride = dil
            if dil > STRIDE_ONE_OP:
                stride = dil // STRIDE_ONE_OP
                part = tm // STRIDE_ONE_OP
                for r0 in range(STRIDE_ONE_OP):
                    for sl in range(nslab):
                        slab2_ref[sl, r0 * part:(r0 + 1) * part, :] = (
                            slab_ref[sl, pl.ds(r0, part, stride=STRIDE_ONE_OP), :])
                src_ref = slab2_ref
                groups = [(r0 * part, r0) for r0 in range(STRIDE_ONE_OP)]
            for base, r0 in groups:
                for r1 in range(stride):
                    r = r1 * (dil // stride) + r0
                    for sl in range(nslab):
                        c0 = (r * 3 + comp) * A_WIDTH + sl * LANES
                        a_ref[0, :, c0:c0 + LANES] = (
                            src_ref[sl, pl.ds(base + r1, tm // dil, stride=stride), :].astype(bf16))

        return (lambda: _dot(hm, wa_ref[:, w_cols])), epilogue, A_WIDTH

    def gate_epilogue(gpre):
        ns = 2 * M_HEADS
        lc = M_CHUNK
        sub = lax.broadcasted_iota(jnp.int32, (2 * ns, tm), 0)
        lrow = jnp.where(sub >= ns, jax.nn.log_sigmoid(gpre + gb_ref[...]), gpre + gb_ref[...])
        tri_r = lax.broadcasted_iota(jnp.int32, (lc, lc), 0)
        tri_c = lax.broadcasted_iota(jnp.int32, (lc, lc), 1)
        upper = (tri_c >= tri_r).astype(f32)
        lower = (tri_c <= tri_r).astype(f32)
        fwd_rows = lax.broadcasted_iota(jnp.int32, (ns, lc), 0) < M_HEADS
        for c in range(tm // lc):
            ls = slice(c * lc, (c + 1) * lc)
            blk = lrow[:, ls]
            lf = blk[ns:]
            f_cum = jnp.where(fwd_rows, _dot_exact(lf, upper), _dot_exact(lf, lower))
            f_all = jnp.sum(lf, axis=-1, keepdims=True)
            gr_ref[0, c, 0:ns, :] = (blk[:ns] - f_cum) * LOG2E
            gr_ref[0, c, ns:2 * ns, :] = lf * LOG2E
            gr_ref[0, c, 2 * ns:3 * ns, :] = jnp.broadcast_to(f_all * LOG2E, (ns, lc))
            gr_ref[0, c, 3 * ns:, :] = jnp.zeros((GATE_ROWS - 3 * ns, lc), f32)

    n_qk = 2 * M_WIDTH // PROJ_COLS
    n_v = M_WIDTH // PROJ_COLS
    light = [attn_item(n) for n in range(3 * N_GROUPS)] + [v_item(c) for c in range(n_v)]
    items = [((lambda: _dot_nt(wift_ref[...], hm)), gate_epilogue, 4 * M_HEADS)]
    for c in range(n_qk):
        items.append(qk_item(c))
        items.append(light.pop())
    items += light

    yield 0
    pending = None
    for matmul, epilogue, cols in items:
        val = matmul()
        if pending is not None:
            pending()
        pending = functools.partial(epilogue, val)
        yield cols
    pending()
    yield 0


PROJ_WEIGHT = 4 * M_HEADS + 3 * M_WIDTH + 3 * N_GROUPS * A_WIDTH


def _proj_parts(x, pre_w, conv_w, conv_b, wqk, wv, wift, gate_b, wa):
    b_, s_, _ = x.shape
    tm = PROJ_ROWS
    hb = tm // PROJ_HALO
    n_halo_blocks = s_ // PROJ_HALO
    row_spec = lambda width: pl.BlockSpec((1, tm, width), lambda b, i: (b, i, 0))
    in_specs = [
        row_spec(D_MODEL),
        pl.BlockSpec((1, PROJ_HALO, D_MODEL), lambda b, i: (b, jnp.maximum(i * hb - 1, 0), 0)),
        pl.BlockSpec((1, PROJ_HALO, D_MODEL), lambda b, i: (b, jnp.minimum((i + 1) * hb, n_halo_blocks - 1), 0)),
        _const_spec(pre_w.shape), _const_spec(conv_w.shape), _const_spec(conv_b.shape),
        _const_spec(wqk.shape), _const_spec(wv.shape), _const_spec(wift.shape), _const_spec(gate_b.shape),
        _const_spec(wa.shape),
    ]
    act = lambda width: jax.ShapeDtypeStruct((b_, s_, width), bf16)
    chunk_spec = lambda height: pl.BlockSpec((1, tm // M_CHUNK, height, M_CHUNK), lambda b, i: (b, i, 0, 0))
    out_shape = [act(M_WIDTH), jax.ShapeDtypeStruct((b_, s_ // M_CHUNK, M_WIDTH, M_CHUNK), bf16), act(M_WIDTH),
                 jax.ShapeDtypeStruct((b_, s_ // M_CHUNK, GATE_ROWS, M_CHUNK), f32)]
    out_specs = [row_spec(M_WIDTH), chunk_spec(M_WIDTH), row_spec(M_WIDTH), chunk_spec(GATE_ROWS)]
    for _, dil in A_PATTERNS:
        out_shape.append(jax.ShapeDtypeStruct((b_, s_ // dil, dil * 3 * A_WIDTH), bf16))
        out_specs.append(pl.BlockSpec((1, tm // dil, dil * 3 * A_WIDTH), lambda b, i: (b, i, 0)))
    scratch = ([pltpu.VMEM((A_WIDTH // LANES, tm, LANES), f32)] * 2
               + [pltpu.VMEM((PROJ_COLS // LANES, tm + 2 * PROJ_HALO, LANES), f32),
                  pltpu.VMEM((PROJ_COLS // LANES, tm, LANES), f32)])
    return dict(grid=(b_, s_ // tm), in_specs=in_specs, out_specs=out_specs, out_shape=out_shape, scratch=scratch,
                args=(x, x, x, pre_w, conv_w, conv_b, wqk, wv, wift, gate_b, wa))


def _proj(*operands):
    parts = _proj_parts(*operands)
    return pl.pallas_call(
        _proj_body,
        grid=parts["grid"],
        in_specs=parts["in_specs"],
        out_specs=parts["out_specs"],
        out_shape=parts["out_shape"],
        scratch_shapes=parts["scratch"],
        compiler_params=pltpu.CompilerParams(
            dimension_semantics=("parallel", "parallel"), vmem_limit_bytes=VMEM_LIMIT_BYTES),
        name="proj",
    )(*parts["args"])


def _mlstm_reset(j, c_ref, n_ref, m_ref):
    @pl.when(j == 0)
    def _():
        c_ref[...] = jnp.zeros_like(c_ref)
        n_ref[...] = jnp.zeros_like(n_ref)
        m_ref[...] = jnp.zeros_like(m_ref)


MLSTM_N_STEPS = 3 * 2 * M_HEADS


def _mlstm_chunk_steps(step, step_chunks, qf_ref, ktf_ref, vf_ref, grf_ref, qb_ref, ktb_ref, vb_ref, grb_ref,
                       hf_ref, hb_ref, c_ref, n_ref, m_ref):
    lc = M_CHUNK
    nh = M_HEADS
    ns = 2 * M_HEADS
    e = M_HEAD_DIM
    row = lax.broadcasted_iota(jnp.int32, (lc, lc), 0)
    col = lax.broadcasted_iota(jnp.int32, (lc, lc), 1)

    def head_cols(hh):
        return slice(hh * e, (hh + 1) * e)

    streams = []
    for d in range(2):
        q_ref, kt_ref, v_ref, gr_ref, out_ref = (
            (qf_ref, ktf_ref, vf_ref, grf_ref, hf_ref) if d == 0 else (qb_ref, ktb_ref, vb_ref, grb_ref, hb_ref))
        mask = (col <= row) if d == 0 else (col >= row)
        chunk = step if d == 0 else step_chunks - 1 - step
        tok = slice(chunk * lc, (chunk + 1) * lc)
        for hh in range(nh):
            streams.append((d * nh + hh, hh, q_ref, kt_ref, v_ref, gr_ref, out_ref, mask, chunk, tok))

    gated = []
    for ci, hh, q_ref, kt_ref, v_ref, gr_ref, out_ref, mask, chunk, tok in streams:
        r_row = gr_ref[0, chunk, ci:ci + 1, :]
        lf_row = gr_ref[0, chunk, ns + ci:ns + ci + 1, :]
        m_sc = m_ref[ci][:, 0:1]
        s_aug = _dot(q_ref[0, tok, head_cols(hh)],
                     jnp.concatenate([kt_ref[0, chunk, head_cols(hh), :], n_ref[ci].astype(bf16)], axis=1))
        r_mat = jnp.where(mask, r_row, NEG)
        u = jnp.maximum(m_sc, jnp.max(r_mat, axis=-1, keepdims=True))
        f_col = jnp.sum(jnp.where(mask, lf_row, 0.0), axis=-1, keepdims=True)
        p = s_aug[:, :lc] * jnp.exp2(r_mat - u)
        wa = jnp.exp2(m_sc - u)
        den = jnp.sum(p, axis=-1, keepdims=True) + wa * s_aug[:, lc:]
        inv = 1.0 / jnp.maximum(jnp.abs(den), jnp.exp2(-(f_col + u)))
        gated.append((p.astype(bf16), wa, inv))
        yield 1

    for (ci, hh, q_ref, kt_ref, v_ref, gr_ref, out_ref, mask, chunk, tok), (p, wa, inv) in zip(streams, gated):
        acc = (_dot(p, v_ref[0, tok, head_cols(hh)])
               + wa * _dot(q_ref[0, tok, head_cols(hh)], c_ref[ci].astype(bf16)))
        for part in range(e // LANES):
            ps = slice(part * LANES, (part + 1) * LANES)
            out_ref[0, tok, hh * e + part * LANES:hh * e + (part + 1) * LANES] = (acc[:, ps] * inv).astype(bf16)
        yield 1

    for ci, hh, q_ref, kt_ref, v_ref, gr_ref, out_ref, mask, chunk, tok in streams:
        r_row = gr_ref[0, chunk, ci:ci + 1, :]
        f_all = gr_ref[0, chunk, 2 * ns + ci:2 * ns + ci + 1, :]
        m_old = m_ref[ci]
        g = f_all + r_row
        m_new = jnp.maximum(f_all + m_old, jnp.max(g, axis=-1, keepdims=True))
        kw = kt_ref[0, chunk, head_cols(hh), :].astype(f32) * jnp.exp2(g - m_new)
        decay = jnp.exp2(f_all + m_old - m_new)[:, 0:1]
        c_ref[ci] = decay * c_ref[ci] + _dot(kw.astype(bf16), v_ref[0, tok, head_cols(hh)])
        n_ref[ci] = decay * n_ref[ci] + jnp.sum(kw, axis=-1, keepdims=True)
        m_ref[ci] = m_new
        yield 1


def _mlstm_parts(q, kt, v, grow, step_chunks):
    b_, s_, _ = q.shape
    sc = step_chunks
    lc = M_CHUNK * sc
    nc = s_ // lc
    fwd = pl.BlockSpec((1, lc, M_WIDTH), lambda b, j: (b, j, 0))
    bwd = pl.BlockSpec((1, lc, M_WIDTH), lambda b, j: (b, nc - 1 - j, 0))
    fwd_t = lambda height: pl.BlockSpec((1, sc, height, M_CHUNK), lambda b, j: (b, j, 0, 0))
    bwd_t = lambda height: pl.BlockSpec((1, sc, height, M_CHUNK), lambda b, j: (b, nc - 1 - j, 0, 0))
    out = jax.ShapeDtypeStruct((b_, s_, M_WIDTH), bf16)
    scratch = [pltpu.VMEM((2 * M_HEADS, M_HEAD_DIM, M_HEAD_DIM), f32),
               pltpu.VMEM((2 * M_HEADS, M_HEAD_DIM, LANES), f32),
               pltpu.VMEM((2 * M_HEADS, 1, LANES), f32)]
    return dict(grid=(b_, nc), out_specs=[fwd, bwd], out_shape=[out, out], scratch=scratch,
                in_specs=[fwd, fwd_t(M_WIDTH), fwd, fwd_t(GATE_ROWS), bwd, bwd_t(M_WIDTH), bwd, bwd_t(GATE_ROWS)],
                args=(q, kt, v, grow, q, kt, v, grow))


def _respec(specs, adapt):
    return [pl.BlockSpec(s.block_shape, (lambda step, f=s.index_map: f(*adapt(step))), pipeline_mode=s.pipeline_mode)
            for s in specs]


def _with_scan_body(*refs, tile_steps, tile_weight, counts, tiles, scan_steps, step_chunks):
    step = pl.program_id(0)
    n_ti, n_to, n_ts = counts
    bounds = np.cumsum([0, n_ti, 8, n_to, 2, n_ts])
    t_in, m_in, t_out, m_out, t_scr = [refs[bounds[k]:bounds[k + 1]] for k in range(5)]
    m_scr = refs[bounds[5]:]
    scan_refs = (*m_in, *m_out, *m_scr)
    _mlstm_reset(step % scan_steps, *m_scr)

    def scan():
        for st in range(step_chunks):
            yield from _mlstm_chunk_steps(st, step_chunks, *scan_refs)

    _interleave(tile_steps(step % tiles, tiles, *t_in, *t_out, *t_scr), scan(),
                tile_weight, step_chunks * MLSTM_N_STEPS)


def _with_scan(name, parts, tile_steps, tile_weight, scan_operands):
    q = scan_operands[0]
    tiles = parts["grid"][1]
    n_steps = parts["grid"][0] * tiles
    seq_chunks = q.shape[1] // M_CHUNK
    step_chunks = q.shape[0] * seq_chunks // n_steps
    assert step_chunks * n_steps == q.shape[0] * seq_chunks and seq_chunks % step_chunks == 0
    mp = _mlstm_parts(*scan_operands, step_chunks)
    scan_steps = mp["grid"][1]
    t_adapt = lambda step: (step // tiles, step % tiles)
    m_adapt = lambda step: (step // scan_steps, step % scan_steps)
    n_out = len(parts["out_shape"])
    outs = pl.pallas_call(
        functools.partial(_with_scan_body, tile_steps=tile_steps, tile_weight=tile_weight,
                          counts=(len(parts["in_specs"]), n_out, len(parts["scratch"])),
                          tiles=tiles, scan_steps=scan_steps, step_chunks=step_chunks),
        grid=(n_steps,),
        in_specs=_respec(parts["in_specs"], t_adapt) + _respec(mp["in_specs"], m_adapt),
        out_specs=_respec(parts["out_specs"], t_adapt) + _respec(mp["out_specs"], m_adapt),
        out_shape=parts["out_shape"] + mp["out_shape"],
        scratch_shapes=parts["scratch"] + mp["scratch"],
        compiler_params=pltpu.CompilerParams(
            dimension_semantics=("arbitrary",), vmem_limit_bytes=VMEM_LIMIT_BYTES),
        name=name,
    )(*parts["args"], *mp["args"])
    return outs[:n_out], outs[n_out:]


def _attn_blocks(q_ref, kwin, vwin, bias_ref, first_keys, sub_len):
    qb = ATT_QBLOCK
    win = qb + 2 * A_HALF
    lane_q = lax.broadcasted_iota(jnp.int32, (qb, LANES), 1)
    lo_q = lane_q < A_HEAD_DIM
    zq = jnp.zeros((qb, LANES), bf16)
    npair = A_HEADS // 2
    last = len(first_keys) - 1
    units = [(b, p, b * qb, slice(p * LANES, (p + 1) * LANES)) for b in range(len(first_keys)) for p in range(npair)]

    raws = []
    for b, p, r0, ps in units:
        qp = q_ref[0, r0:r0 + qb, ps]
        q2 = jnp.concatenate([jnp.where(lo_q, qp, zq), jnp.where(lo_q, zq, qp)], axis=0)
        raws.append(_dot_nt(q2, kwin[r0:r0 + win, ps]))

    edges = {}
    for b in {0, last}:
        kpos = first_keys[b] + lax.broadcasted_iota(jnp.int32, (1, win), 1)
        edges[b] = jnp.where((kpos >= 0) & (kpos < sub_len), 0.0, NEG).astype(f32)

    probs = []
    stats = [jnp.zeros((qb, LANES), f32) for _ in first_keys]
    for (b, p, r0, ps), raw in zip(units, raws):
        s = raw + bias_ref[p]
        if b in edges:
            s = s + edges[b]
        mx = jnp.max(s, axis=-1, keepdims=True)
        pe = jnp.exp2(s - mx)
        den = jnp.sum(pe, axis=-1, keepdims=True)
        probs.append(pe.astype(bf16))
        st = stats[b]
        st = jnp.where(lane_q == 2 * p, mx[:qb], st)
        st = jnp.where(lane_q == 2 * p + 1, mx[qb:], st)
        st = jnp.where(lane_q == A_HEADS + 2 * p, den[:qb], st)
        stats[b] = jnp.where(lane_q == A_HEADS + 2 * p + 1, den[qb:], st)

    outs = [[] for _ in first_keys]
    for (b, p, r0, ps), pb in zip(units, probs):
        vp = vwin[r0:r0 + win, ps]
        outs[b].append(jnp.where(lo_q, _dot(pb[:qb], vp), _dot(pb[qb:], vp)))
    return list(zip(outs, stats))


def _attn_dilated_body(q_ref, k_ref, kp_ref, kn_ref, v_ref, vp_ref, vn_ref, bias_ref, o_ref, st_ref,
                       o_scr, st_scr, *, sub_len, step_rows, dil):
    i = pl.program_id(1)
    r = pl.program_id(2)
    qb = ATT_QBLOCK
    kwin = jnp.concatenate([kp_ref[0], k_ref[0], kn_ref[0]], axis=0)
    vwin = jnp.concatenate([vp_ref[0], v_ref[0], vn_ref[0]], axis=0)
    first_keys = [i * step_rows + blk * qb - A_HALF for blk in range(step_rows // qb)]
    for blk, (outs, stats) in enumerate(_attn_blocks(q_ref, kwin, vwin, bias_ref, first_keys, sub_len)):
        r0 = blk * qb
        rows = pl.ds(r0 * dil + r, qb, stride=dil)
        for p, o in enumerate(outs):
            o_scr[p, rows, :] = o
        st_scr[rows, :] = stats

    @pl.when(r == dil - 1)
    def _():
        for p in range(A_HEADS // 2):
            o_ref[0, :, p * LANES:(p + 1) * LANES] = o_scr[p].astype(bf16)
        st_ref[0] = st_scr[...]


def _attn_mix_body(q_ref, k_ref, kp_ref, kn_ref, v_ref, vp_ref, vn_ref, bias_ref,
                   o1_ref, st1_ref, o2_ref, st2_ref, expand_ref, y_ref, *, sub_len, step_rows):
    i = pl.program_id(1)
    qb = ATT_QBLOCK
    kwin = jnp.concatenate([kp_ref[0], k_ref[0], kn_ref[0]], axis=0)
    vwin = jnp.concatenate([vp_ref[0], v_ref[0], vn_ref[0]], axis=0)
    head_lane = lax.broadcasted_iota(jnp.int32, (qb, LANES), 1) < A_HEADS
    first_keys = [i * step_rows + blk * qb - A_HALF for blk in range(step_rows // qb)]
    for blk, (outs, st0) in enumerate(_attn_blocks(q_ref, kwin, vwin, bias_ref, first_keys, sub_len)):
        r0 = blk * qb
        st1 = st1_ref[0, r0:r0 + qb, :]
        st2 = st2_ref[0, r0:r0 + qb, :]
        dn0, dn1, dn2 = [pltpu.roll(st, LANES - A_HEADS, axis=1) for st in (st0, st1, st2)]
        top = jnp.maximum(jnp.maximum(st0, st1), st2)
        w0, w1, w2 = jnp.exp2(st0 - top), jnp.exp2(st1 - top), jnp.exp2(st2 - top)
        dsum = w0 * dn0 + w1 * dn1 + w2 * dn2
        o_groups = (jnp.concatenate(outs, axis=1),
                    o1_ref[0, r0:r0 + qb, :].astype(f32), o2_ref[0, r0:r0 + qb, :].astype(f32))
        coef = []
        for w in (w0, w1, w2):
            c = jnp.where(head_lane, w / dsum, 0.0)
            c_hi = c.astype(bf16)
            coef.append(jnp.concatenate([c_hi, (c - c_hi.astype(f32)).astype(bf16)], axis=1))
        wide = _dot(jnp.concatenate(coef, axis=0), expand_ref[...])
        y = (wide[:qb] * o_groups[0] + wide[qb:2 * qb] * o_groups[1] + wide[2 * qb:] * o_groups[2])
        y_ref[0, r0:r0 + qb, :] = y.astype(bf16)


def _attn_halo_specs(step_rows, sub_len, width, index):
    hpb = step_rows // A_HALF
    last = sub_len // A_HALF - 1
    before = pl.BlockSpec((1, A_HALF, width), lambda *g: index(g, jnp.maximum(g[1] * hpb - 1, 0)))
    after = pl.BlockSpec((1, A_HALF, width), lambda *g: index(g, jnp.minimum((g[1] + 1) * hpb, last)))
    return before, after


def _attn_dilated(qkv, bias, dil):
    b_, sub_len, _ = qkv.shape
    s_ = sub_len * dil
    step_rows = min(ATT_STEP_ROWS[dil], sub_len)
    tokens = step_rows * dil
    in_specs = []
    for comp in range(3):
        in_specs.append(pl.BlockSpec((1, step_rows, A_WIDTH), lambda b, i, r, comp=comp: (b, i, 3 * r + comp)))
        if comp > 0:
            in_specs += _attn_halo_specs(step_rows, sub_len, A_WIDTH,
                                         lambda g, row, comp=comp: (g[0], row, 3 * g[2] + comp))
    return pl.pallas_call(
        functools.partial(_attn_dilated_body, sub_len=sub_len, step_rows=step_rows, dil=dil),
        grid=(b_, sub_len // step_rows, dil),
        in_specs=in_specs + [_const_spec(bias.shape)],
        out_specs=[pl.BlockSpec((1, tokens, A_WIDTH), lambda b, i, r: (b, i, 0)),
                   pl.BlockSpec((1, tokens, LANES), lambda b, i, r: (b, i, 0))],
        out_shape=[jax.ShapeDtypeStruct((b_, s_, A_WIDTH), bf16), jax.ShapeDtypeStruct((b_, s_, LANES), f32)],
        scratch_shapes=[pltpu.VMEM((A_WIDTH // LANES, tokens, LANES), f32), pltpu.VMEM((tokens, LANES), f32)],
        compiler_params=pltpu.CompilerParams(
            dimension_semantics=("parallel", "parallel", "arbitrary"), vmem_limit_bytes=VMEM_LIMIT_BYTES),
        name=f"attn_d{dil}",
    )(*([qkv] * 7), bias)


def _attn_mix(qkv, bias, o1, st1, o2, st2, expand):
    b_, s_, _ = qkv.shape
    step_rows = min(ATT_STEP_ROWS[1], s_)
    main = lambda width: pl.BlockSpec((1, step_rows, width), lambda b, i: (b, i, 0))
    in_specs = []
    for comp in range(3):
        in_specs.append(pl.BlockSpec((1, step_rows, A_WIDTH), lambda b, i, comp=comp: (b, i, comp)))
        if comp > 0:
            in_specs += _attn_halo_specs(step_rows, s_, A_WIDTH, lambda g, row, comp=comp: (g[0], row, comp))
    return pl.pallas_call(
        functools.partial(_attn_mix_body, sub_len=s_, step_rows=step_rows),
        grid=(b_, s_ // step_rows),
        in_specs=in_specs + [_const_spec(bias.shape),
                             main(A_WIDTH), main(LANES), main(A_WIDTH), main(LANES), _const_spec(expand.shape)],
        out_specs=main(A_WIDTH),
        out_shape=jax.ShapeDtypeStruct((b_, s_, A_WIDTH), bf16),
        compiler_params=pltpu.CompilerParams(
            dimension_semantics=("parallel", "parallel"), vmem_limit_bytes=VMEM_LIMIT_BYTES),
        name="attn_mix",
    )(*([qkv] * 7), bias, o1, st1, o2, st2, expand)


def _final_body(*refs):
    _run(_final_steps(None, None, *refs))


FINAL_WEIGHT = 2 * M_WIDTH + A_WIDTH + 4 * D_MODEL + D_MODEL * A_WIDTH // M_WIDTH


def _final_steps(i, ni, x_ref, hf_ref, hb_ref, ya_ref, prew_ref, w4_ref, hnw_ref, wpm_ref, wpa_ref, wout_ref,
                 postw_ref, y_ref):
    x = x_ref[0]
    h = _rms(x, prew_ref[...]).astype(bf16)
    hsum = hf_ref[0].astype(f32) + hb_ref[0].astype(f32)
    off_z, off_az, off_ga = M_WIDTH, 2 * M_WIDTH, 2 * M_WIDTH + A_WIDTH
    off_gb = off_ga + D_MODEL
    yield 0
    parts = []
    for hh in range(M_HEADS):
        hs = slice(hh * M_HEAD_DIM, (hh + 1) * M_HEAD_DIM)
        o = _dot(h, w4_ref[:, hs])
        z = _dot(h, w4_ref[:, off_z + hh * M_HEAD_DIM:off_z + (hh + 1) * M_HEAD_DIM])
        hx = hsum[:, hs]
        hn = hx * lax.rsqrt(jnp.mean(hx * hx, axis=-1, keepdims=True) + EPS) * hnw_ref[:, hs]
        parts.append((hn * _sigmoid(o) * _silu(z)).astype(bf16))
        yield 2 * M_HEAD_DIM
    ym = jnp.concatenate(parts, axis=1)
    az = _dot(h, w4_ref[:, off_az:off_az + A_WIDTH])
    ya = (ya_ref[0].astype(f32) * _silu(az)).astype(bf16)
    yield A_WIDTH
    pm = _dot(ym, wpm_ref[...])
    yield D_MODEL
    pa = _dot(ya, wpa_ref[...])
    yield D_MODEL * A_WIDTH // M_WIDTH
    ga = _sigmoid(_dot(h, w4_ref[:, off_ga:off_ga + D_MODEL]))
    yield D_MODEL
    gb = _sigmoid(_dot(h, w4_ref[:, off_gb:off_gb + D_MODEL]))
    merged = (ga * pm + gb * pa).astype(bf16)
    yield D_MODEL
    out = _dot(merged, wout_ref[...])
    y_ref[0] = x + _rms(out, postw_ref[...])
    yield D_MODEL


def _final_parts(x, hf, hb, ya, pre_w, w4, hnw, wpm, wpa, wout, post_w, rows):
    b_, s_, _ = x.shape
    row_spec = lambda width: pl.BlockSpec((1, rows, width), lambda b, i: (b, i, 0))
    consts = [pre_w, w4, hnw, wpm, wpa, wout, post_w]
    return dict(grid=(b_, s_ // rows), out_specs=[row_spec(D_MODEL)],
                in_specs=[row_spec(D_MODEL), row_spec(M_WIDTH), row_spec(M_WIDTH), row_spec(A_WIDTH)]
                         + [_const_spec(c.shape) for c in consts],
                out_shape=[jax.ShapeDtypeStruct(x.shape, x.dtype)], scratch=[], args=(x, hf, hb, ya, *consts))


def _final(*operands):
    parts = _final_parts(*operands, FINAL_ROWS)
    return pl.pallas_call(
        _final_body,
        grid=parts["grid"],
        in_specs=parts["in_specs"],
        out_specs=parts["out_specs"],
        out_shape=parts["out_shape"],
        compiler_params=pltpu.CompilerParams(
            dimension_semantics=("parallel", "parallel"), vmem_limit_bytes=VMEM_LIMIT_BYTES),
        name="final",
    )(*parts["args"])[0]


def _t5_bucket(rel):
    nb = N_BUCKETS // 2
    exact = nb // 2
    n = np.abs(rel)
    large = exact + (np.log(np.maximum(n, 1) / exact) / math.log(MAX_DISTANCE / exact) * (nb - exact)).astype(np.int32)
    large = np.minimum(large, nb - 1)
    return (rel > 0).astype(np.int32) * nb + np.where(n < exact, n, large)


def _attn_bias(rel_table, g, dil):
    win = ATT_QBLOCK + 2 * A_HALF
    off = np.arange(win)[None, :] - A_HALF - np.arange(ATT_QBLOCK)[:, None]
    band = np.abs(off) <= A_HALF
    bucket = np.where(band, _t5_bucket(off * dil), -1)
    onehot = jnp.asarray(bucket[None] == np.arange(N_BUCKETS)[:, None, None])
    table = rel_table.astype(f32)[:, g, :]
    bias = jnp.sum(jnp.where(onehot[:, None], table[:, :, None, None], 0.0), axis=0)
    bias = jnp.where(band[None], bias * LOG2E, NEG)
    return bias.reshape(A_HEADS // 2, 2 * ATT_QBLOCK, win)


def _head_expand_matrix():
    e = np.zeros((2 * LANES, A_WIDTH), np.float32)
    for h in range(A_HEADS):
        e[h, h * A_HEAD_DIM:(h + 1) * A_HEAD_DIM] = 1.0
        e[LANES + h, h * A_HEAD_DIM:(h + 1) * A_HEAD_DIM] = 1.0
    return jnp.asarray(e, bf16)


def _proj_operands(x, p):
    return (x, p["pre_w"], p["conv_w"], p["conv_b"], p["wqk"], p["wv"], p["wift"], p["bias_r"], p["wa"])


def _attention(a, p):
    dilated = []
    for g, (_, dil) in enumerate(A_PATTERNS):
        if dil > 1:
            dilated += _attn_dilated(a[g], p["attn_bias"][g], dil)
    return _attn_mix(a[0], p["attn_bias"][0], *dilated, _head_expand_matrix())


def _final_operands(x, hf, hb, ya, p):
    return (x, hf, hb, ya, p["pre_w"], p["w4"], p["hnw"], p["wpm"], p["wpa"], p["wout"], p["post_w"])


def _layer_pair(x_first, x_second, p):
    *scan_in, a_first = _split_proj(_proj(*_proj_operands(x_first, p)))
    proj_second, (hf, hb) = _with_scan("proj_mlstm", _proj_parts(*_proj_operands(x_second, p)), _proj_steps,
                                       PROJ_WEIGHT, tuple(scan_in))
    *scan_in, a_second = _split_proj(proj_second)
    final_first = _final_parts(*_final_operands(x_first, hf, hb, _attention(a_first, p), p), FUSED_FINAL_ROWS)
    (y_first,), (hf, hb) = _with_scan("final_mlstm", final_first, _final_steps, FINAL_WEIGHT, tuple(scan_in))
    return y_first, _final(*_final_operands(x_second, hf, hb, _attention(a_second, p), p))


def _split_proj(outs):
    q, kt, v, grow, *a = outs
    return q, kt, v, grow, a


def kernel(x_prompt, x_sample, pre_norm_w, w_in, m_conv_w, m_conv_b, m_igate_b, m_fgate_b, m_head_norm_w,
           w_proj_m, w_proj_a, w_out, post_norm_w, rel_bias_table):
    depth = pre_norm_w.shape[0]
    params = []
    for l in range(depth):
        w = w_in[l].astype(bf16)
        n_query = N_GROUPS * A_WIDTH
        wq = (w_in[l][:, OFF_AQKV:OFF_AQKV + n_query] * (A_HEAD_DIM ** -0.5 * LOG2E)).astype(bf16)
        wa = jnp.concatenate([wq, w[:, OFF_AQKV + n_query:OFF_AZ]], axis=1)
        gate_b = jnp.concatenate([m_igate_b[l].reshape(-1), m_fgate_b[l].reshape(-1)]).astype(f32)
        params.append(dict(
            pre_w=pre_norm_w[l].reshape(1, D_MODEL), post_w=post_norm_w[l].reshape(1, D_MODEL),
            conv_w=m_conv_w[l], conv_b=m_conv_b[l].reshape(1, 2 * M_WIDTH),
            wqk=w[:, OFF_QK:OFF_V], wv=w[:, OFF_V:OFF_O],
            wift=w[:, OFF_I:OFF_AQKV].T, wa=wa,
            bias_r=gate_b.reshape(4 * M_HEADS, 1),
            w4=jnp.concatenate([w[:, OFF_O:OFF_I], w[:, OFF_AZ:]], axis=1),
            hnw=m_head_norm_w[l].reshape(1, M_WIDTH),
            wpm=w_proj_m[l].astype(bf16), wpa=w_proj_a[l].astype(bf16), wout=w_out[l].astype(bf16),
            attn_bias=[_attn_bias(rel_bias_table, g, dil) for g, (_, dil) in enumerate(A_PATTERNS)],
        ))

    for p in params:
        x_prompt, x_sample = _layer_pair(x_prompt, x_sample, p)
    return (x_prompt, x_sample)
```
